```python
import math
import jax
import jax.numpy as jnp
from jax import lax
import numpy as np

D_MODEL = 2048
BATCH = 2
SEQ = 4096
DEPTH = 2
DEC_BATCH = 32
DEC_SEQ = 4
PAST_LEN = 8192
PAGE_SIZE = 128

N_MIXERS = 2
N_NSA_LAYERS = (DEPTH + 1) // 2
N_RWKV_LAYERS = DEPTH // 2
N_HEADS = 16
HEAD_DIM = D_MODEL // N_HEADS
N_KV_HEADS = 4
GROUP = N_HEADS // N_KV_HEADS
CMP_BLOCK = 32
CMP_STRIDE = 16
SEL_BLOCK = 64
N_SEL = 16
WINDOW = 512
Q_BLOCK = 128
NSA_IN = N_HEADS * HEAD_DIM + 6 * N_KV_HEADS * HEAD_DIM + 3 * N_HEADS
RWKV_HEAD = 64
RWKV_HEADS = D_MODEL // RWKV_HEAD
DECAY_LORA = max(32, int(round(1.8 * D_MODEL ** 0.5 / 32)) * 32)
AAA_LORA = max(32, int(round(1.8 * D_MODEL ** 0.5 / 32)) * 32)
GATE_LORA = max(32, int(round(0.6 * D_MODEL ** 0.8 / 32)) * 32)
D_FF = 4 * D_MODEL
ALPHA = (2 * DEPTH) ** 0.25
BETA = (8 * DEPTH) ** -0.25
LN_EPS = 1e-5
LNX_EPS = 64e-5
FORCE = 1e4
NEG_INF = -1e30
TINY = 1e-30

kernel_name = 'nsa_rwkv7_deepnorm_decode_step'


def layer_norm(x, g, b):
    xf = x.astype(jnp.float32)
    xc = xf - jnp.mean(xf, -1, keepdims=True)
    var = jnp.mean(xc * xc, -1, keepdims=True)
    return (xc * lax.rsqrt(var + LN_EPS) * g + b).astype(x.dtype)


def post_norm(x, y, g, b):
    return layer_norm(ALPHA * x + y, g, b)


def sq_relu_mlp(x, w_up, w_down):
    h = jax.nn.relu(x @ w_up)
    return (h * h) @ w_down


def masked_softmax(s, mask, axes):
    s = jnp.where(mask, s, NEG_INF)
    m = jnp.max(s, axis=axes, keepdims=True)
    p = jnp.where(mask, jnp.exp(s - m), 0.0)
    return p / jnp.maximum(jnp.sum(p, axis=axes, keepdims=True), TINY)


def overlap_matrix(n_c, n_s):
    c0 = jnp.arange(n_c)[:, None] * CMP_STRIDE
    s0 = jnp.arange(n_s)[None, :] * SEL_BLOCK
    return ((c0 < s0 + SEL_BLOCK) & (c0 + CMP_BLOCK > s0)).astype(jnp.float32)


def compress_kv(kv, pe, w1, w2):
    bsz, length = kv.shape[:2]
    n_c = (length - CMP_BLOCK) // CMP_STRIDE + 1
    n_r = CMP_BLOCK // CMP_STRIDE
    n_chunk = n_c + n_r - 1
    chunks = kv[:, :n_chunk * CMP_STRIDE].reshape(bsz, n_chunk, CMP_STRIDE, 2, N_KV_HEADS, HEAD_DIM)
    w1r = w1.reshape(2, n_r, CMP_STRIDE, HEAD_DIM, HEAD_DIM)
    part = jnp.einsum('bnjpkd,prjdh->bnrpkh', chunks, w1r)
    bias = jnp.einsum('prjd,prjdh->ph', pe.reshape(2, n_r, CMP_STRIDE, HEAD_DIM), w1r)
    h = bias[None, None, :, None, :]
    for r in range(n_r):
        h = h + part[:, r:r + n_c, r]
    return jnp.einsum('bcpkh,phe->bcpke', jax.nn.gelu(h), w2)


def nsa_project(x, w_in):
    bsz, t = x.shape[:2]
    y = x @ w_in
    nq = N_HEADS * HEAD_DIM
    nkv = N_KV_HEADS * HEAD_DIM
    q = y[..., :nq].reshape(bsz, t, N_KV_HEADS, GROUP, HEAD_DIM)
    kvp = y[..., nq:nq + 4 * nkv].reshape(bsz, t, 4, N_KV_HEADS, HEAD_DIM)
    kvw = y[..., nq + 4 * nkv:nq + 6 * nkv].reshape(bsz, t, 2, N_KV_HEADS, HEAD_DIM)
    g = jax.nn.sigmoid(y[..., nq + 6 * nkv:].astype(jnp.float32)).astype(x.dtype)
    return q, kvp, kvw, g.reshape(bsz, t, N_KV_HEADS, GROUP, 3)


def nsa_attend(q, gates, qpos, kc, vc, n_s, gather_fn, kw, vw, kwpos):
    scale = HEAD_DIM ** -0.5
    n_c = kc.shape[1]
    cmp_end = jnp.arange(n_c) * CMP_STRIDE + CMP_BLOCK - 1
    s = jnp.einsum('btkgd,bckd->btkgc', q, kc).astype(jnp.float32) * scale
    p_c = masked_softmax(s, (cmp_end[None, :] <= qpos[:, None])[None, :, None, None, :], (-1,))
    o_c = jnp.einsum('btkgc,bckd->btkgd', p_c.astype(vc.dtype), vc)
    imp = jnp.einsum('btkc,cs->btks', jnp.sum(p_c, axis=3), overlap_matrix(n_c, n_s))
    j = jnp.arange(n_s)[None, :]
    cur = qpos[:, None] // SEL_BLOCK
    forced = ((j == 0) | (j == cur) | (j == cur - 1))[None, :, None, :]
    valid = (j * SEL_BLOCK <= qpos[:, None])[None, :, None, :]
    score = jnp.where(forced, FORCE, jnp.where(valid, imp, -FORCE))
    _, idx = lax.top_k(score, min(N_SEL, n_s))
    ks, vs = gather_fn(idx)
    kpos = idx[..., None] * SEL_BLOCK + jnp.arange(SEL_BLOCK)
    m_s = (kpos <= qpos[None, :, None, None, None])[:, :, :, None]
    s = jnp.einsum('btkgd,btknrd->btkgnr', q, ks).astype(jnp.float32) * scale
    p_s = masked_softmax(s, m_s, (-2, -1))
    o_s = jnp.einsum('btkgnr,btknrd->btkgd', p_s.astype(vs.dtype), vs)
    dpos = qpos[:, None] - kwpos[None, :]
    m_w = ((dpos >= 0) & (dpos < WINDOW) & (kwpos[None, :] >= 0))[None, :, None, None, :]
    s = jnp.einsum('btkgd,bnkd->btkgn', q, kw).astype(jnp.float32) * scale
    p_w = masked_softmax(s, m_w, (-1,))
    o_w = jnp.einsum('btkgn,bnkd->btkgd', p_w.astype(vw.dtype), vw)
    return gates[..., 0:1] * o_c + gates[..., 1:2] * o_s + gates[..., 2:3] * o_w


def nsa_prompt(x, w_in, pe, w1, w2, w_out):
    bsz, t = x.shape[:2]
    q, kvp, kvw, g = nsa_project(x, w_in)
    cmp = compress_kv(kvp[:, :, 0:2], pe, w1, w2)
    kc, vc = cmp[:, :, 0], cmp[:, :, 1]
    n_s = t // SEL_BLOCK
    bn = jnp.arange(bsz)[:, None, None, None, None]
    kvi = jnp.arange(N_KV_HEADS)[:, None, None]
    sb = jnp.arange(SEL_BLOCK)

    def gather_fn(idx):
        rows = idx[..., None] * SEL_BLOCK + sb
        return kvp[bn, rows, 2, kvi], kvp[bn, rows, 3, kvi]

    kw_pad = jnp.pad(kvw, ((0, 0), (WINDOW, 0), (0, 0), (0, 0), (0, 0)))
    nq = t // Q_BLOCK

    def block(args):
        qb, gb, i = args
        start = i * Q_BLOCK
        kwb = lax.dynamic_slice_in_dim(kw_pad, start, WINDOW + Q_BLOCK, axis=1)
        qpos = start + jnp.arange(Q_BLOCK)
        kwpos = start - WINDOW + jnp.arange(WINDOW + Q_BLOCK)
        return nsa_attend(qb, gb, qpos, kc, vc, n_s, gather_fn, kwb[:, :, 0], kwb[:, :, 1], kwpos)

    qs = q.reshape(bsz, nq, Q_BLOCK, N_KV_HEADS, GROUP, HEAD_DIM).swapaxes(0, 1)
    gs = g.reshape(bsz, nq, Q_BLOCK, N_KV_HEADS, GROUP, 3).swapaxes(0, 1)
    o = lax.map(block, (qs, gs, jnp.arange(nq)))
    o = o.swapaxes(0, 1).reshape(bsz, t, N_HEADS * HEAD_DIM)
    return o @ w_out, kvp, kvw[:, t - min(WINDOW, t):]


def nsa_sample(x, cache, layer, win, page_table, w_in, pe, w1, w2, w_out):
    db, s_len = x.shape[:2]
    q, kvp, kvw, g = nsa_project(x, w_in)
    n_pages = page_table.shape[1]
    past = n_pages * PAGE_SIZE
    past_cmp = cache[layer, page_table[:, :, None, None], jnp.arange(PAGE_SIZE)[:, None], jnp.arange(2)]
    past_cmp = past_cmp.reshape(db, past, 2, N_KV_HEADS, HEAD_DIM)
    cmp = compress_kv(jnp.concatenate([past_cmp, kvp[:, :, 0:2]], axis=1), pe, w1, w2)
    kc, vc = cmp[:, :, 0], cmp[:, :, 1]
    n_s = -(-(past + s_len) // SEL_BLOCK)
    nbp = past // SEL_BLOCK
    bpp = PAGE_SIZE // SEL_BLOCK
    bi = jnp.arange(db)[:, None, None, None]
    bn = bi[..., None]
    kvi = jnp.arange(N_KV_HEADS)[:, None, None]
    sb = jnp.arange(SEL_BLOCK)

    def gather_fn(idx):
        jp = jnp.minimum(idx, nbp - 1)
        pages = page_table[bi, jp // bpp][..., None]
        prow = (jp % bpp)[..., None] * SEL_BLOCK + sb
        nrow = jnp.clip((idx - nbp)[..., None] * SEL_BLOCK + sb, 0, s_len - 1)
        is_past = (idx < nbp)[..., None, None]
        ks = jnp.where(is_past, cache[layer, pages, prow, 2, kvi], kvp[bn, nrow, 2, kvi])
        vs = jnp.where(is_past, cache[layer, pages, prow, 3, kvi], kvp[bn, nrow, 3, kvi])
        return ks, vs

    kw_all = jnp.concatenate([win.astype(kvw.dtype), kvw], axis=1)
    wb = win.shape[1]
    kwpos = past - wb + jnp.arange(wb + s_len)
    qpos = past + jnp.arange(s_len)
    o = nsa_attend(q, g, qpos, kc, vc, n_s, gather_fn, kw_all[:, :, 0], kw_all[:, :, 1], kwpos)
    o = o.reshape(db, s_len, N_HEADS * HEAD_DIM)
    return o @ w_out, kvp, kw_all[:, s_len:]


def rwkv_time_mix(x, shift, s0, mu, w_rkv, w0, w1, w2, a0, a1, a2, g1, g2, k_k, k_a, r_k, lnx_g, lnx_b, w_out):
    f32 = jnp.float32
    bsz, t, d = x.shape
    xx = jnp.concatenate([shift[:, None].astype(x.dtype), x[:, :-1]], axis=1) - x
    xm = x[None] + xx[None] * mu[:, None, None, :]
    rkv = jnp.einsum('pbtd,pde->pbte', xm[:3], w_rkv)
    r, k, v = rkv[0], rkv[1], rkv[2]
    w_log = -jax.nn.softplus(-(w0 + jnp.tanh(xm[3] @ w1) @ w2).astype(f32)) - 0.5
    decay = jnp.exp(-jnp.exp(w_log))
    a = jax.nn.sigmoid((a0 + (xm[4] @ a1) @ a2).astype(f32))
    g = jax.nn.sigmoid(xm[5] @ g1) @ g2

    def heads(z):
        return z.reshape(bsz, t, RWKV_HEADS, RWKV_HEAD)

    kk = heads((k * k_k).astype(f32))
    kk = kk / jnp.maximum(jnp.sqrt(jnp.sum(kk * kk, -1, keepdims=True)), 1e-12)
    k = k.astype(f32) * (1.0 + (a - 1.0) * k_a)
    r_h, k_h, v_h = heads(r.astype(f32)), heads(k), heads(v.astype(f32))
    w_h, a_h = heads(decay), heads(a)

    def step(st, inp):
        r_t, w_t, k_t, v_t, kk_t, b_t = inp
        sa = jnp.einsum('bhij,bhj->bhi', st, -kk_t)
        st = st * w_t[:, :, None, :] + sa[..., None] * b_t[:, :, None, :] + v_t[..., None] * k_t[:, :, None, :]
        return st, jnp.einsum('bhij,bhj->bhi', st, r_t)

    def tm(z):
        return z.swapaxes(0, 1)

    s_fin, y = lax.scan(step, s0.astype(f32), (tm(r_h), tm(w_h), tm(k_h), tm(v_h), tm(kk), tm(kk * a_h)))
    y = y.swapaxes(0, 1)
    yc = y - jnp.mean(y, -1, keepdims=True)
    y = yc * lax.rsqrt(jnp.mean(yc * yc, -1, keepdims=True) + LNX_EPS)
    y = y.reshape(bsz, t, d) * lnx_g + lnx_b
    y = y + (jnp.sum(r_h * k_h * r_k, -1, keepdims=True) * v_h).reshape(bsz, t, d)
    return (y.astype(x.dtype) * g) @ w_out, s_fin, x[:, -1]


def setup_inputs(seed: int = 0) -> dict:
    key = jax.random.key(seed)
    ks = iter(jax.random.split(key, 40))
    f32 = jnp.float32

    def nrm(shape, scale=1.0):
        return jax.random.normal(next(ks), shape, f32) * scale

    def unif(shape, lo, hi):
        return jax.random.uniform(next(ks), shape, f32, lo, hi)

    n_pages = PAST_LEN // PAGE_SIZE
    n_used = DEC_BATCH * n_pages
    n_pool = n_used + max(1, n_used // 4)
    win_buf = min(WINDOW, PAST_LEN)
    hd_all = N_HEADS * HEAD_DIM
    nn_, nr = N_NSA_LAYERS, N_RWKV_LAYERS
    d = D_MODEL
    return {
        'x_prompt': nrm((BATCH, SEQ, d)),
        'x_sample': nrm((DEC_BATCH, DEC_SEQ, d)),
        'cache_nsa_kv': nrm((nn_, n_pool, PAGE_SIZE, 4, N_KV_HEADS, HEAD_DIM)),
        'state_nsa_win': nrm((nn_, DEC_BATCH, win_buf, 2, N_KV_HEADS, HEAD_DIM)),
        'state_rwkv_wkv': nrm((nr, DEC_BATCH, RWKV_HEADS, RWKV_HEAD, RWKV_HEAD), 0.3),
        'state_rwkv_shift': nrm((nr, DEC_BATCH, d)),
        'page_table': jax.random.permutation(next(ks), n_pool)[:n_used].reshape(DEC_BATCH, n_pages).astype(jnp.int32),
        'nsa_w_in': nrm((nn_, d, NSA_IN), d ** -0.5),
        'nsa_cmp_pe': nrm((nn_, 2, CMP_BLOCK, HEAD_DIM), 0.5),
        'nsa_cmp_w1': nrm((nn_, 2, CMP_BLOCK, HEAD_DIM, HEAD_DIM), (CMP_BLOCK * HEAD_DIM) ** -0.5),
        'nsa_cmp_w2': nrm((nn_, 2, HEAD_DIM, HEAD_DIM), HEAD_DIM ** -0.5),
        'nsa_w_out': nrm((nn_, hd_all, d), hd_all ** -0.5 * BETA),
        'rwkv_mu': unif((nr, 6, d), 0.0, 1.0),
        'rwkv_w_rkv': nrm((nr, 3, d, d), d ** -0.5),
        'rwkv_w0': unif((nr, d), -5.0, 0.0),
        'rwkv_w1': nrm((nr, d, DECAY_LORA), d ** -0.5),
        'rwkv_w2': nrm((nr, DECAY_LORA, d), 0.1 * DECAY_LORA ** -0.5),
        'rwkv_a0': nrm((nr, d), 0.1),
        'rwkv_a1': nrm((nr, d, AAA_LORA), d ** -0.5),
        'rwkv_a2': nrm((nr, AAA_LORA, d), 0.3 * AAA_LORA ** -0.5),
        'rwkv_g1': nrm((nr, d, GATE_LORA), d ** -0.5),
        'rwkv_g2': nrm((nr, GATE_LORA, d), GATE_LORA ** -0.5),
        'rwkv_k_k': 0.85 + nrm((nr, d), 0.05),
        'rwkv_k_a': 1.0 + nrm((nr, d), 0.05),
        'rwkv_r_k': nrm((nr, RWKV_HEADS, RWKV_HEAD), 0.1),
        'rwkv_lnx_g': 1.0 + nrm((nr, d), 0.05),
        'rwkv_lnx_b': nrm((nr, d), 0.02),
        'rwkv_w_out': nrm((nr, d, d), d ** -0.5 * BETA),
        'ffn_w_up': nrm((DEPTH, d, D_FF), d ** -0.5),
        'ffn_w_down': nrm((DEPTH, D_FF, d), D_FF ** -0.5 * BETA),
        'ln_g': 1.0 + nrm((DEPTH, 2, d), 0.05),
        'ln_b': nrm((DEPTH, 2, d), 0.02),
    }


def reference(x_prompt, x_sample, cache_nsa_kv, state_nsa_win, state_rwkv_wkv, state_rwkv_shift, page_table,
              nsa_w_in, nsa_cmp_pe, nsa_cmp_w1, nsa_cmp_w2, nsa_w_out,
              rwkv_mu, rwkv_w_rkv, rwkv_w0, rwkv_w1, rwkv_w2, rwkv_a0, rwkv_a1, rwkv_a2, rwkv_g1, rwkv_g2,
              rwkv_k_k, rwkv_k_a, rwkv_r_k, rwkv_lnx_g, rwkv_lnx_b, rwkv_w_out,
              ffn_w_up, ffn_w_down, ln_g, ln_b):
    xp, xs = x_prompt, x_sample
    kv_p, kv_s, win_p, win_s = [], [], [], []
    wkv_p, wkv_s, sh_p, sh_s = [], [], [], []
    for i in range(DEPTH):
        j = i // N_MIXERS
        if i % N_MIXERS == 0:
            wts = (nsa_w_in[j], nsa_cmp_pe[j], nsa_cmp_w1[j], nsa_cmp_w2[j], nsa_w_out[j])
            yp, kvp_new, wp_new = nsa_prompt(xp, *wts)
            ys, kvs_new, ws_new = nsa_sample(xs, cache_nsa_kv, j, state_nsa_win[j], page_table, *wts)
            kv_p.append(kvp_new)
            kv_s.append(kvs_new)
            win_p.append(wp_new)
            win_s.append(ws_new)
        else:
            wts = (rwkv_mu[j], rwkv_w_rkv[j], rwkv_w0[j], rwkv_w1[j], rwkv_w2[j], rwkv_a0[j], rwkv_a1[j],
                   rwkv_a2[j], rwkv_g1[j], rwkv_g2[j], rwkv_k_k[j], rwkv_k_a[j], rwkv_r_k[j],
                   rwkv_lnx_g[j], rwkv_lnx_b[j], rwkv_w_out[j])
            s0 = jnp.zeros((xp.shape[0], RWKV_HEADS, RWKV_HEAD, RWKV_HEAD), jnp.float32)
            yp, sp_new, hp_new = rwkv_time_mix(xp, jnp.zeros_like(xp[:, 0]), s0, *wts)
            ys, ss_new, hs_new = rwkv_time_mix(xs, state_rwkv_shift[j], state_rwkv_wkv[j], *wts)
            wkv_p.append(sp_new)
            wkv_s.append(ss_new)
            sh_p.append(hp_new)
            sh_s.append(hs_new)
        xp = post_norm(xp, yp, ln_g[i, 0], ln_b[i, 0])
        xs = post_norm(xs, ys, ln_g[i, 0], ln_b[i, 0])
        xp = post_norm(xp, sq_relu_mlp(xp, ffn_w_up[i], ffn_w_down[i]), ln_g[i, 1], ln_b[i, 1])
        xs = post_norm(xs, sq_relu_mlp(xs, ffn_w_up[i], ffn_w_down[i]), ln_g[i, 1], ln_b[i, 1])
    return (xp, xs, jnp.stack(kv_p), jnp.stack(kv_s), jnp.stack(win_p), jnp.stack(win_s),
            jnp.stack(wkv_p), jnp.stack(wkv_s), jnp.stack(sh_p), jnp.stack(sh_s))
```

```python
import functools
import math

import jax
import jax.numpy as jnp
import numpy as np
from jax import lax
from jax.experimental import pallas as pl
from jax.experimental.pallas import tpu as pltpu

F32 = jnp.float32
BF16 = jnp.bfloat16

HEAD_DIM = 128
N_KV_HEADS = 4
GROUP = 4
N_HEADS = N_KV_HEADS * GROUP
QW = GROUP * HEAD_DIM
NQ_COLS = N_HEADS * HEAD_DIM
NKV_COLS = N_KV_HEADS * HEAD_DIM
CMP_BLOCK = 32
CMP_STRIDE = 16
SEL_BLOCK = 64
N_SEL = 16
WINDOW = 512
Q_BLOCK = 128
PAGE_SIZE = 128
RWKV_HEAD = 64
LN_EPS = 1e-5
LNX_EPS = 64e-5
FORCE = 1e4
NEG_INF = -1e30
TINY = 1e-30
CHUNKS_PER_PAGE = PAGE_SIZE // CMP_STRIDE

VMEM_LIMIT_BYTES = 56 * 1024 * 1024
HI = lax.Precision.HIGHEST


def _params(*sem):
    return pltpu.CompilerParams(dimension_semantics=sem, vmem_limit_bytes=VMEM_LIMIT_BYTES)


def _tile(n, pref, mult=16):
    if n <= pref:
        return n
    for t in range(pref - pref % mult, 0, -mult):
        if n % t == 0:
            return t
    return n


def _dot(a, b, precision=None):
    return jnp.dot(a, b, preferred_element_type=F32, precision=precision)


def _dot_nt(a, b, precision=None):
    return lax.dot_general(a, b, (((1,), (1,)), ((), ())), preferred_element_type=F32, precision=precision)


def _dot_tn(a, b, precision=None):
    return lax.dot_general(a, b, (((0,), (0,)), ((), ())), preferred_element_type=F32, precision=precision)


def _masked_softmax(s, mask):
    s = jnp.where(mask, s, NEG_INF)
    m = jnp.max(s, axis=-1, keepdims=True)
    p = jnp.where(mask, jnp.exp(s - m), 0.0)
    return p / jnp.maximum(jnp.sum(p, axis=-1, keepdims=True), TINY)


def _layer_norm_rows(s, g, b, eps):
    mu = jnp.mean(s, axis=-1, keepdims=True)
    c = s - mu
    var = jnp.mean(c * c, axis=-1, keepdims=True)
    return c * lax.rsqrt(var + eps) * g + b


def _mm_kernel(x_ref, w_ref, o_ref, xb_ref, *, act):
    @pl.when(pl.program_id(1) == 0)
    def _():
        xb_ref[...] = x_ref[...].astype(BF16)

    y = _dot(xb_ref[...], w_ref[...].astype(BF16))
    if act == "relu2":
        y = jnp.maximum(y, 0.0)
        y = y * y
    elif act == "sigmoid":
        y = jax.nn.sigmoid(y)
    o_ref[...] = y.astype(o_ref.dtype)


def _mm(x, w, n_out, *, act=None, out_dtype=F32, tm_pref=1040, tn_pref=1024):
    m, k = x.shape
    tm = _tile(m, tm_pref)
    tn = _tile(n_out, tn_pref, 128)
    return pl.pallas_call(
        functools.partial(_mm_kernel, act=act),
        grid=(m // tm, n_out // tn),
        in_specs=[pl.BlockSpec((tm, k), lambda i, j: (i, 0)),
                  pl.BlockSpec((k, tn), lambda i, j: (0, j))],
        out_specs=pl.BlockSpec((tm, tn), lambda i, j: (i, j)),
        out_shape=jax.ShapeDtypeStruct((m, n_out), out_dtype),
        scratch_shapes=[pltpu.VMEM((tm, k), BF16)],
        compiler_params=_params("parallel", "arbitrary"),
    )(x, w)


def _mm_mix_kernel(x_ref, xp_ref, mu_ref, w_ref, o_ref, xb_ref):
    @pl.when(pl.program_id(2) == 0)
    def _():
        x = x_ref[...]
        xb_ref[...] = (x + (xp_ref[...] - x) * mu_ref[...]).astype(BF16)

    o_ref[...] = _dot(xb_ref[...], w_ref[...].astype(BF16))


def _mm_mix(x, xp, mu, w, *, tm_pref=640, tn_pref=1024):
    m, k = x.shape
    npar, _, n = w.shape
    tm = _tile(m, tm_pref)
    tn = _tile(n, tn_pref, 128)
    return pl.pallas_call(
        _mm_mix_kernel,
        grid=(npar, m // tm, n // tn),
        in_specs=[pl.BlockSpec((tm, k), lambda p, i, j: (i, 0)),
                  pl.BlockSpec((tm, k), lambda p, i, j: (i, 0)),
                  pl.BlockSpec((None, 1, k), lambda p, i, j: (p, 0, 0)),
                  pl.BlockSpec((None, k, tn), lambda p, i, j: (p, 0, j))],
        out_specs=pl.BlockSpec((None, tm, tn), lambda p, i, j: (p, i, j)),
        out_shape=jax.ShapeDtypeStruct((npar, m, n), F32),
        scratch_shapes=[pltpu.VMEM((tm, k), BF16)],
        compiler_params=_params("parallel", "parallel", "arbitrary"),
    )(x, xp, mu, w)


def _lora_kernel(x_ref, xp_ref, mu_ref, a_ref, b_ref, bias_ref, o_ref, *, kind):
    x = x_ref[...]
    xm = (x + (xp_ref[...] - x) * mu_ref[...]).astype(BF16)
    h = _dot(xm, a_ref[...].astype(BF16))
    if kind == "decay":
        h = jnp.tanh(h)
    elif kind == "gate":
        h = jax.nn.sigmoid(h)
    z = _dot(h.astype(BF16), b_ref[...].astype(BF16))
    if kind == "decay":
        u = -(z + bias_ref[...])
        softplus = jnp.maximum(u, 0.0) + jnp.log(1.0 + jnp.exp(-jnp.abs(u)))
        z = -jnp.exp(-softplus - 0.5)
    elif kind == "aaa":
        z = jax.nn.sigmoid(z + bias_ref[...])
    o_ref[...] = z


def _lora(x, xp, mu, a, b, bias, kind, *, tm_pref=320):
    m, k = x.shape
    r = a.shape[1]
    n = b.shape[1]
    tm = _tile(m, tm_pref)
    row = lambda i: (i, 0)
    fixed = lambda i: (0, 0)
    return pl.pallas_call(
        functools.partial(_lora_kernel, kind=kind),
        grid=(m // tm,),
        in_specs=[pl.BlockSpec((tm, k), row), pl.BlockSpec((tm, k), row), pl.BlockSpec((1, k), fixed),
                  pl.BlockSpec((k, r), fixed), pl.BlockSpec((r, n), fixed), pl.BlockSpec((1, n), fixed)],
        out_specs=pl.BlockSpec((tm, n), row),
        out_shape=jax.ShapeDtypeStruct((m, n), F32),
        compiler_params=_params("parallel"),
    )(x, xp, mu, a, b, bias)


def _mm_ln_kernel(z_ref, w_ref, res_ref, g_ref, b_ref, o_ref, *, alpha):
    kk = pl.program_id(1)
    part = _dot(z_ref[...].astype(BF16), w_ref[...].astype(BF16))

    @pl.when(kk == 0)
    def _():
        o_ref[...] = part

    @pl.when(kk > 0)
    def _():
        o_ref[...] += part

    @pl.when(kk == pl.num_programs(1) - 1)
    def _():
        s = alpha * res_ref[...] + o_ref[...]
        o_ref[...] = _layer_norm_rows(s, g_ref[...], b_ref[...], LN_EPS)


def _mm_ln(z, w, res, g, b, alpha, *, tm_pref=640, tk_pref=512):
    m, k = z.shape
    n = w.shape[1]
    tm = _tile(m, tm_pref)
    tk = _tile(k, tk_pref, 128)
    return pl.pallas_call(
        functools.partial(_mm_ln_kernel, alpha=alpha),
        grid=(m // tm, k // tk),
        in_specs=[pl.BlockSpec((tm, tk), lambda i, kk: (i, kk)),
                  pl.BlockSpec((tk, n), lambda i, kk: (kk, 0)),
                  pl.BlockSpec((tm, n), lambda i, kk: (i, 0)),
                  pl.BlockSpec((1, n), lambda i, kk: (0, 0)),
                  pl.BlockSpec((1, n), lambda i, kk: (0, 0))],
        out_specs=pl.BlockSpec((tm, n), lambda i, kk: (i, 0)),
        out_shape=jax.ShapeDtypeStruct((m, n), F32),
        compiler_params=_params("parallel", "arbitrary"),
    )(z, w, res, g, b)


def _cmp_bias_kernel(pe_ref, w1_ref, o_ref):
    acc = jnp.zeros((1, HEAD_DIM), F32)
    for rj in range(CMP_BLOCK):
        acc = acc + _dot(pe_ref[rj:rj + 1, :], w1_ref[rj], HI)
    o_ref[...] = acc


def _cmp_bias(pe, w1):
    return pl.pallas_call(
        _cmp_bias_kernel,
        grid=(2,),
        in_specs=[pl.BlockSpec((None, CMP_BLOCK, HEAD_DIM), lambda p: (p, 0, 0)),
                  pl.BlockSpec((None, CMP_BLOCK, HEAD_DIM, HEAD_DIM), lambda p: (p, 0, 0, 0))],
        out_specs=pl.BlockSpec((None, 1, HEAD_DIM), lambda p: (p, 0, 0)),
        out_shape=jax.ShapeDtypeStruct((2, 1, HEAD_DIM), F32),
        compiler_params=_params("parallel"),
    )(pe, w1)


def _cmp_kernel(*refs, n_in):
    pages = refs[-5 - n_in:-5]
    wc_ref, bias_ref, w2_ref, out_ref, carry_ref = refs[-5:]
    mk = CHUNKS_PER_PAGE * n_in
    rows = N_KV_HEADS * mk

    @pl.when(pl.program_id(1) == 0)
    def _():
        carry_ref[...] = jnp.zeros_like(carry_ref)

    first = (lax.broadcasted_iota(jnp.int32, (rows, 1), 0) % mk) == 0
    r_out = lax.broadcasted_iota(jnp.int32, (PAGE_SIZE, PAGE_SIZE), 0)
    r_in = lax.broadcasted_iota(jnp.int32, (PAGE_SIZE, PAGE_SIZE), 1)
    perm = jnp.where(r_in == (r_out % CHUNKS_PER_PAGE) * CMP_STRIDE + r_out // CHUNKS_PER_PAGE, 1.0, 0.0)
    perm = perm.astype(BF16)
    grouped = [_dot(perm, pages[i][...].astype(BF16)) for i in range(n_in)]
    for p in range(2):
        acc = jnp.zeros((rows, 2 * HEAD_DIM), F32)
        for j in range(CMP_STRIDE):
            lhs = jnp.concatenate(
                [grouped[i][j * CHUNKS_PER_PAGE:(j + 1) * CHUNKS_PER_PAGE,
                            (p * N_KV_HEADS + k) * HEAD_DIM:(p * N_KV_HEADS + k + 1) * HEAD_DIM]
                 for k in range(N_KV_HEADS) for i in range(n_in)], axis=0)
            acc = acc + _dot(lhs.astype(BF16), wc_ref[p, j])
        part0 = acc[:, :HEAD_DIM]
        part1 = acc[:, HEAD_DIM:]
        prev = jnp.where(first, carry_ref[p], pltpu.roll(part0, 1, 0))
        carry_ref[p] = pltpu.roll(part0, rows - (mk - 1), 0)
        h = prev + part1 + bias_ref[p]
        c = _dot(jax.nn.gelu(h).astype(BF16), w2_ref[p])
        for k in range(N_KV_HEADS):
            col = (p * N_KV_HEADS + k) * HEAD_DIM
            out_ref[:, col:col + HEAD_DIM] = c[k * mk:(k + 1) * mk]


def _cmp_weights(w1, w2):
    n_r = CMP_BLOCK // CMP_STRIDE
    w1r = w1.reshape(2, n_r, CMP_STRIDE, HEAD_DIM, HEAD_DIM)
    wc = jnp.concatenate([w1r[:, r] for r in range(n_r)], axis=-1).astype(BF16)
    return wc, w2.astype(BF16)


def _cmp_call(n_seq, n_steps, n_in, page_specs, operands, wc, bias, w2b, num_prefetch, prefetch):
    mk = CHUNKS_PER_PAGE * n_in
    if num_prefetch:
        fixed3 = lambda s, t, pt: (0, 0, 0)
        fixed4 = lambda s, t, pt: (0, 0, 0, 0)
        out_map = lambda s, t, pt: (s, t, 0)
    else:
        fixed3 = lambda s, t: (0, 0, 0)
        fixed4 = lambda s, t: (0, 0, 0, 0)
        out_map = lambda s, t: (s, t, 0)
    grid_spec = pltpu.PrefetchScalarGridSpec(
        num_scalar_prefetch=num_prefetch,
        grid=(n_seq, n_steps),
        in_specs=page_specs + [
            pl.BlockSpec((2, CMP_STRIDE, HEAD_DIM, 2 * HEAD_DIM), fixed4),
            pl.BlockSpec((2, 1, HEAD_DIM), fixed3),
            pl.BlockSpec((2, HEAD_DIM, HEAD_DIM), fixed3)],
        out_specs=pl.BlockSpec((None, mk, 2 * NKV_COLS), out_map),
        scratch_shapes=[pltpu.VMEM((2, N_KV_HEADS * mk, HEAD_DIM), F32)],
    )
    return pl.pallas_call(
        functools.partial(_cmp_kernel, n_in=n_in),
        grid_spec=grid_spec,
        out_shape=jax.ShapeDtypeStruct((n_seq, n_steps * mk, 2 * NKV_COLS), F32),
        compiler_params=_params("parallel", "arbitrary"),
    )(*prefetch, *operands, wc, bias, w2b)


CMP_PAGES_PER_STEP = 4


def _cmp_prompt(y_main, bsz, t, wc, bias, w2b):
    n_in = CMP_PAGES_PER_STEP
    n_steps = t // (PAGE_SIZE * n_in)
    pages_per_seq = t // PAGE_SIZE
    col_blk = NQ_COLS // (2 * NKV_COLS)
    specs = [pl.BlockSpec((PAGE_SIZE, 2 * NKV_COLS),
                          functools.partial(lambda s, st, i: (s * pages_per_seq + st * n_in + i, col_blk), i=i))
             for i in range(n_in)]
    return _cmp_call(bsz, n_steps, n_in, specs, [y_main] * n_in, wc, bias, w2b, 0, ())


def _cmp_sample(cache2, layer, page_table, wc, bias, w2b):
    n_in = CMP_PAGES_PER_STEP
    db, n_pages = page_table.shape
    n_steps = n_pages // n_in
    specs = [pl.BlockSpec((None, None, PAGE_SIZE, 2 * NKV_COLS),
                          functools.partial(lambda s, st, pt, i: (layer, pt[s, st * n_in + i], 0, 0), i=i))
             for i in range(n_in)]
    return _cmp_call(db, n_steps, n_in, specs, [cache2] * n_in, wc, bias, w2b, 1, (page_table,))


def _overlap_shifted(n_rows, n_c, n_s, n_cols):
    m = np.arange(n_rows)[:, None]
    j = np.arange(n_cols)[None, :]
    c0 = (m - 1) * CMP_STRIDE
    s0 = j * SEL_BLOCK
    ov = (c0 < s0 + SEL_BLOCK) & (c0 + CMP_BLOCK > s0) & (m >= 1) & (m <= n_c) & (j < n_s)
    return jnp.asarray(ov.astype(np.float32))


def _select_blocks(imp, qpos, n_s, n_cols):
    j = lax.broadcasted_iota(jnp.int32, (1, n_cols), 1)
    jf = j.astype(F32)
    cur = qpos // SEL_BLOCK
    forced = (j == 0) | (j == cur) | (j == cur - 1)
    valid = j * SEL_BLOCK <= qpos
    score = jnp.where(forced, FORCE, jnp.where(valid, imp, -FORCE))
    score = jnp.where(j < n_s, score, -jnp.inf)
    sel = jnp.zeros(score.shape, F32)
    for _ in range(min(N_SEL, n_s)):
        mx = jnp.max(score, axis=-1, keepdims=True)
        idx = jnp.min(jnp.where(score == mx, jf, float(n_cols)), axis=-1, keepdims=True)
        hit = jf == idx
        sel = jnp.where(hit, 1.0, sel)
        score = jnp.where(hit, -jnp.inf, score)
    return sel


SEL_KEY_TILE = 512


def _nsa_prompt_kernel(q_ref, kc_ref, vc_ref, ks_ref, vs_ref, kw_ref, vw_ref, g_ref, ov_ref, o_ref, *, t):
    start = pl.program_id(2) * Q_BLOCK
    n_rows = GROUP * Q_BLOCK
    scale = HEAD_DIM ** -0.5
    q = q_ref[...] * scale
    q4 = jnp.concatenate([q[:, g * HEAD_DIM:(g + 1) * HEAD_DIM] for g in range(GROUP)], axis=0).astype(BF16)
    qpos1 = start + lax.broadcasted_iota(jnp.int32, (Q_BLOCK, 1), 0)
    qpos4 = start + lax.broadcasted_iota(jnp.int32, (n_rows, 1), 0) % Q_BLOCK

    n_cr = kc_ref.shape[0]
    s = _dot_nt(q4, kc_ref[...].astype(BF16))
    m_idx = lax.broadcasted_iota(jnp.int32, (1, n_cr), 1)
    cmask = (m_idx >= 1) & (m_idx * CMP_STRIDE + (CMP_BLOCK - CMP_STRIDE - 1) <= qpos4)
    p_c = _masked_softmax(s, cmask)
    o_c = _dot(p_c.astype(BF16), vc_ref[...].astype(BF16))
    psum = p_c[0:Q_BLOCK]
    for g in range(1, GROUP):
        psum = psum + p_c[g * Q_BLOCK:(g + 1) * Q_BLOCK]
    n_s = ov_ref.shape[1]
    imp = _dot(psum, ov_ref[...], HI)
    sel = _select_blocks(imp, qpos1, n_s, n_s).astype(BF16)

    tk = SEL_KEY_TILE
    blk_row = lax.broadcasted_iota(jnp.int32, (n_s, 1), 0)

    def body(kt, carry):
        m, l, acc = carry
        off = pl.multiple_of(kt * tk, tk)
        k_t = ks_ref[pl.ds(off, tk), :].astype(BF16)
        v_t = vs_ref[pl.ds(off, tk), :].astype(BF16)
        sc = _dot_nt(q4, k_t)
        kpos = off + lax.broadcasted_iota(jnp.int32, (1, tk), 1)
        expand = jnp.where(blk_row == kpos // SEL_BLOCK, 1.0, 0.0).astype(BF16)
        selx = _dot(sel, expand)
        selx4 = jnp.concatenate([selx] * GROUP, axis=0)
        mask = (selx4 > 0.5) & (kpos <= qpos4)
        sc = jnp.where(mask, sc, NEG_INF)
        m_new = jnp.maximum(m, jnp.max(sc, axis=-1, keepdims=True))
        alpha = jnp.exp(m - m_new)
        p = jnp.where(mask, jnp.exp(sc - m_new), 0.0)
        l = alpha * l + jnp.sum(p, axis=-1, keepdims=True)
        acc = alpha * acc + _dot(p.astype(BF16), v_t)
        return m_new, l, acc

    n_tiles = (start + Q_BLOCK + tk - 1) // tk
    m0 = jnp.full((n_rows, 1), NEG_INF, F32)
    l0 = jnp.zeros((n_rows, 1), F32)
    a0 = jnp.zeros((n_rows, HEAD_DIM), F32)
    _, l, acc = lax.fori_loop(0, n_tiles, body, (m0, l0, a0))
    o_s = acc / jnp.maximum(l, TINY)

    wlen = WINDOW + Q_BLOCK
    w0 = pl.multiple_of(jnp.clip(start - WINDOW, 0, t - wlen), Q_BLOCK)
    kw = kw_ref[pl.ds(w0, wlen), :].astype(BF16)
    vw = vw_ref[pl.ds(w0, wlen), :].astype(BF16)
    sw = _dot_nt(q4, kw)
    dpos = qpos4 - (w0 + lax.broadcasted_iota(jnp.int32, (1, wlen), 1))
    p_w = _masked_softmax(sw, (dpos >= 0) & (dpos < WINDOW))
    o_w = _dot(p_w.astype(BF16), vw)

    gt = g_ref[...]
    outs = []
    for g in range(GROUP):
        r0, r1 = g * Q_BLOCK, (g + 1) * Q_BLOCK
        outs.append(gt[:, 3 * g:3 * g + 1] * o_c[r0:r1] + gt[:, 3 * g + 1:3 * g + 2] * o_s[r0:r1]
                    + gt[:, 3 * g + 2:3 * g + 3] * o_w[r0:r1])
    o_ref[...] = jnp.concatenate(outs, axis=1).astype(o_ref.dtype)


def _nsa_prompt(y_main, cmp_p, gates_t, bsz, t):
    assert t % SEL_KEY_TILE == 0 and t >= WINDOW + Q_BLOCK
    nq = t // Q_BLOCK
    n_cr = t // CMP_STRIDE
    n_c = (t - CMP_BLOCK) // CMP_STRIDE + 1
    n_s = t // SEL_BLOCK
    assert n_s >= N_SEL
    ov = _overlap_shifted(n_cr, n_c, n_s, n_s)
    cb = NQ_COLS // HEAD_DIM
    seq_col = lambda off: (lambda b, k, i: (b, off + k))
    return pl.pallas_call(
        functools.partial(_nsa_prompt_kernel, t=t),
        grid=(bsz, N_KV_HEADS, nq),
        in_specs=[pl.BlockSpec((Q_BLOCK, QW), lambda b, k, i: (b * nq + i, k)),
                  pl.BlockSpec((None, n_cr, HEAD_DIM), lambda b, k, i: (b, 0, k)),
                  pl.BlockSpec((None, n_cr, HEAD_DIM), lambda b, k, i: (b, 0, N_KV_HEADS + k)),
                  pl.BlockSpec((t, HEAD_DIM), seq_col(cb + 2 * N_KV_HEADS)),
                  pl.BlockSpec((t, HEAD_DIM), seq_col(cb + 3 * N_KV_HEADS)),
                  pl.BlockSpec((t, HEAD_DIM), seq_col(cb + 4 * N_KV_HEADS)),
                  pl.BlockSpec((t, HEAD_DIM), seq_col(cb + 5 * N_KV_HEADS)),
                  pl.BlockSpec((None, Q_BLOCK, 3 * GROUP), lambda b, k, i: (k, b * nq + i, 0)),
                  pl.BlockSpec((n_cr, n_s), lambda b, k, i: (0, 0))],
        out_specs=pl.BlockSpec((Q_BLOCK, QW), lambda b, k, i: (b * nq + i, k)),
        out_shape=jax.ShapeDtypeStruct((bsz * t, NQ_COLS), BF16),
        compiler_params=_params("parallel", "parallel", "arbitrary"),
    )(y_main, cmp_p, cmp_p, y_main, y_main, y_main, y_main, gates_t, ov)


SAMPLE_ROWS = 8


def _nsa_sample_kernel(pt_ref, y_ref, cmp_ref, page_ref, win_ref, g_ref, ov_ref, o_ref,
                       q_scr, sel_scr, m_scr, l_scr, acc_scr, oc_scr, *, past, s_len, n_s):
    del pt_ref
    pg = pl.program_id(1)
    n_pages = pl.num_programs(1)
    sr = SAMPLE_ROWS
    rk = GROUP * sr
    scale = HEAD_DIM ** -0.5
    tpos = lax.broadcasted_iota(jnp.int32, (rk, 1), 0) % sr
    qpos = past + tpos
    n_sc = sel_scr.shape[1]
    nbp = past // SEL_BLOCK
    kv0 = NQ_COLS

    @pl.when(pg == 0)
    def _():
        y = y_ref[...]
        n_cr = cmp_ref.shape[0]
        m_idx = lax.broadcasted_iota(jnp.int32, (1, n_cr), 1)
        cmask = (m_idx >= 1) & (m_idx * CMP_STRIDE + (CMP_BLOCK - CMP_STRIDE - 1) <= qpos)
        for k in range(N_KV_HEADS):
            qk = jnp.concatenate([y[:, (k * GROUP + g) * HEAD_DIM:(k * GROUP + g + 1) * HEAD_DIM]
                                  for g in range(GROUP)], axis=0) * scale
            qk = qk.astype(BF16)
            q_scr[k] = qk
            kc = cmp_ref[:, k * HEAD_DIM:(k + 1) * HEAD_DIM].astype(BF16)
            vc = cmp_ref[:, (N_KV_HEADS + k) * HEAD_DIM:(N_KV_HEADS + k + 1) * HEAD_DIM].astype(BF16)
            p_c = _masked_softmax(_dot_nt(qk, kc), cmask)
            oc_scr[k * rk:(k + 1) * rk, :] = _dot(p_c.astype(BF16), vc)
            psum = p_c[0:sr]
            for g in range(1, GROUP):
                psum = psum + p_c[g * sr:(g + 1) * sr]
            imp = _dot(psum, ov_ref[...], HI)
            sel = _select_blocks(imp, qpos[0:sr], n_s, n_sc)
            for g in range(GROUP):
                sel_scr[k * rk + g * sr:k * rk + (g + 1) * sr, :] = sel
        m_scr[...] = jnp.full(m_scr.shape, NEG_INF, F32)
        l_scr[...] = jnp.zeros(l_scr.shape, F32)
        acc_scr[...] = jnp.zeros(acc_scr.shape, F32)

    def online_update(k, sc, mask, v):
        rows = slice(k * rk, (k + 1) * rk)
        sc = jnp.where(mask, sc, NEG_INF)
        m_old = m_scr[rows, :]
        m_new = jnp.maximum(m_old, jnp.max(sc, axis=-1, keepdims=True))
        alpha = jnp.exp(m_old - m_new)
        p = jnp.where(mask, jnp.exp(sc - m_new), 0.0)
        l_scr[rows, :] = alpha * l_scr[rows, :] + jnp.sum(p, axis=-1, keepdims=True)
        acc_scr[rows, :] = alpha * acc_scr[rows, :] + _dot(p.astype(v.dtype), v)
        m_scr[rows, :] = m_new

    kpos = pg * PAGE_SIZE + lax.broadcasted_iota(jnp.int32, (1, PAGE_SIZE), 1)
    blk_row = lax.broadcasted_iota(jnp.int32, (n_sc, 1), 0)
    expand = jnp.where(blk_row == kpos // SEL_BLOCK, 1.0, 0.0).astype(BF16)
    selx = _dot(sel_scr[...].astype(BF16), expand)
    for k in range(N_KV_HEADS):
        kp = page_ref[:, k * HEAD_DIM:(k + 1) * HEAD_DIM].astype(BF16)
        vp = page_ref[:, (N_KV_HEADS + k) * HEAD_DIM:(N_KV_HEADS + k + 1) * HEAD_DIM].astype(BF16)
        mask = (selx[k * rk:(k + 1) * rk] > 0.5) & (kpos <= qpos)
        online_update(k, _dot_nt(q_scr[k], kp), mask, vp)

    @pl.when(pg == n_pages - 1)
    def _():
        y = y_ref[...]
        rpos = lax.broadcasted_iota(jnp.int32, (1, sr), 1)
        new_ok = (rpos < s_len) & (rpos <= tpos)
        gt = g_ref[...]
        for k in range(N_KV_HEADS):
            rows = slice(k * rk, (k + 1) * rk)
            qk = q_scr[k]
            qf = qk.astype(F32)
            c_ks = kv0 + (2 * N_KV_HEADS + k) * HEAD_DIM
            c_vs = kv0 + (3 * N_KV_HEADS + k) * HEAD_DIM
            kn = y[:, c_ks:c_ks + HEAD_DIM]
            vn = y[:, c_vs:c_vs + HEAD_DIM]
            new_sel = sel_scr[rows, nbp:nbp + 1] > 0.5
            online_update(k, _dot_nt(qf, kn), new_sel & new_ok, vn)
            o_s = acc_scr[rows, :] / jnp.maximum(l_scr[rows, :], TINY)
            wb = win_ref.shape[0]
            c_kw = kv0 + (4 * N_KV_HEADS + k) * HEAD_DIM
            c_vw = kv0 + (5 * N_KV_HEADS + k) * HEAD_DIM
            kwb = win_ref[:, k * HEAD_DIM:(k + 1) * HEAD_DIM].astype(BF16)
            vwb = win_ref[:, (N_KV_HEADS + k) * HEAD_DIM:(N_KV_HEADS + k + 1) * HEAD_DIM].astype(BF16)
            kwn = y[:, c_kw:c_kw + HEAD_DIM]
            vwn = y[:, c_vw:c_vw + HEAD_DIM]
            dpos = qpos - (past - wb + lax.broadcasted_iota(jnp.int32, (1, wb), 1))
            mask_b = (dpos >= 0) & (dpos < WINDOW)
            s_b = jnp.where(mask_b, _dot_nt(qk, kwb), NEG_INF)
            s_n = jnp.where(new_ok, _dot_nt(qf, kwn), NEG_INF)
            mw = jnp.maximum(jnp.max(s_b, axis=-1, keepdims=True), jnp.max(s_n, axis=-1, keepdims=True))
            p_b = jnp.where(mask_b, jnp.exp(s_b - mw), 0.0)
            p_n = jnp.where(new_ok, jnp.exp(s_n - mw), 0.0)
            den = jnp.sum(p_b, axis=-1, keepdims=True) + jnp.sum(p_n, axis=-1, keepdims=True)
            o_w = (_dot(p_b.astype(BF16), vwb) + _dot(p_n, vwn)) / jnp.maximum(den, TINY)
            o_c = oc_scr[rows, :]
            for g in range(GROUP):
                r0, r1 = g * sr, (g + 1) * sr
                c = (k * GROUP + g) * 3
                o = gt[:, c:c + 1] * o_c[r0:r1] + gt[:, c + 1:c + 2] * o_s[r0:r1] + gt[:, c + 2:c + 3] * o_w[r0:r1]
                o_ref[:, (k * GROUP + g) * HEAD_DIM:(k * GROUP + g + 1) * HEAD_DIM] = o


def _nsa_sample(ys8, cmp_s, cache2, layer, win2, gates8, page_table, s_len):
    db, n_pages = page_table.shape
    past = n_pages * PAGE_SIZE
    assert s_len <= SAMPLE_ROWS and s_len <= SEL_BLOCK and past % SEL_BLOCK == 0
    n_cr = cmp_s.shape[1]
    n_c = (past + s_len - CMP_BLOCK) // CMP_STRIDE + 1
    assert n_c + 1 <= n_cr
    n_s = -(-(past + s_len) // SEL_BLOCK)
    n_sc = -(-n_s // 128) * 128
    ov = _overlap_shifted(n_cr, n_c, n_s, n_sc)
    wb = win2.shape[1]
    ncol = ys8.shape[2]
    rows = N_KV_HEADS * GROUP * SAMPLE_ROWS
    grid_spec = pltpu.PrefetchScalarGridSpec(
        num_scalar_prefetch=1,
        grid=(db, n_pages),
        in_specs=[pl.BlockSpec((None, SAMPLE_ROWS, ncol), lambda b, p, pt: (b, 0, 0)),
                  pl.BlockSpec((None, n_cr, 2 * NKV_COLS), lambda b, p, pt: (b, 0, 0)),
                  pl.BlockSpec((None, None, PAGE_SIZE, 2 * NKV_COLS), lambda b, p, pt: (layer, pt[b, p], 0, 1)),
                  pl.BlockSpec((None, wb, 2 * NKV_COLS), lambda b, p, pt: (b, 0, 0)),
                  pl.BlockSpec((None, SAMPLE_ROWS, 3 * N_HEADS), lambda b, p, pt: (b, 0, 0)),
                  pl.BlockSpec((n_cr, n_sc), lambda b, p, pt: (0, 0))],
        out_specs=pl.BlockSpec((None, SAMPLE_ROWS, NQ_COLS), lambda b, p, pt: (b, 0, 0)),
        scratch_shapes=[pltpu.VMEM((N_KV_HEADS, GROUP * SAMPLE_ROWS, HEAD_DIM), BF16),
                        pltpu.VMEM((rows, n_sc), F32),
                        pltpu.VMEM((rows, 1), F32),
                        pltpu.VMEM((rows, 1), F32),
                        pltpu.VMEM((rows, HEAD_DIM), F32),
                        pltpu.VMEM((rows, HEAD_DIM), F32)],
    )
    return pl.pallas_call(
        functools.partial(_nsa_sample_kernel, past=past, s_len=s_len, n_s=n_s),
        grid_spec=grid_spec,
        out_shape=jax.ShapeDtypeStruct((db, SAMPLE_ROWS, NQ_COLS), F32),
        compiler_params=_params("parallel", "arbitrary"),
    )(page_table, ys8, cmp_s, cache2, win2, gates8, ov)


WKV_PREC = HI
WKV_HEADS_PER_STEP = 4
WKV_DIAG_BLOCK = 16


def _unit_lower_inverse(a, c):
    blk = min(WKV_DIAG_BLOCK, c)
    ri = lax.broadcasted_iota(jnp.int32, (c, c), 0)
    ci = lax.broadcasted_iota(jnp.int32, (c, c), 1)
    eye = jnp.where(ri == ci, 1.0, 0.0)
    same = (ri // blk) == (ci // blk)
    d = jnp.where(same, a, 0.0)
    td = eye + d
    dp = d
    for _ in range(int(math.log2(blk)) - 1):
        dp = _dot(dp, dp, WKV_PREC)
        td = td + _dot(dp, td, WKV_PREC)
    if blk == c:
        return td
    lo = jnp.where(same, 0.0, a)
    n = _dot(td, lo, WKV_PREC)
    tinv = td + _dot(n, td, WKV_PREC)
    npow = n
    for _ in range(int(math.log2(c // blk)) - 1):
        npow = _dot(npow, npow, WKV_PREC)
        tinv = tinv + _dot(npow, tinv, WKV_PREC)
    return tinv


def _wkv_kernel(r_ref, k_ref, v_ref, lw_ref, a_ref, g_ref, kk_ref, ka_ref, rk_ref, lg_ref, lb_ref, s0_ref,
                z_ref, s_ref, *, c, hb):
    n = RWKV_HEAD

    @pl.when(pl.program_id(2) == 0)
    def _():
        s_ref[...] = s0_ref[...]

    ri = lax.broadcasted_iota(jnp.int32, (c, c), 0)
    ci = lax.broadcasted_iota(jnp.int32, (c, c), 1)
    strict = ri > ci
    incl = ri >= ci
    tri = jnp.where(incl, 1.0, 0.0)
    zs = []
    for hh in range(hb):
        sl = slice(hh * n, (hh + 1) * n)
        r = r_ref[:, sl]
        k = k_ref[:, sl]
        v = v_ref[:, sl]
        lw = lw_ref[:, sl]
        a = a_ref[:, sl]
        kk = k * kk_ref[:, sl]
        kk = kk / jnp.maximum(jnp.sqrt(jnp.sum(kk * kk, axis=-1, keepdims=True)), 1e-12)
        k2 = k * (1.0 + (a - 1.0) * ka_ref[:, sl])
        b = kk * a
        cum = _dot(tri, lw, HI)
        e_in = jnp.exp(cum)
        e_neg = jnp.exp(-cum)
        r_t = r * e_in
        a_t = -kk * jnp.exp(cum - lw)
        b_t = b * e_neg
        k_t = k2 * e_neg
        lhs = jnp.concatenate([a_t, r_t], axis=0)
        rhs = jnp.concatenate([b_t, k_t], axis=0)
        aa = _dot_nt(lhs, rhs, WKV_PREC)
        a_ab = jnp.where(strict, aa[:c, :c], 0.0)
        a_ak = jnp.where(strict, aa[:c, c:], 0.0)
        a_rb = jnp.where(incl, aa[c:, :c], 0.0)
        a_rk = jnp.where(incl, aa[c:, c:], 0.0)
        tinv = _unit_lower_inverse(a_ab, c)
        s = s_ref[hh]
        ls = _dot_nt(lhs, s, WKV_PREC)
        p = _dot(tinv, ls[:c] + _dot(a_ak, v, WKV_PREC), WKV_PREC)
        pv = jnp.concatenate([p, v], axis=0)
        y = ls[c:] + _dot(jnp.concatenate([a_rb, a_rk], axis=1), pv, WKV_PREC)
        w_c = e_in[c - 1:c, :]
        bk = jnp.concatenate([b_t * w_c, k_t * w_c], axis=0)
        s_ref[hh] = s * w_c + _dot_tn(pv, bk, WKV_PREC)

        mu = jnp.mean(y, axis=-1, keepdims=True)
        yc = y - mu
        yn = yc * lax.rsqrt(jnp.mean(yc * yc, axis=-1, keepdims=True) + LNX_EPS)
        yn = yn * lg_ref[:, sl] + lb_ref[:, sl]
        bonus = jnp.sum(r * k2 * rk_ref[:, sl], axis=-1, keepdims=True) * v
        zs.append((yn + bonus) * g_ref[:, sl])
    z_ref[...] = jnp.concatenate(zs, axis=1).astype(z_ref.dtype)


def _wkv(rkv, lw, a, g, k_k, k_a, r_k, lnx_g, lnx_b, s0, n_seq, t, c):
    rows, d = lw.shape
    h = d // RWKV_HEAD
    hb = min(WKV_HEADS_PER_STEP, h)
    hw = hb * RWKV_HEAD
    nch = t // c
    tok = lambda s, hg, ch: (s * nch + ch, hg)
    par = lambda s, hg, ch: (0, hg)
    st = lambda s, hg, ch: (s, hg, 0, 0)
    rkv_spec = lambda p: pl.BlockSpec((None, c, hw), lambda s, hg, ch: (p, s * nch + ch, hg))
    return pl.pallas_call(
        functools.partial(_wkv_kernel, c=c, hb=hb),
        grid=(n_seq, h // hb, nch),
        in_specs=[rkv_spec(0), rkv_spec(1), rkv_spec(2),
                  pl.BlockSpec((c, hw), tok), pl.BlockSpec((c, hw), tok), pl.BlockSpec((c, hw), tok),
                  pl.BlockSpec((1, hw), par), pl.BlockSpec((1, hw), par), pl.BlockSpec((1, hw), par),
                  pl.BlockSpec((1, hw), par), pl.BlockSpec((1, hw), par),
                  pl.BlockSpec((None, hb, RWKV_HEAD, RWKV_HEAD), st)],
        out_specs=[pl.BlockSpec((c, hw), tok),
                   pl.BlockSpec((None, hb, RWKV_HEAD, RWKV_HEAD), st)],
        out_shape=[jax.ShapeDtypeStruct((rows, d), BF16),
                   jax.ShapeDtypeStruct((n_seq, h, RWKV_HEAD, RWKV_HEAD), F32)],
        compiler_params=_params("parallel", "parallel", "arbitrary"),
    )(rkv, rkv, rkv, lw, a, g, k_k, k_a, r_k, lnx_g, lnx_b, s0)


WKV_CHUNK = 64


def _nsa_layer(x, np_tok, bsz, t, db, s_len, cache2, layer, win, page_table, w_in, pe, w1, w2, w_out,
               ln_g, ln_b, alpha):
    d = x.shape[1]
    n_main = NQ_COLS + 6 * NKV_COLS
    y_main = _mm(x, w_in, n_main, tn_pref=1024)
    w_gate = jnp.pad(w_in[:, n_main:], ((0, 0), (0, 128 - 3 * N_HEADS)))
    gates = _mm(x, w_gate, 128, act="sigmoid")[:, :3 * N_HEADS]

    wc, w2b = _cmp_weights(w1, w2)
    bias = _cmp_bias(pe, w1)
    cmp_p = _cmp_prompt(y_main, bsz, t, wc, bias, w2b)
    cmp_s = _cmp_sample(cache2, layer, page_table, wc, bias, w2b)

    gates_p = gates[:np_tok].reshape(np_tok, N_KV_HEADS, 3 * GROUP).transpose(1, 0, 2)
    o_p = _nsa_prompt(y_main, cmp_p, gates_p, bsz, t)

    pad_rows = ((0, 0), (0, SAMPLE_ROWS - s_len), (0, 0))
    ys = y_main[np_tok:].reshape(db, s_len, n_main)
    ys8 = jnp.pad(ys, pad_rows)
    gates8 = jnp.pad(gates[np_tok:].reshape(db, s_len, 3 * N_HEADS), pad_rows)
    wb = win.shape[1]
    win2 = win.reshape(db, wb, 2 * NKV_COLS)
    o_s = _nsa_sample(ys8, cmp_s, cache2, layer, win2, gates8, page_table, s_len)
    o_s = o_s[:, :s_len].reshape(db * s_len, NQ_COLS).astype(BF16)

    o = jnp.concatenate([o_p, o_s], axis=0)
    h = _mm_ln(o, w_out, x, ln_g, ln_b, alpha)

    yp = y_main[:np_tok].reshape(bsz, t, n_main)
    kv_p = yp[:, :, NQ_COLS:NQ_COLS + 4 * NKV_COLS].reshape(bsz, t, 4, N_KV_HEADS, HEAD_DIM)
    wlen = min(WINDOW, t)
    win_p = yp[:, t - wlen:, NQ_COLS + 4 * NKV_COLS:].reshape(bsz, wlen, 2, N_KV_HEADS, HEAD_DIM)
    kv_s = ys[:, :, NQ_COLS:NQ_COLS + 4 * NKV_COLS].reshape(db, s_len, 4, N_KV_HEADS, HEAD_DIM)
    kvw_s = ys[:, :, NQ_COLS + 4 * NKV_COLS:].reshape(db, s_len, 2, N_KV_HEADS, HEAD_DIM)
    win_s = jnp.concatenate([win, kvw_s], axis=1)[:, s_len:]
    return h, kv_p, kv_s, win_p, win_s


def _rwkv_layer(x, np_tok, bsz, t, db, s_len, shift_s, wkv_s, mu, w_rkv, w0, w1, w2, a0, a1, a2, g1, g2,
                k_k, k_a, r_k, lnx_g, lnx_b, w_out, ln_g, ln_b, alpha):
    d = x.shape[1]
    heads = d // RWKV_HEAD
    xp3 = x[:np_tok].reshape(bsz, t, d)
    xs3 = x[np_tok:].reshape(db, s_len, d)
    prev_p = jnp.concatenate([jnp.zeros((bsz, 1, d), x.dtype), xp3[:, :-1]], axis=1)
    prev_s = jnp.concatenate([shift_s[:, None].astype(x.dtype), xs3[:, :-1]], axis=1)
    xprev = jnp.concatenate([prev_p.reshape(np_tok, d), prev_s.reshape(db * s_len, d)], axis=0)

    row = lambda v: v.reshape(1, -1)
    rkv = _mm_mix(x, xprev, mu[:3, None, :], w_rkv)
    lw = _lora(x, xprev, row(mu[3]), w1, w2, row(w0), "decay")
    a = _lora(x, xprev, row(mu[4]), a1, a2, row(a0), "aaa")
    g = _lora(x, xprev, row(mu[5]), g1, g2, jnp.zeros((1, d), F32), "gate")

    pvec = (row(k_k), row(k_a), row(r_k), row(lnx_g), row(lnx_b))
    c_p = _tile(t, WKV_CHUNK, 8)
    zero_state = jnp.zeros((bsz, heads, RWKV_HEAD, RWKV_HEAD), F32)
    z_p, st_p = _wkv(rkv[:, :np_tok], lw[:np_tok], a[:np_tok], g[:np_tok], *pvec, zero_state, bsz, t, c_p)

    c_s = -(-s_len // 8) * 8
    pad = lambda v: jnp.pad(v.reshape(v.shape[:-2] + (db, s_len, d)),
                            [(0, 0)] * (v.ndim - 1) + [(0, c_s - s_len), (0, 0)]
                            ).reshape(v.shape[:-2] + (db * c_s, d))
    z_s, st_s = _wkv(pad(rkv[:, np_tok:]), pad(lw[np_tok:]), pad(a[np_tok:]), pad(g[np_tok:]), *pvec,
                     wkv_s.astype(F32), db, c_s, c_s)
    z_s = z_s.reshape(db, c_s, d)[:, :s_len].reshape(db * s_len, d)

    z = jnp.concatenate([z_p, z_s], axis=0)
    h = _mm_ln(z, w_out, x, ln_g, ln_b, alpha)
    return h, st_p, st_s, xp3[:, -1], xs3[:, -1]


def _mlp(h, w_up, w_down, ln_g, ln_b, alpha):
    d_ff = w_up.shape[1]
    u = _mm(h, w_up, d_ff, act="relu2", out_dtype=BF16)
    return _mm_ln(u, w_down, h, ln_g, ln_b, alpha)


def kernel(x_prompt, x_sample, cache_nsa_kv, state_nsa_win, state_rwkv_wkv, state_rwkv_shift, page_table,
           nsa_w_in, nsa_cmp_pe, nsa_cmp_w1, nsa_cmp_w2, nsa_w_out,
           rwkv_mu, rwkv_w_rkv, rwkv_w0, rwkv_w1, rwkv_w2, rwkv_a0, rwkv_a1, rwkv_a2, rwkv_g1, rwkv_g2,
           rwkv_k_k, rwkv_k_a, rwkv_r_k, rwkv_lnx_g, rwkv_lnx_b, rwkv_w_out,
           ffn_w_up, ffn_w_down, ln_g, ln_b):
    bsz, t, d = x_prompt.shape
    db, s_len, _ = x_sample.shape
    depth = ffn_w_up.shape[0]
    alpha = (2 * depth) ** 0.25
    np_tok = bsz * t
    x = jnp.concatenate([x_prompt.reshape(np_tok, d), x_sample.reshape(db * s_len, d)], axis=0)
    n_l, n_pool = cache_nsa_kv.shape[:2]
    cache2 = cache_nsa_kv.reshape(n_l, n_pool, PAGE_SIZE, 4 * NKV_COLS)
    page_table = page_table.astype(jnp.int32)

    kv_p, kv_s, win_p, win_s, wkv_p, wkv_s, sh_p, sh_s = ([] for _ in range(8))
    row = lambda v: v.reshape(1, -1)
    for i in range(depth):
        j = i // 2
        if i % 2 == 0:
            h, kvp_new, kvs_new, wp_new, ws_new = _nsa_layer(
                x, np_tok, bsz, t, db, s_len, cache2, j, state_nsa_win[j], page_table,
                nsa_w_in[j], nsa_cmp_pe[j], nsa_cmp_w1[j], nsa_cmp_w2[j], nsa_w_out[j],
                row(ln_g[i, 0]), row(ln_b[i, 0]), alpha)
            kv_p.append(kvp_new)
            kv_s.append(kvs_new)
            win_p.append(wp_new)
            win_s.append(ws_new)
        else:
            h, sp_new, ss_new, hp_new, hs_new = _rwkv_layer(
                x, np_tok, bsz, t, db, s_len, state_rwkv_shift[j], state_rwkv_wkv[j],
                rwkv_mu[j], rwkv_w_rkv[j], rwkv_w0[j], rwkv_w1[j], rwkv_w2[j], rwkv_a0[j], rwkv_a1[j],
                rwkv_a2[j], rwkv_g1[j], rwkv_g2[j], rwkv_k_k[j], rwkv_k_a[j], rwkv_r_k[j],
                rwkv_lnx_g[j], rwkv_lnx_b[j], rwkv_w_out[j], row(ln_g[i, 0]), row(ln_b[i, 0]), alpha)
            wkv_p.append(sp_new)
            wkv_s.append(ss_new)
            sh_p.append(hp_new)
            sh_s.append(hs_new)
        x = _mlp(h, ffn_w_up[i], ffn_w_down[i], row(ln_g[i, 1]), row(ln_b[i, 1]), alpha)
    return (x[:np_tok].reshape(bsz, t, d), x[np_tok:].reshape(db, s_len, d),
            jnp.stack(kv_p), jnp.stack(kv_s), jnp.stack(win_p), jnp.stack(win_s),
            jnp.stack(wkv_p), jnp.stack(wkv_s), jnp.stack(sh_p), jnp.stack(sh_s))
```

```python
import functools
import math

import jax
import jax.numpy as jnp
import numpy as np
from jax import lax
from jax.experimental import pallas as pl
from jax.experimental.pallas import tpu as pltpu

F32 = jnp.float32
BF16 = jnp.bfloat16

HEAD_DIM = 128
N_KV_HEADS = 4
GROUP = 4
N_HEADS = N_KV_HEADS * GROUP
QW = GROUP * HEAD_DIM
NQ_COLS = N_HEADS * HEAD_DIM
NKV_COLS = N_KV_HEADS * HEAD_DIM
CMP_BLOCK = 32
CMP_STRIDE = 16
SEL_BLOCK = 64
N_SEL = 16
WINDOW = 512
Q_BLOCK = 128
PAGE_SIZE = 128
RWKV_HEAD = 64
LN_EPS = 1e-5
LNX_EPS = 64e-5
FORCE = 1e4
NEG_INF = -1e30
TINY = 1e-30
CHUNKS_PER_PAGE = PAGE_SIZE // CMP_STRIDE

VMEM_LIMIT_BYTES = 56 * 1024 * 1024
HI = lax.Precision.HIGHEST


def _params(*sem):
    return pltpu.CompilerParams(dimension_semantics=sem, vmem_limit_bytes=VMEM_LIMIT_BYTES)


def _tile(n, pref, mult=16):
    if n <= pref:
        return n
    for t in range(pref - pref % mult, 0, -mult):
        if n % t == 0:
            return t
    return n


def _dot(a, b, precision=None):
    return jnp.dot(a, b, preferred_element_type=F32, precision=precision)


def _dot_nt(a, b, precision=None):
    return lax.dot_general(a, b, (((1,), (1,)), ((), ())), preferred_element_type=F32, precision=precision)


def _dot_tn(a, b, precision=None):
    return lax.dot_general(a, b, (((0,), (0,)), ((), ())), preferred_element_type=F32, precision=precision)


def _masked_softmax(s, mask):
    s = jnp.where(mask, s, NEG_INF)
    m = jnp.max(s, axis=-1, keepdims=True)
    p = jnp.where(mask, jnp.exp(s - m), 0.0)
    return p * (1.0 / jnp.maximum(jnp.sum(p, axis=-1, keepdims=True), TINY))


def _layer_norm_rows(s, g, b, eps):
    mu = jnp.mean(s, axis=-1, keepdims=True)
    c = s - mu
    var = jnp.mean(c * c, axis=-1, keepdims=True)
    return c * lax.rsqrt(var + eps) * g + b


def _mm_kernel(x_ref, w_ref, o_ref, xb_ref, *, act):
    @pl.when(pl.program_id(1) == 0)
    def _():
        xb_ref[...] = x_ref[...].astype(BF16)

    y = _dot(xb_ref[...], w_ref[...].astype(BF16))
    if act == "relu2":
        y = jnp.maximum(y, 0.0)
        y = y * y
    elif act == "sigmoid":
        y = jax.nn.sigmoid(y)
    o_ref[...] = y.astype(o_ref.dtype)


def _mm(x, w, n_out, *, act=None, out_dtype=F32, tm_pref=1040, tn_pref=1024):
    m, k = x.shape
    tm = _tile(m, tm_pref)
    tn = _tile(n_out, tn_pref, 128)
    return pl.pallas_call(
        functools.partial(_mm_kernel, act=act),
        grid=(m // tm, n_out // tn),
        in_specs=[pl.BlockSpec((tm, k), lambda i, j: (i, 0)),
                  pl.BlockSpec((k, tn), lambda i, j: (0, j))],
        out_specs=pl.BlockSpec((tm, tn), lambda i, j: (i, j)),
        out_shape=jax.ShapeDtypeStruct((m, n_out), out_dtype),
        scratch_shapes=[pltpu.VMEM((tm, k), BF16)],
        compiler_params=_params("parallel", "arbitrary"),
    )(x, w)


def _mm_mix_kernel(x_ref, xp_ref, mu_ref, w_ref, o_ref, xb_ref):
    @pl.when(pl.program_id(2) == 0)
    def _():
        x = x_ref[...]
        xb_ref[...] = (x + (xp_ref[...] - x) * mu_ref[...]).astype(BF16)

    o_ref[...] = _dot(xb_ref[...], w_ref[...].astype(BF16))


def _mm_mix(x, xp, mu, w, *, tm_pref=640, tn_pref=1024):
    m, k = x.shape
    npar, _, n = w.shape
    tm = _tile(m, tm_pref)
    tn = _tile(n, tn_pref, 128)
    return pl.pallas_call(
        _mm_mix_kernel,
        grid=(npar, m // tm, n // tn),
        in_specs=[pl.BlockSpec((tm, k), lambda p, i, j: (i, 0)),
                  pl.BlockSpec((tm, k), lambda p, i, j: (i, 0)),
                  pl.BlockSpec((None, 1, k), lambda p, i, j: (p, 0, 0)),
                  pl.BlockSpec((None, k, tn), lambda p, i, j: (p, 0, j))],
        out_specs=pl.BlockSpec((None, tm, tn), lambda p, i, j: (p, i, j)),
        out_shape=jax.ShapeDtypeStruct((npar, m, n), F32),
        scratch_shapes=[pltpu.VMEM((tm, k), BF16)],
        compiler_params=_params("parallel", "parallel", "arbitrary"),
    )(x, xp, mu, w)


def _lora_kernel(x_ref, xp_ref, mu_ref, a_ref, b_ref, bias_ref, o_ref, *, kind):
    x = x_ref[...]
    xm = (x + (xp_ref[...] - x) * mu_ref[...]).astype(BF16)
    h = _dot(xm, a_ref[...].astype(BF16))
    if kind == "decay":
        h = jnp.tanh(h)
    elif kind == "gate":
        h = jax.nn.sigmoid(h)
    z = _dot(h.astype(BF16), b_ref[...].astype(BF16))
    if kind == "decay":
        u = -(z + bias_ref[...])
        softplus = jnp.maximum(u, 0.0) + jnp.log(1.0 + jnp.exp(-jnp.abs(u)))
        z = -jnp.exp(-softplus - 0.5)
    elif kind == "aaa":
        z = jax.nn.sigmoid(z + bias_ref[...])
    o_ref[...] = z


def _lora(x, xp, mu, a, b, bias, kind, *, tm_pref=320):
    m, k = x.shape
    r = a.shape[1]
    n = b.shape[1]
    tm = _tile(m, tm_pref)
    row = lambda i: (i, 0)
    fixed = lambda i: (0, 0)
    return pl.pallas_call(
        functools.partial(_lora_kernel, kind=kind),
        grid=(m // tm,),
        in_specs=[pl.BlockSpec((tm, k), row), pl.BlockSpec((tm, k), row), pl.BlockSpec((1, k), fixed),
                  pl.BlockSpec((k, r), fixed), pl.BlockSpec((r, n), fixed), pl.BlockSpec((1, n), fixed)],
        out_specs=pl.BlockSpec((tm, n), row),
        out_shape=jax.ShapeDtypeStruct((m, n), F32),
        compiler_params=_params("parallel"),
    )(x, xp, mu, a, b, bias)


def _mm_ln_kernel(z_ref, w_ref, res_ref, g_ref, b_ref, o_ref, *, alpha):
    kk = pl.program_id(1)
    part = _dot(z_ref[...].astype(BF16), w_ref[...].astype(BF16))

    @pl.when(kk == 0)
    def _():
        o_ref[...] = part

    @pl.when(kk > 0)
    def _():
        o_ref[...] += part

    @pl.when(kk == pl.num_programs(1) - 1)
    def _():
        s = alpha * res_ref[...] + o_ref[...]
        o_ref[...] = _layer_norm_rows(s, g_ref[...], b_ref[...], LN_EPS)


def _mm_ln(z, w, res, g, b, alpha, *, tm_pref=640, tk_pref=512):
    m, k = z.shape
    n = w.shape[1]
    tm = _tile(m, tm_pref)
    tk = _tile(k, tk_pref, 128)
    return pl.pallas_call(
        functools.partial(_mm_ln_kernel, alpha=alpha),
        grid=(m // tm, k // tk),
        in_specs=[pl.BlockSpec((tm, tk), lambda i, kk: (i, kk)),
                  pl.BlockSpec((tk, n), lambda i, kk: (kk, 0)),
                  pl.BlockSpec((tm, n), lambda i, kk: (i, 0)),
                  pl.BlockSpec((1, n), lambda i, kk: (0, 0)),
                  pl.BlockSpec((1, n), lambda i, kk: (0, 0))],
        out_specs=pl.BlockSpec((tm, n), lambda i, kk: (i, 0)),
        out_shape=jax.ShapeDtypeStruct((m, n), F32),
        compiler_params=_params("parallel", "arbitrary"),
    )(z, w, res, g, b)


def _cmp_bias_kernel(pe_ref, w1_ref, o_ref):
    acc = jnp.zeros((1, HEAD_DIM), F32)
    for rj in range(CMP_BLOCK):
        acc = acc + _dot(pe_ref[rj:rj + 1, :], w1_ref[rj], HI)
    o_ref[...] = acc


def _cmp_bias(pe, w1):
    return pl.pallas_call(
        _cmp_bias_kernel,
        grid=(2,),
        in_specs=[pl.BlockSpec((None, CMP_BLOCK, HEAD_DIM), lambda p: (p, 0, 0)),
                  pl.BlockSpec((None, CMP_BLOCK, HEAD_DIM, HEAD_DIM), lambda p: (p, 0, 0, 0))],
        out_specs=pl.BlockSpec((None, 1, HEAD_DIM), lambda p: (p, 0, 0)),
        out_shape=jax.ShapeDtypeStruct((2, 1, HEAD_DIM), F32),
        compiler_params=_params("parallel"),
    )(pe, w1)


def _cmp_kernel(*refs, n_in):
    pages = refs[-5 - n_in:-5]
    wc_ref, bias_ref, w2_ref, out_ref, carry_ref = refs[-5:]
    mk = CHUNKS_PER_PAGE * n_in
    rows = N_KV_HEADS * mk

    @pl.when(pl.program_id(1) == 0)
    def _():
        carry_ref[...] = jnp.zeros_like(carry_ref)

    first = (lax.broadcasted_iota(jnp.int32, (rows, 1), 0) % mk) == 0
    r_out = lax.broadcasted_iota(jnp.int32, (PAGE_SIZE, PAGE_SIZE), 0)
    r_in = lax.broadcasted_iota(jnp.int32, (PAGE_SIZE, PAGE_SIZE), 1)
    perm = jnp.where(r_in == (r_out % CHUNKS_PER_PAGE) * CMP_STRIDE + r_out // CHUNKS_PER_PAGE, 1.0, 0.0)
    perm = perm.astype(BF16)

    def page_rows(ref):
        if len(ref.shape) == 2:
            return ref[...]
        by_slot = jnp.swapaxes(ref[...], 0, 1)
        return jnp.concatenate([by_slot[s] for s in range(2 * N_KV_HEADS)], axis=1)

    grouped = [_dot(perm, page_rows(pages[i]).astype(BF16)) for i in range(n_in)]
    for p in range(2):
        acc = jnp.zeros((rows, 2 * HEAD_DIM), F32)
        for j in range(CMP_STRIDE):
            lhs = jnp.concatenate(
                [grouped[i][j * CHUNKS_PER_PAGE:(j + 1) * CHUNKS_PER_PAGE,
                            (p * N_KV_HEADS + k) * HEAD_DIM:(p * N_KV_HEADS + k + 1) * HEAD_DIM]
                 for k in range(N_KV_HEADS) for i in range(n_in)], axis=0)
            acc = acc + _dot(lhs.astype(BF16), wc_ref[p, j])
        part0 = acc[:, :HEAD_DIM]
        part1 = acc[:, HEAD_DIM:]
        prev = jnp.where(first, carry_ref[p], pltpu.roll(part0, 1, 0))
        carry_ref[p] = pltpu.roll(part0, rows - (mk - 1), 0)
        h = prev + part1 + bias_ref[p]
        c = _dot(jax.nn.gelu(h).astype(BF16), w2_ref[p])
        for k in range(N_KV_HEADS):
            col = (p * N_KV_HEADS + k) * HEAD_DIM
            out_ref[:, col:col + HEAD_DIM] = c[k * mk:(k + 1) * mk]


def _cmp_weights(w1, w2):
    n_r = CMP_BLOCK // CMP_STRIDE
    w1r = w1.reshape(2, n_r, CMP_STRIDE, HEAD_DIM, HEAD_DIM)
    wc = jnp.concatenate([w1r[:, r] for r in range(n_r)], axis=-1).astype(BF16)
    return wc, w2.astype(BF16)


def _cmp_call(n_seq, n_steps, n_in, page_specs, operands, wc, bias, w2b, num_prefetch, prefetch):
    mk = CHUNKS_PER_PAGE * n_in
    if num_prefetch:
        fixed3 = lambda s, t, pt: (0, 0, 0)
        fixed4 = lambda s, t, pt: (0, 0, 0, 0)
        out_map = lambda s, t, pt: (s, t, 0)
    else:
        fixed3 = lambda s, t: (0, 0, 0)
        fixed4 = lambda s, t: (0, 0, 0, 0)
        out_map = lambda s, t: (s, t, 0)
    grid_spec = pltpu.PrefetchScalarGridSpec(
        num_scalar_prefetch=num_prefetch,
        grid=(n_seq, n_steps),
        in_specs=page_specs + [
            pl.BlockSpec((2, CMP_STRIDE, HEAD_DIM, 2 * HEAD_DIM), fixed4),
            pl.BlockSpec((2, 1, HEAD_DIM), fixed3),
            pl.BlockSpec((2, HEAD_DIM, HEAD_DIM), fixed3)],
        out_specs=pl.BlockSpec((None, mk, 2 * NKV_COLS), out_map),
        scratch_shapes=[pltpu.VMEM((2, N_KV_HEADS * mk, HEAD_DIM), F32)],
    )
    return pl.pallas_call(
        functools.partial(_cmp_kernel, n_in=n_in),
        grid_spec=grid_spec,
        out_shape=jax.ShapeDtypeStruct((n_seq, n_steps * mk, 2 * NKV_COLS), F32),
        compiler_params=_params("parallel", "arbitrary"),
    )(*prefetch, *operands, wc, bias, w2b)


CMP_PAGES_PER_STEP = 8


def _cmp_prompt(y_main, bsz, t, wc, bias, w2b):
    pages_per_seq = t // PAGE_SIZE
    n_in = math.gcd(pages_per_seq, CMP_PAGES_PER_STEP)
    n_steps = pages_per_seq // n_in
    col_blk = NQ_COLS // (2 * NKV_COLS)
    specs = [pl.BlockSpec((PAGE_SIZE, 2 * NKV_COLS),
                          functools.partial(lambda s, st, i: (s * pages_per_seq + st * n_in + i, col_blk), i=i))
             for i in range(n_in)]
    return _cmp_call(bsz, n_steps, n_in, specs, [y_main] * n_in, wc, bias, w2b, 0, ())


def _cmp_sample(cache5, layer, page_table, wc, bias, w2b):
    n_in = math.gcd(page_table.shape[1], CMP_PAGES_PER_STEP)
    db, n_pages = page_table.shape
    n_steps = n_pages // n_in
    specs = [pl.BlockSpec((None, None, PAGE_SIZE, 2 * N_KV_HEADS, HEAD_DIM),
                          functools.partial(lambda s, st, pt, i: (layer, pt[s, st * n_in + i], 0, 0, 0), i=i))
             for i in range(n_in)]
    return _cmp_call(db, n_steps, n_in, specs, [cache5] * n_in, wc, bias, w2b, 1, (page_table,))


def _overlap_shifted(n_rows, n_c, n_s, n_cols):
    m = np.arange(n_rows)[:, None]
    j = np.arange(n_cols)[None, :]
    c0 = (m - 1) * CMP_STRIDE
    s0 = j * SEL_BLOCK
    ov = (c0 < s0 + SEL_BLOCK) & (c0 + CMP_BLOCK > s0) & (m >= 1) & (m <= n_c) & (j < n_s)
    return jnp.asarray(ov.astype(np.float32))


def _select_blocks(imp, qpos, n_s, n_cols):
    j = lax.broadcasted_iota(jnp.int32, (1, n_cols), 1)
    jf = j.astype(F32)
    cur = qpos // SEL_BLOCK
    forced = (j == 0) | (j == cur) | (j == cur - 1)
    valid = j * SEL_BLOCK <= qpos
    score = jnp.where(forced, FORCE, jnp.where(valid, imp, -FORCE))
    score = jnp.where(j < n_s, score, -jnp.inf)
    sel = jnp.zeros(score.shape, F32)
    for _ in range(min(N_SEL, n_s)):
        mx = jnp.max(score, axis=-1, keepdims=True)
        idx = jnp.min(jnp.where(score == mx, jf, float(n_cols)), axis=-1, keepdims=True)
        hit = jf == idx
        sel = jnp.where(hit, 1.0, sel)
        score = jnp.where(hit, -jnp.inf, score)
    return sel


SEL_KEY_TILE = 512


def _nsa_prompt_kernel(q_ref, kc_ref, vc_ref, ks_ref, vs_ref, kw_ref, vw_ref, g_ref, ov_ref, o_ref, *, t):
    start = pl.program_id(2) * Q_BLOCK
    n_rows = GROUP * Q_BLOCK
    scale = HEAD_DIM ** -0.5
    q = q_ref[...] * scale
    q4 = jnp.concatenate([q[:, g * HEAD_DIM:(g + 1) * HEAD_DIM] for g in range(GROUP)], axis=0).astype(BF16)
    qpos1 = start + lax.broadcasted_iota(jnp.int32, (Q_BLOCK, 1), 0)
    qpos4 = start + lax.broadcasted_iota(jnp.int32, (n_rows, 1), 0) % Q_BLOCK

    n_cr = kc_ref.shape[0]
    s = _dot_nt(q4, kc_ref[...].astype(BF16))
    m_idx = lax.broadcasted_iota(jnp.int32, (1, n_cr), 1)
    cmask = (m_idx >= 1) & (m_idx * CMP_STRIDE + (CMP_BLOCK - CMP_STRIDE - 1) <= qpos4)
    p_c = _masked_softmax(s, cmask)
    o_c = _dot(p_c.astype(BF16), vc_ref[...].astype(BF16))
    psum = p_c[0:Q_BLOCK]
    for g in range(1, GROUP):
        psum = psum + p_c[g * Q_BLOCK:(g + 1) * Q_BLOCK]
    n_s = ov_ref.shape[1]
    imp = _dot(psum, ov_ref[...], HI)
    sel = _select_blocks(imp, qpos1, n_s, n_s).astype(BF16)

    wlen = WINDOW + Q_BLOCK
    w0 = pl.multiple_of(jnp.clip(start - WINDOW, 0, t - wlen), Q_BLOCK)
    kw = kw_ref[pl.ds(w0, wlen), :].astype(BF16)
    vw = vw_ref[pl.ds(w0, wlen), :].astype(BF16)
    sw = _dot_nt(q4, kw)
    dpos = qpos4 - (w0 + lax.broadcasted_iota(jnp.int32, (1, wlen), 1))
    p_w = _masked_softmax(sw, (dpos >= 0) & (dpos < WINDOW))
    o_w = _dot(p_w.astype(BF16), vw)

    tk = SEL_KEY_TILE
    blk_row = lax.broadcasted_iota(jnp.int32, (n_s, 1), 0)

    def body(kt, carry):
        m, l, acc = carry
        off = pl.multiple_of(kt * tk, tk)
        k_t = ks_ref[pl.ds(off, tk), :].astype(BF16)
        v_t = vs_ref[pl.ds(off, tk), :].astype(BF16)
        sc = _dot_nt(q4, k_t)
        kpos = off + lax.broadcasted_iota(jnp.int32, (1, tk), 1)
        expand = jnp.where(blk_row == kpos // SEL_BLOCK, 1.0, 0.0).astype(BF16)
        selx = _dot(sel, expand)
        bias = jnp.where((selx > 0.5) & (kpos <= qpos1), 0.0, NEG_INF)
        sc = sc + jnp.concatenate([bias] * GROUP, axis=0)
        m_new = jnp.maximum(m, jnp.max(sc, axis=-1, keepdims=True))
        alpha = jnp.exp(m - m_new)
        p = jnp.exp(sc - m_new)
        l = alpha * l + jnp.sum(p, axis=-1, keepdims=True)
        acc = alpha * acc + _dot(p.astype(BF16), v_t)
        return m_new, l, acc

    n_tiles = (start + Q_BLOCK + tk - 1) // tk
    m0 = jnp.full((n_rows, 1), NEG_INF, F32)
    l0 = jnp.zeros((n_rows, 1), F32)
    a0 = jnp.zeros((n_rows, HEAD_DIM), F32)
    _, l, acc = lax.fori_loop(0, n_tiles, body, (m0, l0, a0))
    o_s = acc * (1.0 / jnp.maximum(l, TINY))

    gt = g_ref[...]
    outs = []
    for g in range(GROUP):
        r0, r1 = g * Q_BLOCK, (g + 1) * Q_BLOCK
        outs.append(gt[:, 3 * g:3 * g + 1] * o_c[r0:r1] + gt[:, 3 * g + 1:3 * g + 2] * o_s[r0:r1]
                    + gt[:, 3 * g + 2:3 * g + 3] * o_w[r0:r1])
    o_ref[...] = jnp.concatenate(outs, axis=1).astype(o_ref.dtype)


def _nsa_prompt(y_main, cmp_p, gates_t, bsz, t):
    assert t % SEL_KEY_TILE == 0 and t >= WINDOW + Q_BLOCK
    nq = t // Q_BLOCK
    n_cr = t // CMP_STRIDE
    n_c = (t - CMP_BLOCK) // CMP_STRIDE + 1
    n_s = t // SEL_BLOCK
    assert n_s >= N_SEL
    ov = _overlap_shifted(n_cr, n_c, n_s, n_s)
    cb = NQ_COLS // HEAD_DIM
    seq_col = lambda off: (lambda b, k, i: (b, off + k))
    return pl.pallas_call(
        functools.partial(_nsa_prompt_kernel, t=t),
        grid=(bsz, N_KV_HEADS, nq),
        in_specs=[pl.BlockSpec((Q_BLOCK, QW), lambda b, k, i: (b * nq + i, k)),
                  pl.BlockSpec((None, n_cr, HEAD_DIM), lambda b, k, i: (b, 0, k)),
                  pl.BlockSpec((None, n_cr, HEAD_DIM), lambda b, k, i: (b, 0, N_KV_HEADS + k)),
                  pl.BlockSpec((t, HEAD_DIM), seq_col(cb + 2 * N_KV_HEADS)),
                  pl.BlockSpec((t, HEAD_DIM), seq_col(cb + 3 * N_KV_HEADS)),
                  pl.BlockSpec((t, HEAD_DIM), seq_col(cb + 4 * N_KV_HEADS)),
                  pl.BlockSpec((t, HEAD_DIM), seq_col(cb + 5 * N_KV_HEADS)),
                  pl.BlockSpec((None, Q_BLOCK, 3 * GROUP), lambda b, k, i: (k, b * nq + i, 0)),
                  pl.BlockSpec((n_cr, n_s), lambda b, k, i: (0, 0))],
        out_specs=pl.BlockSpec((Q_BLOCK, QW), lambda b, k, i: (b * nq + i, k)),
        out_shape=jax.ShapeDtypeStruct((bsz * t, NQ_COLS), BF16),
        compiler_params=_params("parallel", "parallel", "arbitrary"),
    )(y_main, cmp_p, cmp_p, y_main, y_main, y_main, y_main, gates_t, ov)


SAMPLE_ROWS = 8


def _nsa_sample_kernel(*refs, past, s_len, n_s, n_in):
    y_ref, cmp_ref = refs[1:3]
    pages = refs[3:3 + n_in]
    win_ref, g_ref, ov_ref, o_ref, q_scr, sel_scr, m_scr, l_scr, acc_scr, oc_scr = refs[3 + n_in:]
    pg = pl.program_id(1)
    n_pages = pl.num_programs(1)
    sr = SAMPLE_ROWS
    rk = GROUP * sr
    scale = HEAD_DIM ** -0.5
    tpos = lax.broadcasted_iota(jnp.int32, (rk, 1), 0) % sr
    qpos = past + tpos
    n_sc = sel_scr.shape[1]
    nbp = past // SEL_BLOCK
    kv0 = NQ_COLS

    @pl.when(pg == 0)
    def _():
        y = y_ref[...]
        n_cr = cmp_ref.shape[0]
        m_idx = lax.broadcasted_iota(jnp.int32, (1, n_cr), 1)
        cmask = (m_idx >= 1) & (m_idx * CMP_STRIDE + (CMP_BLOCK - CMP_STRIDE - 1) <= qpos)
        for k in range(N_KV_HEADS):
            qk = jnp.concatenate([y[:, (k * GROUP + g) * HEAD_DIM:(k * GROUP + g + 1) * HEAD_DIM]
                                  for g in range(GROUP)], axis=0) * scale
            qk = qk.astype(BF16)
            q_scr[k] = qk
            kc = cmp_ref[:, k * HEAD_DIM:(k + 1) * HEAD_DIM].astype(BF16)
            vc = cmp_ref[:, (N_KV_HEADS + k) * HEAD_DIM:(N_KV_HEADS + k + 1) * HEAD_DIM].astype(BF16)
            p_c = _masked_softmax(_dot_nt(qk, kc), cmask)
            oc_scr[k * rk:(k + 1) * rk, :] = _dot(p_c.astype(BF16), vc)
            psum = p_c[0:sr]
            for g in range(1, GROUP):
                psum = psum + p_c[g * sr:(g + 1) * sr]
            imp = _dot(psum, ov_ref[...], HI)
            sel = _select_blocks(imp, qpos[0:sr], n_s, n_sc)
            for g in range(GROUP):
                sel_scr[k * rk + g * sr:k * rk + (g + 1) * sr, :] = sel
        m_scr[...] = jnp.full(m_scr.shape, NEG_INF, F32)
        l_scr[...] = jnp.zeros(l_scr.shape, F32)
        acc_scr[...] = jnp.zeros(acc_scr.shape, F32)

    def online_update(k, sc, mask, v):
        rows = slice(k * rk, (k + 1) * rk)
        sc = jnp.where(mask, sc, NEG_INF)
        m_old = m_scr[rows, :]
        m_new = jnp.maximum(m_old, jnp.max(sc, axis=-1, keepdims=True))
        alpha = jnp.exp(m_old - m_new)
        p = jnp.where(mask, jnp.exp(sc - m_new), 0.0)
        l_scr[rows, :] = alpha * l_scr[rows, :] + jnp.sum(p, axis=-1, keepdims=True)
        acc_scr[rows, :] = alpha * acc_scr[rows, :] + _dot(p.astype(v.dtype), v)
        m_scr[rows, :] = m_new

    n_keys = n_in * PAGE_SIZE
    kpos = pg * n_keys + lax.broadcasted_iota(jnp.int32, (1, n_keys), 1)
    blk_row = lax.broadcasted_iota(jnp.int32, (n_sc, 1), 0)
    expand = jnp.where(blk_row == kpos // SEL_BLOCK, 1.0, 0.0).astype(BF16)
    selx = _dot(sel_scr[...].astype(BF16), expand)
    by_slot = [jnp.swapaxes(pages[i][...], 0, 1).astype(BF16) for i in range(n_in)]
    for k in range(N_KV_HEADS):
        kp = jnp.concatenate([by_slot[i][k] for i in range(n_in)], axis=0)
        vp = jnp.concatenate([by_slot[i][N_KV_HEADS + k] for i in range(n_in)], axis=0)
        mask = (selx[k * rk:(k + 1) * rk] > 0.5) & (kpos <= qpos)
        online_update(k, _dot_nt(q_scr[k], kp), mask, vp)

    @pl.when(pg == n_pages - 1)
    def _():
        y = y_ref[...]
        rpos = lax.broadcasted_iota(jnp.int32, (1, sr), 1)
        new_ok = (rpos < s_len) & (rpos <= tpos)
        gt = g_ref[...]
        win_by_slot = jnp.swapaxes(win_ref[...], 0, 1).astype(BF16)
        for k in range(N_KV_HEADS):
            rows = slice(k * rk, (k + 1) * rk)
            qk = q_scr[k]
            qf = qk.astype(F32)
            c_ks = kv0 + (2 * N_KV_HEADS + k) * HEAD_DIM
            c_vs = kv0 + (3 * N_KV_HEADS + k) * HEAD_DIM
            kn = y[:, c_ks:c_ks + HEAD_DIM]
            vn = y[:, c_vs:c_vs + HEAD_DIM]
            new_sel = sel_scr[rows, nbp:nbp + 1] > 0.5
            online_update(k, _dot_nt(qf, kn), new_sel & new_ok, vn)
            o_s = acc_scr[rows, :] / jnp.maximum(l_scr[rows, :], TINY)
            wb = win_ref.shape[0]
            c_kw = kv0 + (4 * N_KV_HEADS + k) * HEAD_DIM
            c_vw = kv0 + (5 * N_KV_HEADS + k) * HEAD_DIM
            kwb = win_by_slot[k]
            vwb = win_by_slot[N_KV_HEADS + k]
            kwn = y[:, c_kw:c_kw + HEAD_DIM]
            vwn = y[:, c_vw:c_vw + HEAD_DIM]
            dpos = qpos - (past - wb + lax.broadcasted_iota(jnp.int32, (1, wb), 1))
            mask_b = (dpos >= 0) & (dpos < WINDOW)
            s_b = jnp.where(mask_b, _dot_nt(qk, kwb), NEG_INF)
            s_n = jnp.where(new_ok, _dot_nt(qf, kwn), NEG_INF)
            mw = jnp.maximum(jnp.max(s_b, axis=-1, keepdims=True), jnp.max(s_n, axis=-1, keepdims=True))
            p_b = jnp.where(mask_b, jnp.exp(s_b - mw), 0.0)
            p_n = jnp.where(new_ok, jnp.exp(s_n - mw), 0.0)
            den = jnp.sum(p_b, axis=-1, keepdims=True) + jnp.sum(p_n, axis=-1, keepdims=True)
            o_w = (_dot(p_b.astype(BF16), vwb) + _dot(p_n, vwn)) / jnp.maximum(den, TINY)
            o_c = oc_scr[rows, :]
            for g in range(GROUP):
                r0, r1 = g * sr, (g + 1) * sr
                c = (k * GROUP + g) * 3
                o = gt[:, c:c + 1] * o_c[r0:r1] + gt[:, c + 1:c + 2] * o_s[r0:r1] + gt[:, c + 2:c + 3] * o_w[r0:r1]
                o_ref[:, (k * GROUP + g) * HEAD_DIM:(k * GROUP + g + 1) * HEAD_DIM] = o


SAMPLE_PAGES_PER_STEP = 8


def _nsa_sample(ys8, cmp_s, cache5, layer, win4, gates8, page_table, s_len):
    db, n_pages = page_table.shape
    past = n_pages * PAGE_SIZE
    n_in = math.gcd(n_pages, SAMPLE_PAGES_PER_STEP)
    assert s_len <= SAMPLE_ROWS and s_len <= SEL_BLOCK and past % SEL_BLOCK == 0
    n_cr = cmp_s.shape[1]
    n_c = (past + s_len - CMP_BLOCK) // CMP_STRIDE + 1
    assert n_c + 1 <= n_cr
    n_s = -(-(past + s_len) // SEL_BLOCK)
    n_sc = -(-n_s // 128) * 128
    ov = _overlap_shifted(n_cr, n_c, n_s, n_sc)
    wb = win4.shape[1]
    ncol = ys8.shape[2]
    rows = N_KV_HEADS * GROUP * SAMPLE_ROWS
    page_specs = [pl.BlockSpec((None, None, PAGE_SIZE, 2 * N_KV_HEADS, HEAD_DIM),
                               functools.partial(lambda b, p, pt, i: (layer, pt[b, p * n_in + i], 0, 1, 0), i=i))
                  for i in range(n_in)]
    grid_spec = pltpu.PrefetchScalarGridSpec(
        num_scalar_prefetch=1,
        grid=(db, n_pages // n_in),
        in_specs=[pl.BlockSpec((None, SAMPLE_ROWS, ncol), lambda b, p, pt: (b, 0, 0)),
                  pl.BlockSpec((None, n_cr, 2 * NKV_COLS), lambda b, p, pt: (b, 0, 0))]
                 + page_specs
                 + [pl.BlockSpec((None, wb, 2 * N_KV_HEADS, HEAD_DIM), lambda b, p, pt: (b, 0, 0, 0)),
                    pl.BlockSpec((None, SAMPLE_ROWS, 3 * N_HEADS), lambda b, p, pt: (b, 0, 0)),
                    pl.BlockSpec((n_cr, n_sc), lambda b, p, pt: (0, 0))],
        out_specs=pl.BlockSpec((None, SAMPLE_ROWS, NQ_COLS), lambda b, p, pt: (b, 0, 0)),
        scratch_shapes=[pltpu.VMEM((N_KV_HEADS, GROUP * SAMPLE_ROWS, HEAD_DIM), BF16),
                        pltpu.VMEM((rows, n_sc), F32),
                        pltpu.VMEM((rows, 1), F32),
                        pltpu.VMEM((rows, 1), F32),
                        pltpu.VMEM((rows, HEAD_DIM), F32),
                        pltpu.VMEM((rows, HEAD_DIM), F32)],
    )
    return pl.pallas_call(
        functools.partial(_nsa_sample_kernel, past=past, s_len=s_len, n_s=n_s, n_in=n_in),
        grid_spec=grid_spec,
        out_shape=jax.ShapeDtypeStruct((db, SAMPLE_ROWS, NQ_COLS), F32),
        compiler_params=_params("parallel", "arbitrary"),
    )(page_table, ys8, cmp_s, *([cache5] * n_in), win4, gates8, ov)


WKV_DIAG_BLOCK = 16
WKV_CHUNK = 64


def _bdot(a, b):
    return _dot(a.astype(BF16), b.astype(BF16))


def _bdot_nt(a, b):
    return _dot_nt(a.astype(BF16), b.astype(BF16))


def _bdot_tn(a, b):
    return _dot_tn(a.astype(BF16), b.astype(BF16))


def _unit_lower_inverses(mats, c):
    blk = min(WKV_DIAG_BLOCK, c)
    ri = lax.broadcasted_iota(jnp.int32, (c, c), 0)
    ci = lax.broadcasted_iota(jnp.int32, (c, c), 1)
    eye = jnp.where(ri == ci, 1.0, 0.0)
    same = (ri // blk) == (ci // blk)
    dp = [jnp.where(same, a, 0.0) for a in mats]
    td = [eye + d for d in dp]
    for _ in range(int(math.log2(blk)) - 1):
        dp = [_bdot(d, d) for d in dp]
        td = [t + _bdot(d, t) for d, t in zip(dp, td)]
    if blk == c:
        return td
    npow = [_bdot(t, jnp.where(same, 0.0, a)) for t, a in zip(td, mats)]
    tinv = [t + _bdot(x, t) for x, t in zip(npow, td)]
    for _ in range(int(math.log2(c // blk)) - 1):
        npow = [_bdot(x, x) for x in npow]
        tinv = [t + _bdot(x, t) for x, t in zip(npow, tinv)]
    return tinv


def _wkv_kernel(r_ref, k_ref, v_ref, lw_ref, a_ref, g_ref, kk_ref, ka_ref, rk_ref, lg_ref, lb_ref, s0_ref,
                z_ref, s_ref, *, c, nc, hb):
    n = RWKV_HEAD

    @pl.when(pl.program_id(2) == 0)
    def _():
        s_ref[...] = s0_ref[...]

    ri = lax.broadcasted_iota(jnp.int32, (c, c), 0)
    ci = lax.broadcasted_iota(jnp.int32, (c, c), 1)
    strict = ri > ci
    incl = ri >= ci
    tri = jnp.where(incl, 1.0, 0.0)
    lw_all = lw_ref[...]
    cum = jnp.concatenate([_dot(tri, lw_all[i * c:(i + 1) * c], HI) for i in range(nc)], axis=0)
    e_in_all = jnp.exp(cum)
    e_neg_all = jnp.exp(-cum)
    e_ex_all = jnp.exp(cum - lw_all)
    heads = range(hb)
    chunks = range(nc)
    sls = [slice(hh * n, (hh + 1) * n) for hh in heads]
    rws = [slice(i * c, (i + 1) * c) for i in chunks]
    r, v, k2, r_t, a_t, b_t, k_t, vb = ([] for _ in range(8))
    for sl in sls:
        k = k_ref[:, sl]
        a = a_ref[:, sl]
        kk = k * kk_ref[:, sl]
        kk = kk * (1.0 / jnp.maximum(jnp.sqrt(jnp.sum(kk * kk, axis=-1, keepdims=True)), 1e-12))
        r.append(r_ref[:, sl])
        v.append(v_ref[:, sl])
        k2.append(k * (1.0 + (a - 1.0) * ka_ref[:, sl]))
        r_t.append(r[-1] * e_in_all[:, sl])
        a_t.append(-kk * e_ex_all[:, sl])
        b_t.append(kk * a * e_neg_all[:, sl])
        k_t.append(k2[-1] * e_neg_all[:, sl])
        vb.append(v[-1].astype(BF16))

    items = [(hh, i) for hh in heads for i in chunks]
    aa = [_bdot_nt(jnp.concatenate([a_t[hh][rws[i]], r_t[hh][rws[i]]], axis=0),
                   jnp.concatenate([b_t[hh][rws[i]], k_t[hh][rws[i]]], axis=0)) for hh, i in items]
    a_ab = [jnp.where(strict, x[:c, :c], 0.0) for x in aa]
    a_ak = [jnp.where(strict, x[:c, c:], 0.0) for x in aa]
    a_rb = [jnp.where(incl, x[c:, :c], 0.0).astype(BF16) for x in aa]
    a_rk = [jnp.where(incl, x[c:, c:], 0.0) for x in aa]
    akv = [_bdot(x, vb[hh][rws[i]]) for x, (hh, i) in zip(a_ak, items)]
    yv = [_bdot(x, vb[hh][rws[i]]) for x, (hh, i) in zip(a_rk, items)]
    w_c = [e_in_all[i * c + c - 1:(i + 1) * c, sls[hh]] for hh, i in items]
    kv = [_bdot_tn(vb[hh][rws[i]], k_t[hh][rws[i]] * w) for w, (hh, i) in zip(w_c, items)]
    bh = [(b_t[hh][rws[i]] * w).astype(BF16) for w, (hh, i) in zip(w_c, items)]
    tinv = _unit_lower_inverses(a_ab, c)
    wt = [_bdot(t, a_t[hh][rws[i]]) for t, (hh, i) in zip(tinv, items)]
    u = [_bdot(t, x) for t, x in zip(tinv, akv)]
    wr = [jnp.concatenate([w, r_t[hh][rws[i]]], axis=0).astype(BF16) for w, (hh, i) in zip(wt, items)]

    s = [s_ref[hh] for hh in heads]
    ys = [[] for _ in heads]
    for i in chunks:
        idx = [hh * nc + i for hh in heads]
        ls = [_dot_nt(wr[j], s[hh].astype(BF16)) for hh, j in zip(heads, idx)]
        p = [x[:c] + u[j] for x, j in zip(ls, idx)]
        for hh, j in zip(heads, idx):
            ys[hh].append(ls[hh][c:] + _bdot(a_rb[j], p[hh]) + yv[j])
        s = [s[hh] * w_c[j] + _bdot_tn(p[hh], bh[j]) + kv[j] for hh, j in zip(heads, idx)]
    zs = []
    for hh, sl in zip(heads, sls):
        s_ref[hh] = s[hh]
        y = ys[hh][0] if nc == 1 else jnp.concatenate(ys[hh], axis=0)
        mu = jnp.mean(y, axis=-1, keepdims=True)
        yc = y - mu
        yn = yc * lax.rsqrt(jnp.mean(yc * yc, axis=-1, keepdims=True) + LNX_EPS)
        yn = yn * lg_ref[:, sl] + lb_ref[:, sl]
        bonus = jnp.sum(r[hh] * k2[hh] * rk_ref[:, sl], axis=-1, keepdims=True) * v[hh]
        zs.append((yn + bonus) * g_ref[:, sl])
    z_ref[...] = jnp.concatenate(zs, axis=1).astype(z_ref.dtype)


def _wkv(rkv, lw, a, g, k_k, k_a, r_k, lnx_g, lnx_b, s0, n_seq, t, c, nc, hb):
    rows, d = lw.shape
    h = d // RWKV_HEAD
    hw = hb * RWKV_HEAD
    tb = nc * c
    nblk = t // tb
    tok = lambda s, hg, ch: (s * nblk + ch, hg)
    par = lambda s, hg, ch: (0, hg)
    st = lambda s, hg, ch: (s, hg, 0, 0)
    rkv_spec = lambda p: pl.BlockSpec((None, tb, hw), lambda s, hg, ch: (p, s * nblk + ch, hg))
    return pl.pallas_call(
        functools.partial(_wkv_kernel, c=c, nc=nc, hb=hb),
        grid=(n_seq, h // hb, nblk),
        in_specs=[rkv_spec(0), rkv_spec(1), rkv_spec(2),
                  pl.BlockSpec((tb, hw), tok), pl.BlockSpec((tb, hw), tok), pl.BlockSpec((tb, hw), tok),
                  pl.BlockSpec((1, hw), par), pl.BlockSpec((1, hw), par), pl.BlockSpec((1, hw), par),
                  pl.BlockSpec((1, hw), par), pl.BlockSpec((1, hw), par),
                  pl.BlockSpec((None, hb, RWKV_HEAD, RWKV_HEAD), st)],
        out_specs=[pl.BlockSpec((tb, hw), tok),
                   pl.BlockSpec((None, hb, RWKV_HEAD, RWKV_HEAD), st)],
        out_shape=[jax.ShapeDtypeStruct((rows, d), BF16),
                   jax.ShapeDtypeStruct((n_seq, h, RWKV_HEAD, RWKV_HEAD), F32)],
        compiler_params=_params("parallel", "parallel", "arbitrary"),
    )(rkv, rkv, rkv, lw, a, g, k_k, k_a, r_k, lnx_g, lnx_b, s0)


WKV_PROMPT_CHUNKS_PER_STEP = 4
WKV_PROMPT_HEADS_PER_STEP = 4


def _nsa_layer(x, np_tok, bsz, t, db, s_len, cache5, layer, win, page_table, w_in, pe, w1, w2, w_out,
               ln_g, ln_b, alpha):
    d = x.shape[1]
    n_main = NQ_COLS + 6 * NKV_COLS
    y_main = _mm(x, w_in, n_main, tn_pref=1024)
    w_gate = jnp.pad(w_in[:, n_main:], ((0, 0), (0, 128 - 3 * N_HEADS)))
    gates = _mm(x, w_gate, 128, act="sigmoid")[:, :3 * N_HEADS]

    wc, w2b = _cmp_weights(w1, w2)
    bias = _cmp_bias(pe, w1)
    cmp_p = _cmp_prompt(y_main, bsz, t, wc, bias, w2b)
    cmp_s = _cmp_sample(cache5, layer, page_table, wc, bias, w2b)

    gates_p = gates[:np_tok].reshape(np_tok, N_KV_HEADS, 3 * GROUP).transpose(1, 0, 2)
    o_p = _nsa_prompt(y_main, cmp_p, gates_p, bsz, t)

    pad_rows = ((0, 0), (0, SAMPLE_ROWS - s_len), (0, 0))
    ys = y_main[np_tok:].reshape(db, s_len, n_main)
    ys8 = jnp.pad(ys, pad_rows)
    gates8 = jnp.pad(gates[np_tok:].reshape(db, s_len, 3 * N_HEADS), pad_rows)
    wb = win.shape[1]
    win4 = win.reshape(db, wb, 2 * N_KV_HEADS, HEAD_DIM)
    o_s = _nsa_sample(ys8, cmp_s, cache5, layer, win4, gates8, page_table, s_len)
    o_s = o_s[:, :s_len].reshape(db * s_len, NQ_COLS).astype(BF16)

    o = jnp.concatenate([o_p, o_s], axis=0)
    h = _mm_ln(o, w_out, x, ln_g, ln_b, alpha)

    yp = y_main[:np_tok].reshape(bsz, t, n_main)
    kv_p = yp[:, :, NQ_COLS:NQ_COLS + 4 * NKV_COLS].reshape(bsz, t, 4, N_KV_HEADS, HEAD_DIM)
    wlen = min(WINDOW, t)
    win_p = yp[:, t - wlen:, NQ_COLS + 4 * NKV_COLS:].reshape(bsz, wlen, 2, N_KV_HEADS, HEAD_DIM)
    kv_s = ys[:, :, NQ_COLS:NQ_COLS + 4 * NKV_COLS].reshape(db, s_len, 4, N_KV_HEADS, HEAD_DIM)
    kvw_s = ys[:, :, NQ_COLS + 4 * NKV_COLS:].reshape(db, s_len, 2, N_KV_HEADS, HEAD_DIM)
    win_s = jnp.concatenate([win, kvw_s], axis=1)[:, s_len:]
    return h, kv_p, kv_s, win_p, win_s


def _rwkv_layer(x, np_tok, bsz, t, db, s_len, shift_s, wkv_s, mu, w_rkv, w0, w1, w2, a0, a1, a2, g1, g2,
                k_k, k_a, r_k, lnx_g, lnx_b, w_out, ln_g, ln_b, alpha):
    d = x.shape[1]
    heads = d // RWKV_HEAD
    xp3 = x[:np_tok].reshape(bsz, t, d)
    xs3 = x[np_tok:].reshape(db, s_len, d)
    prev_p = jnp.concatenate([jnp.zeros((bsz, 1, d), x.dtype), xp3[:, :-1]], axis=1)
    prev_s = jnp.concatenate([shift_s[:, None].astype(x.dtype), xs3[:, :-1]], axis=1)
    xprev = jnp.concatenate([prev_p.reshape(np_tok, d), prev_s.reshape(db * s_len, d)], axis=0)

    row = lambda v: v.reshape(1, -1)
    rkv = _mm_mix(x, xprev, mu[:3, None, :], w_rkv)
    lw = _lora(x, xprev, row(mu[3]), w1, w2, row(w0), "decay")
    a = _lora(x, xprev, row(mu[4]), a1, a2, row(a0), "aaa")
    g = _lora(x, xprev, row(mu[5]), g1, g2, jnp.zeros((1, d), F32), "gate")

    pvec = (row(k_k), row(k_a), row(r_k), row(lnx_g), row(lnx_b))
    c_p = _tile(t, WKV_CHUNK, 8)
    nc_p = math.gcd(t // c_p, WKV_PROMPT_CHUNKS_PER_STEP)
    hb_p = math.gcd(heads, WKV_PROMPT_HEADS_PER_STEP)
    zero_state = jnp.zeros((bsz, heads, RWKV_HEAD, RWKV_HEAD), F32)
    z_p, st_p = _wkv(rkv[:, :np_tok], lw[:np_tok], a[:np_tok], g[:np_tok], *pvec, zero_state, bsz, t,
                     c_p, nc_p, hb_p)

    c_s = -(-s_len // 8) * 8
    pad = lambda v: jnp.pad(v.reshape(v.shape[:-2] + (db, s_len, d)),
                            [(0, 0)] * (v.ndim - 1) + [(0, c_s - s_len), (0, 0)]
                            ).reshape(v.shape[:-2] + (db * c_s, d))
    z_s, st_s = _wkv(pad(rkv[:, np_tok:]), pad(lw[np_tok:]), pad(a[np_tok:]), pad(g[np_tok:]), *pvec,
                     wkv_s.astype(F32), db, c_s, c_s, 1, heads)
    z_s = z_s.reshape(db, c_s, d)[:, :s_len].reshape(db * s_len, d)

    z = jnp.concatenate([z_p, z_s], axis=0)
    h = _mm_ln(z, w_out, x, ln_g, ln_b, alpha)
    return h, st_p, st_s, xp3[:, -1], xs3[:, -1]


def _mlp(h, w_up, w_down, ln_g, ln_b, alpha):
    d_ff = w_up.shape[1]
    u = _mm(h, w_up, d_ff, act="relu2", out_dtype=BF16)
    return _mm_ln(u, w_down, h, ln_g, ln_b, alpha)


def kernel(x_prompt, x_sample, cache_nsa_kv, state_nsa_win, state_rwkv_wkv, state_rwkv_shift, page_table,
           nsa_w_in, nsa_cmp_pe, nsa_cmp_w1, nsa_cmp_w2, nsa_w_out,
           rwkv_mu, rwkv_w_rkv, rwkv_w0, rwkv_w1, rwkv_w2, rwkv_a0, rwkv_a1, rwkv_a2, rwkv_g1, rwkv_g2,
           rwkv_k_k, rwkv_k_a, rwkv_r_k, rwkv_lnx_g, rwkv_lnx_b, rwkv_w_out,
           ffn_w_up, ffn_w_down, ln_g, ln_b):
    bsz, t, d = x_prompt.shape
    db, s_len, _ = x_sample.shape
    depth = ffn_w_up.shape[0]
    alpha = (2 * depth) ** 0.25
    np_tok = bsz * t
    x = jnp.concatenate([x_prompt.reshape(np_tok, d), x_sample.reshape(db * s_len, d)], axis=0)
    n_l, n_pool = cache_nsa_kv.shape[:2]
    cache5 = cache_nsa_kv.reshape(n_l, n_pool, PAGE_SIZE, 4 * N_KV_HEADS, HEAD_DIM)
    page_table = page_table.astype(jnp.int32)

    kv_p, kv_s, win_p, win_s, wkv_p, wkv_s, sh_p, sh_s = ([] for _ in range(8))
    row = lambda v: v.reshape(1, -1)
    for i in range(depth):
        j = i // 2
        if i % 2 == 0:
            h, kvp_new, kvs_new, wp_new, ws_new = _nsa_layer(
                x, np_tok, bsz, t, db, s_len, cache5, j, state_nsa_win[j], page_table,
                nsa_w_in[j], nsa_cmp_pe[j], nsa_cmp_w1[j], nsa_cmp_w2[j], nsa_w_out[j],
                row(ln_g[i, 0]), row(ln_b[i, 0]), alpha)
            kv_p.append(kvp_new)
            kv_s.append(kvs_new)
            win_p.append(wp_new)
            win_s.append(ws_new)
        else:
            h, sp_new, ss_new, hp_new, hs_new = _rwkv_layer(
                x, np_tok, bsz, t, db, s_len, state_rwkv_shift[j], state_rwkv_wkv[j],
                rwkv_mu[j], rwkv_w_rkv[j], rwkv_w0[j], rwkv_w1[j], rwkv_w2[j], rwkv_a0[j], rwkv_a1[j],
                rwkv_a2[j], rwkv_g1[j], rwkv_g2[j], rwkv_k_k[j], rwkv_k_a[j], rwkv_r_k[j],
                rwkv_lnx_g[j], rwkv_lnx_b[j], rwkv_w_out[j], row(ln_g[i, 0]), row(ln_b[i, 0]), alpha)
            wkv_p.append(sp_new)
            wkv_s.append(ss_new)
            sh_p.append(hp_new)
            sh_s.append(hs_new)
        x = _mlp(h, ffn_w_up[i], ffn_w_down[i], row(ln_g[i, 1]), row(ln_b[i, 1]), alpha)
    return (x[:np_tok].reshape(bsz, t, d), x[np_tok:].reshape(db, s_len, d),
            jnp.stack(kv_p), jnp.stack(kv_s), jnp.stack(win_p), jnp.stack(win_s),
            jnp.stack(wkv_p), jnp.stack(wkv_s), jnp.stack(sh_p), jnp.stack(sh_s))
```

```python
import functools
import math

import jax
import jax.numpy as jnp
import numpy as np
from jax import lax
from jax.experimental import pallas as pl
from jax.experimental.pallas import tpu as pltpu

F32 = jnp.float32
BF16 = jnp.bfloat16

HEAD_DIM = 128
N_KV_HEADS = 4
GROUP = 4
N_HEADS = N_KV_HEADS * GROUP
QW = GROUP * HEAD_DIM
NQ_COLS = N_HEADS * HEAD_DIM
NKV_COLS = N_KV_HEADS * HEAD_DIM
CMP_BLOCK = 32
CMP_STRIDE = 16
SEL_BLOCK = 64
N_SEL = 16
WINDOW = 512
Q_BLOCK = 128
PAGE_SIZE = 128
RWKV_HEAD = 64
LN_EPS = 1e-5
LNX_EPS = 64e-5
FORCE = 1e4
NEG_INF = -1e30
TINY = 1e-30
CHUNKS_PER_PAGE = PAGE_SIZE // CMP_STRIDE

VMEM_LIMIT_BYTES = 56 * 1024 * 1024
HI = lax.Precision.HIGHEST


def _params(*sem):
    return pltpu.CompilerParams(dimension_semantics=sem, vmem_limit_bytes=VMEM_LIMIT_BYTES)


def _tile(n, pref, mult=16):
    if n <= pref:
        return n
    for t in range(pref - pref % mult, 0, -mult):
        if n % t == 0:
            return t
    return n


def _dot(a, b, precision=None):
    return jnp.dot(a, b, preferred_element_type=F32, precision=precision)


def _dot_nt(a, b, precision=None):
    return lax.dot_general(a, b, (((1,), (1,)), ((), ())), preferred_element_type=F32, precision=precision)


def _dot_tn(a, b, precision=None):
    return lax.dot_general(a, b, (((0,), (0,)), ((), ())), preferred_element_type=F32, precision=precision)


def _masked_softmax(s, mask):
    s = jnp.where(mask, s, NEG_INF)
    m = jnp.max(s, axis=-1, keepdims=True)
    p = jnp.where(mask, jnp.exp(s - m), 0.0)
    return p * (1.0 / jnp.maximum(jnp.sum(p, axis=-1, keepdims=True), TINY))


def _layer_norm_rows(s, g, b, eps):
    mu = jnp.mean(s, axis=-1, keepdims=True)
    c = s - mu
    var = jnp.mean(c * c, axis=-1, keepdims=True)
    return c * lax.rsqrt(var + eps) * g + b


def _mm_kernel(x_ref, w_ref, o_ref, xb_ref, *, act):
    @pl.when(pl.program_id(1) == 0)
    def _():
        xb_ref[...] = x_ref[...].astype(BF16)

    y = _dot(xb_ref[...], w_ref[...].astype(BF16))
    if act == "relu2":
        y = jnp.maximum(y, 0.0)
        y = y * y
    elif act == "sigmoid":
        y = jax.nn.sigmoid(y)
    o_ref[...] = y.astype(o_ref.dtype)


def _mm(x, w, layer, n_out, *, act=None, out_dtype=F32, tm_pref=1040, tn_pref=1024):
    m, k = x.shape
    tm = _tile(m, tm_pref)
    tn = _tile(n_out, tn_pref, 128)
    return pl.pallas_call(
        functools.partial(_mm_kernel, act=act),
        grid=(m // tm, n_out // tn),
        in_specs=[pl.BlockSpec((tm, k), lambda i, j: (i, 0)),
                  pl.BlockSpec((None, k, tn), lambda i, j: (layer, 0, j))],
        out_specs=pl.BlockSpec((tm, tn), lambda i, j: (i, j)),
        out_shape=jax.ShapeDtypeStruct((m, n_out), out_dtype),
        scratch_shapes=[pltpu.VMEM((tm, k), BF16)],
        compiler_params=_params("parallel", "arbitrary"),
    )(x, w)


def _mm_mix_kernel(x_ref, xp_ref, mu_ref, w_ref, o_ref, xb_ref):
    @pl.when(pl.program_id(2) == 0)
    def _():
        x = x_ref[...]
        xb_ref[...] = (x + (xp_ref[...] - x) * mu_ref[...]).astype(BF16)

    o_ref[...] = _dot(xb_ref[...], w_ref[...].astype(BF16))


def _mm_mix(x, xp, mu, w, *, tm_pref=832, tn_pref=512):
    m, k = x.shape
    npar, _, n = w.shape
    tm = _tile(m, tm_pref)
    tn = _tile(n, tn_pref, 128)
    return pl.pallas_call(
        _mm_mix_kernel,
        grid=(m // tm, npar, n // tn),
        in_specs=[pl.BlockSpec((tm, k), lambda i, p, j: (i, 0)),
                  pl.BlockSpec((tm, k), lambda i, p, j: (i, 0)),
                  pl.BlockSpec((None, 1, k), lambda i, p, j: (p, 0, 0)),
                  pl.BlockSpec((None, k, tn), lambda i, p, j: (p, 0, j))],
        out_specs=pl.BlockSpec((None, tm, tn), lambda i, p, j: (p, i, j)),
        out_shape=jax.ShapeDtypeStruct((npar, m, n), F32),
        scratch_shapes=[pltpu.VMEM((tm, k), BF16)],
        compiler_params=_params("parallel", "arbitrary", "arbitrary"),
    )(x, xp, mu, w)


def _lora_kernel(x_ref, xp_ref, mu_ref, a_ref, b_ref, bias_ref, o_ref, *, kind):
    x = x_ref[...]
    xm = (x + (xp_ref[...] - x) * mu_ref[...]).astype(BF16)
    h = _dot(xm, a_ref[...].astype(BF16))
    if kind == "decay":
        h = jnp.tanh(h)
    elif kind == "gate":
        h = jax.nn.sigmoid(h)
    z = _dot(h.astype(BF16), b_ref[...].astype(BF16))
    if kind == "decay":
        u = -(z + bias_ref[...])
        softplus = jnp.maximum(u, 0.0) + jnp.log(1.0 + jnp.exp(-jnp.abs(u)))
        z = -jnp.exp(-softplus - 0.5)
    elif kind == "aaa":
        z = jax.nn.sigmoid(z + bias_ref[...])
    o_ref[...] = z


def _lora(x, xp, mu, a, b, bias, kind, *, tm_pref=320):
    m, k = x.shape
    r = a.shape[1]
    n = b.shape[1]
    tm = _tile(m, tm_pref)
    row = lambda i: (i, 0)
    fixed = lambda i: (0, 0)
    return pl.pallas_call(
        functools.partial(_lora_kernel, kind=kind),
        grid=(m // tm,),
        in_specs=[pl.BlockSpec((tm, k), row), pl.BlockSpec((tm, k), row), pl.BlockSpec((1, k), fixed),
                  pl.BlockSpec((k, r), fixed), pl.BlockSpec((r, n), fixed), pl.BlockSpec((1, n), fixed)],
        out_specs=pl.BlockSpec((tm, n), row),
        out_shape=jax.ShapeDtypeStruct((m, n), F32),
        compiler_params=_params("parallel"),
    )(x, xp, mu, a, b, bias)


def _mm_ln_kernel(z_ref, w_ref, res_ref, g_ref, b_ref, o_ref, *, alpha):
    kk = pl.program_id(1)
    part = _dot(z_ref[...].astype(BF16), w_ref[...].astype(BF16))

    @pl.when(kk == 0)
    def _():
        o_ref[...] = part

    @pl.when(kk > 0)
    def _():
        o_ref[...] += part

    @pl.when(kk == pl.num_programs(1) - 1)
    def _():
        s = alpha * res_ref[...] + o_ref[...]
        o_ref[...] = _layer_norm_rows(s, g_ref[...], b_ref[...], LN_EPS)


def _mm_ln(z, w, layer, res, g, b, alpha, *, tm_pref=640, tk_pref=1024):
    m, k = z.shape
    n = w.shape[2]
    tm = _tile(m, tm_pref)
    tk = _tile(k, tk_pref, 128)
    return pl.pallas_call(
        functools.partial(_mm_ln_kernel, alpha=alpha),
        grid=(m // tm, k // tk),
        in_specs=[pl.BlockSpec((tm, tk), lambda i, kk: (i, kk)),
                  pl.BlockSpec((None, tk, n), lambda i, kk: (layer, kk, 0)),
                  pl.BlockSpec((tm, n), lambda i, kk: (i, 0)),
                  pl.BlockSpec((1, n), lambda i, kk: (0, 0)),
                  pl.BlockSpec((1, n), lambda i, kk: (0, 0))],
        out_specs=pl.BlockSpec((tm, n), lambda i, kk: (i, 0)),
        out_shape=jax.ShapeDtypeStruct((m, n), F32),
        compiler_params=_params("parallel", "arbitrary"),
    )(z, w, res, g, b)


def _cmp_bias_kernel(pe_ref, w1_ref, o_ref):
    acc = jnp.zeros((1, HEAD_DIM), F32)
    for rj in range(CMP_BLOCK):
        acc = acc + _dot(pe_ref[rj:rj + 1, :], w1_ref[rj], HI)
    o_ref[...] = acc


def _cmp_bias(pe, w1):
    return pl.pallas_call(
        _cmp_bias_kernel,
        grid=(2,),
        in_specs=[pl.BlockSpec((None, CMP_BLOCK, HEAD_DIM), lambda p: (p, 0, 0)),
                  pl.BlockSpec((None, CMP_BLOCK, HEAD_DIM, HEAD_DIM), lambda p: (p, 0, 0, 0))],
        out_specs=pl.BlockSpec((None, 1, HEAD_DIM), lambda p: (p, 0, 0)),
        out_shape=jax.ShapeDtypeStruct((2, 1, HEAD_DIM), F32),
        compiler_params=_params("parallel"),
    )(pe, w1)


def _cmp_kernel(*refs, n_in):
    pages = refs[-5 - n_in:-5]
    wc_ref, bias_ref, w2_ref, out_ref, carry_ref = refs[-5:]
    mk = CHUNKS_PER_PAGE * n_in
    rows = N_KV_HEADS * mk

    @pl.when(pl.program_id(1) == 0)
    def _():
        carry_ref[...] = jnp.zeros_like(carry_ref)

    first = (lax.broadcasted_iota(jnp.int32, (rows, 1), 0) % mk) == 0
    if len(pages[0].shape) == 3:
        split = [pages[i][...].reshape(CHUNKS_PER_PAGE, CMP_STRIDE, 2 * N_KV_HEADS, HEAD_DIM)
                 for i in range(n_in)]
        by_slot = [[jnp.swapaxes(split[i][:, j], 0, 1) for j in range(CMP_STRIDE)] for i in range(n_in)]
        piece = lambda i, j, slot: by_slot[i][j][slot]
    else:
        r_out = lax.broadcasted_iota(jnp.int32, (PAGE_SIZE, PAGE_SIZE), 0)
        r_in = lax.broadcasted_iota(jnp.int32, (PAGE_SIZE, PAGE_SIZE), 1)
        perm = jnp.where(r_in == (r_out % CHUNKS_PER_PAGE) * CMP_STRIDE + r_out // CHUNKS_PER_PAGE, 1.0, 0.0)
        grouped = [_dot(perm.astype(BF16), pages[i][...].astype(BF16)) for i in range(n_in)]
        piece = lambda i, j, slot: grouped[i][j * CHUNKS_PER_PAGE:(j + 1) * CHUNKS_PER_PAGE,
                                              slot * HEAD_DIM:(slot + 1) * HEAD_DIM]
    for p in range(2):
        acc = jnp.zeros((rows, 2 * HEAD_DIM), F32)
        for j in range(CMP_STRIDE):
            lhs = jnp.concatenate([piece(i, j, p * N_KV_HEADS + k)
                                   for k in range(N_KV_HEADS) for i in range(n_in)], axis=0)
            acc = acc + _dot(lhs.astype(BF16), wc_ref[p, j])
        part0 = acc[:, :HEAD_DIM]
        part1 = acc[:, HEAD_DIM:]
        prev = jnp.where(first, carry_ref[p], pltpu.roll(part0, 1, 0))
        carry_ref[p] = pltpu.roll(part0, rows - (mk - 1), 0)
        h = prev + part1 + bias_ref[p]
        c = _dot(jax.nn.gelu(h).astype(BF16), w2_ref[p])
        for k in range(N_KV_HEADS):
            col = (p * N_KV_HEADS + k) * HEAD_DIM
            out_ref[:, col:col + HEAD_DIM] = c[k * mk:(k + 1) * mk]


def _cmp_weights(w1, w2):
    n_r = CMP_BLOCK // CMP_STRIDE
    w1r = w1.reshape(2, n_r, CMP_STRIDE, HEAD_DIM, HEAD_DIM)
    wc = jnp.concatenate([w1r[:, r] for r in range(n_r)], axis=-1).astype(BF16)
    return wc, w2.astype(BF16)


def _cmp_call(n_seq, n_steps, n_in, page_specs, operands, wc, bias, w2b, num_prefetch, prefetch):
    mk = CHUNKS_PER_PAGE * n_in
    if num_prefetch:
        fixed3 = lambda s, t, pt: (0, 0, 0)
        fixed4 = lambda s, t, pt: (0, 0, 0, 0)
        out_map = lambda s, t, pt: (s, t, 0)
    else:
        fixed3 = lambda s, t: (0, 0, 0)
        fixed4 = lambda s, t: (0, 0, 0, 0)
        out_map = lambda s, t: (s, t, 0)
    grid_spec = pltpu.PrefetchScalarGridSpec(
        num_scalar_prefetch=num_prefetch,
        grid=(n_seq, n_steps),
        in_specs=page_specs + [
            pl.BlockSpec((2, CMP_STRIDE, HEAD_DIM, 2 * HEAD_DIM), fixed4),
            pl.BlockSpec((2, 1, HEAD_DIM), fixed3),
            pl.BlockSpec((2, HEAD_DIM, HEAD_DIM), fixed3)],
        out_specs=pl.BlockSpec((None, mk, 2 * NKV_COLS), out_map),
        scratch_shapes=[pltpu.VMEM((2, N_KV_HEADS * mk, HEAD_DIM), F32)],
    )
    return pl.pallas_call(
        functools.partial(_cmp_kernel, n_in=n_in),
        grid_spec=grid_spec,
        out_shape=jax.ShapeDtypeStruct((n_seq, n_steps * mk, 2 * NKV_COLS), F32),
        compiler_params=_params("parallel", "arbitrary"),
    )(*prefetch, *operands, wc, bias, w2b)


CMP_PAGES_PER_STEP = 8


def _cmp_prompt(y_main, bsz, t, wc, bias, w2b):
    pages_per_seq = t // PAGE_SIZE
    n_in = math.gcd(pages_per_seq, CMP_PAGES_PER_STEP)
    n_steps = pages_per_seq // n_in
    col_blk = NQ_COLS // (2 * NKV_COLS)
    specs = [pl.BlockSpec((PAGE_SIZE, 2 * NKV_COLS),
                          functools.partial(lambda s, st, i: (s * pages_per_seq + st * n_in + i, col_blk), i=i))
             for i in range(n_in)]
    return _cmp_call(bsz, n_steps, n_in, specs, [y_main] * n_in, wc, bias, w2b, 0, ())


def _cmp_sample(cache5, layer, page_table, wc, bias, w2b):
    n_in = math.gcd(page_table.shape[1], CMP_PAGES_PER_STEP)
    db, n_pages = page_table.shape
    n_steps = n_pages // n_in
    specs = [pl.BlockSpec((None, None, PAGE_SIZE, 2 * N_KV_HEADS, HEAD_DIM),
                          functools.partial(lambda s, st, pt, i: (layer, pt[s, st * n_in + i], 0, 0, 0), i=i))
             for i in range(n_in)]
    return _cmp_call(db, n_steps, n_in, specs, [cache5] * n_in, wc, bias, w2b, 1, (page_table,))


def _overlap_shifted(n_rows, n_c, n_s, n_cols):
    m = np.arange(n_rows)[:, None]
    j = np.arange(n_cols)[None, :]
    c0 = (m - 1) * CMP_STRIDE
    s0 = j * SEL_BLOCK
    ov = (c0 < s0 + SEL_BLOCK) & (c0 + CMP_BLOCK > s0) & (m >= 1) & (m <= n_c) & (j < n_s)
    return jnp.asarray(ov.astype(np.float32))


def _select_blocks(imp, qpos, n_s, n_cols):
    j = lax.broadcasted_iota(jnp.int32, (1, n_cols), 1)
    jf = j.astype(F32)
    cur = qpos // SEL_BLOCK
    forced = (j == 0) | (j == cur) | (j == cur - 1)
    valid = j * SEL_BLOCK <= qpos
    score = jnp.where(forced, FORCE, jnp.where(valid, imp, -FORCE))
    score = jnp.where(j < n_s, score, -jnp.inf)
    sel = jnp.zeros(score.shape, F32)
    for _ in range(min(N_SEL, n_s)):
        mx = jnp.max(score, axis=-1, keepdims=True)
        idx = jnp.min(jnp.where(score == mx, jf, float(n_cols)), axis=-1, keepdims=True)
        hit = jf == idx
        sel = jnp.where(hit, 1.0, sel)
        score = jnp.where(hit, -jnp.inf, score)
    return sel


SEL_KEY_TILE = 512


def _softmax_cols(s, mask):
    s = jnp.where(mask, s, NEG_INF)
    m = jnp.max(s, axis=0, keepdims=True)
    p = jnp.where(mask, jnp.exp(s - m), 0.0)
    return p * (1.0 / jnp.maximum(jnp.sum(p, axis=0, keepdims=True), TINY))


def _select_blocks_cols(imp, qpos, n_s):
    j = lax.broadcasted_iota(jnp.int32, (n_s, 1), 0)
    cur = qpos // SEL_BLOCK
    forced = (j == 0) | (j == cur) | (j == cur - 1)
    valid = j * SEL_BLOCK <= qpos
    score = jnp.where(forced, FORCE, jnp.where(valid, imp, -FORCE))
    sub = 8
    blocks = [score[b * sub:(b + 1) * sub] for b in range(n_s // sub)]
    ranks = [jnp.zeros(blk.shape, F32) for blk in blocks]
    row = lax.broadcasted_iota(jnp.int32, (sub, 1), 0)
    for i in range(n_s):
        bi, ri = divmod(i, sub)
        si = blocks[bi][ri:ri + 1, :]
        for b, blk in enumerate(blocks):
            if b < bi:
                beats = si > blk
            elif b > bi:
                beats = si >= blk
            else:
                beats = jnp.where(row > ri, jnp.where(si >= blk, 1.0, 0.0), jnp.where(si > blk, 1.0, 0.0)) > 0.5
            ranks[b] = ranks[b] + jnp.where(beats, 1.0, 0.0)
    rank = jnp.concatenate(ranks, axis=0)
    return jnp.where(rank < float(min(N_SEL, n_s)), 1.0, 0.0)


def _nsa_prompt_kernel(q_ref, kc_ref, vct_ref, ks_ref, vst_ref, kw_ref, vwt_ref, g_ref, ovt_ref, o_ref,
                       selbias_ref, *, t):
    start = pl.program_id(2) * Q_BLOCK
    n_cols = GROUP * Q_BLOCK
    scale = HEAD_DIM ** -0.5
    q = q_ref[...] * scale
    q4 = jnp.concatenate([q[:, g * HEAD_DIM:(g + 1) * HEAD_DIM] for g in range(GROUP)], axis=0).astype(BF16)
    qpos1 = start + lax.broadcasted_iota(jnp.int32, (1, Q_BLOCK), 1)
    qpos4 = start + lax.broadcasted_iota(jnp.int32, (1, n_cols), 1) % Q_BLOCK

    n_cr = kc_ref.shape[0]
    n_s = ovt_ref.shape[0]
    s = _dot_nt(kc_ref[...].astype(BF16), q4)
    m_idx = lax.broadcasted_iota(jnp.int32, (n_cr, 1), 0)
    cmask = (m_idx >= 1) & (m_idx * CMP_STRIDE + (CMP_BLOCK - CMP_STRIDE - 1) <= qpos4)
    p_c = _softmax_cols(s, cmask)
    o_c = _dot(vct_ref[...].astype(BF16), p_c.astype(BF16))
    psum = p_c[:, 0:Q_BLOCK]
    for g in range(1, GROUP):
        psum = psum + p_c[:, g * Q_BLOCK:(g + 1) * Q_BLOCK]
    imp = _dot(ovt_ref[...], psum, HI)
    sel = _select_blocks_cols(imp, qpos1, n_s)

    wlen = WINDOW + Q_BLOCK
    w0 = pl.multiple_of(jnp.clip(start - WINDOW, 0, t - wlen), Q_BLOCK)
    sw = _dot_nt(kw_ref[pl.ds(w0, wlen), :], q4)
    dpos = qpos4 - (w0 + lax.broadcasted_iota(jnp.int32, (wlen, 1), 0))
    p_w = _softmax_cols(sw, (dpos >= 0) & (dpos < WINDOW))
    wblk = w0 // PAGE_SIZE
    vwt = jnp.concatenate([vwt_ref[wblk + i] for i in range(wlen // PAGE_SIZE)], axis=1)
    o_w = _dot(vwt, p_w.astype(BF16))

    tk = SEL_KEY_TILE
    blocks_per_tile = tk // SEL_BLOCK
    selbias_ref[...] = jnp.where(sel > 0.5, 0.0, NEG_INF)

    def scores(kt):
        return _dot_nt(ks_ref[pl.ds(pl.multiple_of(kt * tk, tk), tk), :], q4)

    def tile(kt, carry, last):
        m, l, acc, sc = carry
        sc_next = sc if last else scores(kt + 1)
        off = kt * tk
        rows = selbias_ref[pl.ds(pl.multiple_of(kt * blocks_per_tile, blocks_per_tile), blocks_per_tile), :]
        bias = jnp.concatenate([jnp.broadcast_to(rows[i:i + 1, :], (SEL_BLOCK, Q_BLOCK))
                                for i in range(blocks_per_tile)], axis=0)
        if last:
            kpos = off + lax.broadcasted_iota(jnp.int32, (tk, Q_BLOCK), 0)
            bias = jnp.where(kpos <= qpos1, bias, NEG_INF)
        sc = sc + jnp.concatenate([bias] * GROUP, axis=1)
        m_new = jnp.maximum(m, jnp.max(sc, axis=0, keepdims=True))
        alpha = jnp.exp(m - m_new)
        p = jnp.exp(sc - m_new)
        l = alpha * l + jnp.sum(p, axis=0, keepdims=True)
        vblk = kt * (tk // PAGE_SIZE)
        vst = jnp.concatenate([vst_ref[vblk + i] for i in range(tk // PAGE_SIZE)], axis=1)
        acc = alpha * acc + _dot(vst, p.astype(BF16))
        return m_new, l, acc, sc_next

    n_tiles = (start + Q_BLOCK + tk - 1) // tk
    m0 = jnp.full((1, n_cols), NEG_INF, F32)
    l0 = jnp.zeros((1, n_cols), F32)
    a0 = jnp.zeros((HEAD_DIM, n_cols), F32)
    carry = lax.fori_loop(0, n_tiles - 1, functools.partial(tile, last=False), (m0, l0, a0, scores(0)))
    _, l, acc, _ = tile(n_tiles - 1, carry, last=True)
    o_s = acc * (1.0 / jnp.maximum(l, TINY))

    gt = g_ref[...]
    o = gt[0:1, :] * o_c + gt[1:2, :] * o_s + gt[2:3, :] * o_w
    o_ref[...] = jnp.concatenate([o[:, g * Q_BLOCK:(g + 1) * Q_BLOCK].T for g in range(GROUP)],
                                 axis=1).astype(o_ref.dtype)


def _keys_bf16(y_main, np_tok, col0):
    return lax.slice(y_main, (0, col0), (np_tok, col0 + NKV_COLS)).astype(BF16)


def _values_t_bf16(y_main, bsz, t, col0):
    v = lax.slice(y_main, (0, col0), (bsz * t, col0 + NKV_COLS)).astype(BF16)
    v = v.reshape(bsz, t // PAGE_SIZE, PAGE_SIZE, N_KV_HEADS, HEAD_DIM)
    return v.transpose(0, 3, 1, 4, 2)


def _nsa_prompt(y_main, cmp_p, gates, bsz, t):
    assert t % SEL_KEY_TILE == 0 and t >= WINDOW + Q_BLOCK and SEL_KEY_TILE % PAGE_SIZE == 0
    np_tok = bsz * t
    nq = t // Q_BLOCK
    n_cr = t // CMP_STRIDE
    n_c = (t - CMP_BLOCK) // CMP_STRIDE + 1
    n_s = t // SEL_BLOCK
    assert n_s >= N_SEL and n_s % 8 == 0
    ovt = _overlap_shifted(n_cr, n_c, n_s, n_s).T
    n_pg = t // PAGE_SIZE
    k_sel = _keys_bf16(y_main, np_tok, NQ_COLS + 2 * NKV_COLS)
    v_sel_t = _values_t_bf16(y_main, bsz, t, NQ_COLS + 3 * NKV_COLS)
    k_win = _keys_bf16(y_main, np_tok, NQ_COLS + 4 * NKV_COLS)
    v_win_t = _values_t_bf16(y_main, bsz, t, NQ_COLS + 5 * NKV_COLS)
    cmp_t = cmp_p.transpose(0, 2, 1)
    g_t = gates[:np_tok].reshape(bsz, nq, Q_BLOCK, N_KV_HEADS, GROUP, 3).transpose(0, 3, 1, 5, 4, 2)
    g_t = g_t.reshape(bsz, N_KV_HEADS, nq, 3, GROUP * Q_BLOCK)
    seq_key = pl.BlockSpec((t, HEAD_DIM), lambda b, k, i: (b, k))
    seq_val = pl.BlockSpec((None, None, n_pg, HEAD_DIM, PAGE_SIZE), lambda b, k, i: (b, k, 0, 0, 0))
    return pl.pallas_call(
        functools.partial(_nsa_prompt_kernel, t=t),
        grid=(bsz, N_KV_HEADS, nq),
        in_specs=[pl.BlockSpec((Q_BLOCK, QW), lambda b, k, i: (b * nq + i, k)),
                  pl.BlockSpec((None, n_cr, HEAD_DIM), lambda b, k, i: (b, 0, k)),
                  pl.BlockSpec((None, HEAD_DIM, n_cr), lambda b, k, i: (b, N_KV_HEADS + k, 0)),
                  seq_key, seq_val, seq_key, seq_val,
                  pl.BlockSpec((None, None, None, 3, GROUP * Q_BLOCK), lambda b, k, i: (b, k, i, 0, 0)),
                  pl.BlockSpec((n_s, n_cr), lambda b, k, i: (0, 0))],
        out_specs=pl.BlockSpec((Q_BLOCK, QW), lambda b, k, i: (b * nq + i, k)),
        out_shape=jax.ShapeDtypeStruct((np_tok, NQ_COLS), BF16),
        scratch_shapes=[pltpu.VMEM((n_s, Q_BLOCK), F32)],
        compiler_params=_params("parallel", "parallel", "arbitrary"),
    )(y_main, cmp_p, cmp_t, k_sel, v_sel_t, k_win, v_win_t, g_t, ovt)


SAMPLE_ROWS = 8


def _nsa_sample_kernel(*refs, past, s_len, n_s, n_in):
    y_ref, cmp_ref = refs[1:3]
    pages = refs[3:3 + n_in]
    win_ref, g_ref, ov_ref, o_ref, q_scr, sel_scr, m_scr, l_scr, acc_scr, oc_scr = refs[3 + n_in:]
    pg = pl.program_id(1)
    n_pages = pl.num_programs(1)
    sr = SAMPLE_ROWS
    rk = GROUP * sr
    scale = HEAD_DIM ** -0.5
    tpos = lax.broadcasted_iota(jnp.int32, (rk, 1), 0) % sr
    qpos = past + tpos
    n_sc = sel_scr.shape[1]
    nbp = past // SEL_BLOCK
    kv0 = NQ_COLS

    @pl.when(pg == 0)
    def _():
        y = y_ref[...]
        n_cr = cmp_ref.shape[0]
        m_idx = lax.broadcasted_iota(jnp.int32, (1, n_cr), 1)
        cmask = (m_idx >= 1) & (m_idx * CMP_STRIDE + (CMP_BLOCK - CMP_STRIDE - 1) <= qpos)
        for k in range(N_KV_HEADS):
            qk = jnp.concatenate([y[:, (k * GROUP + g) * HEAD_DIM:(k * GROUP + g + 1) * HEAD_DIM]
                                  for g in range(GROUP)], axis=0) * scale
            qk = qk.astype(BF16)
            q_scr[k] = qk
            kc = cmp_ref[:, k * HEAD_DIM:(k + 1) * HEAD_DIM].astype(BF16)
            vc = cmp_ref[:, (N_KV_HEADS + k) * HEAD_DIM:(N_KV_HEADS + k + 1) * HEAD_DIM].astype(BF16)
            p_c = _masked_softmax(_dot_nt(qk, kc), cmask)
            oc_scr[k * rk:(k + 1) * rk, :] = _dot(p_c.astype(BF16), vc)
            psum = p_c[0:sr]
            for g in range(1, GROUP):
                psum = psum + p_c[g * sr:(g + 1) * sr]
            imp = _dot(psum, ov_ref[...], HI)
            sel = _select_blocks(imp, qpos[0:sr], n_s, n_sc)
            for g in range(GROUP):
                sel_scr[k * rk + g * sr:k * rk + (g + 1) * sr, :] = sel
        m_scr[...] = jnp.full(m_scr.shape, NEG_INF, F32)
        l_scr[...] = jnp.zeros(l_scr.shape, F32)
        acc_scr[...] = jnp.zeros(acc_scr.shape, F32)

    def online_update(k, sc, mask, v):
        rows = slice(k * rk, (k + 1) * rk)
        sc = jnp.where(mask, sc, NEG_INF)
        m_old = m_scr[rows, :]
        m_new = jnp.maximum(m_old, jnp.max(sc, axis=-1, keepdims=True))
        alpha = jnp.exp(m_old - m_new)
        p = jnp.where(mask, jnp.exp(sc - m_new), 0.0)
        l_scr[rows, :] = alpha * l_scr[rows, :] + jnp.sum(p, axis=-1, keepdims=True)
        acc_scr[rows, :] = alpha * acc_scr[rows, :] + _dot(p.astype(v.dtype), v)
        m_scr[rows, :] = m_new

    n_keys = n_in * PAGE_SIZE
    kpos = pg * n_keys + lax.broadcasted_iota(jnp.int32, (1, n_keys), 1)
    blk_row = lax.broadcasted_iota(jnp.int32, (n_sc, 1), 0)
    expand = jnp.where(blk_row == kpos // SEL_BLOCK, 1.0, 0.0).astype(BF16)
    selx = _dot(sel_scr[...].astype(BF16), expand)
    by_slot = [jnp.swapaxes(pages[i][...], 0, 1).astype(BF16) for i in range(n_in)]
    for k in range(N_KV_HEADS):
        kp = jnp.concatenate([by_slot[i][k] for i in range(n_in)], axis=0)
        vp = jnp.concatenate([by_slot[i][N_KV_HEADS + k] for i in range(n_in)], axis=0)
        mask = (selx[k * rk:(k + 1) * rk] > 0.5) & (kpos <= qpos)
        online_update(k, _dot_nt(q_scr[k], kp), mask, vp)

    @pl.when(pg == n_pages - 1)
    def _():
        y = y_ref[...]
        rpos = lax.broadcasted_iota(jnp.int32, (1, sr), 1)
        new_ok = (rpos < s_len) & (rpos <= tpos)
        gt = g_ref[...]
        win_by_slot = jnp.swapaxes(win_ref[...], 0, 1).astype(BF16)
        for k in range(N_KV_HEADS):
            rows = slice(k * rk, (k + 1) * rk)
            qk = q_scr[k]
            qf = qk.astype(F32)
            c_ks = kv0 + (2 * N_KV_HEADS + k) * HEAD_DIM
            c_vs = kv0 + (3 * N_KV_HEADS + k) * HEAD_DIM
            kn = y[:, c_ks:c_ks + HEAD_DIM]
            vn = y[:, c_vs:c_vs + HEAD_DIM]
            new_sel = sel_scr[rows, nbp:nbp + 1] > 0.5
            online_update(k, _dot_nt(qf, kn), new_sel & new_ok, vn)
            o_s = acc_scr[rows, :] / jnp.maximum(l_scr[rows, :], TINY)
            wb = win_ref.shape[0]
            c_kw = kv0 + (4 * N_KV_HEADS + k) * HEAD_DIM
            c_vw = kv0 + (5 * N_KV_HEADS + k) * HEAD_DIM
            kwb = win_by_slot[k]
            vwb = win_by_slot[N_KV_HEADS + k]
            kwn = y[:, c_kw:c_kw + HEAD_DIM]
            vwn = y[:, c_vw:c_vw + HEAD_DIM]
            dpos = qpos - (past - wb + lax.broadcasted_iota(jnp.int32, (1, wb), 1))
            mask_b = (dpos >= 0) & (dpos < WINDOW)
            s_b = jnp.where(mask_b, _dot_nt(qk, kwb), NEG_INF)
            s_n = jnp.where(new_ok, _dot_nt(qf, kwn), NEG_INF)
            mw = jnp.maximum(jnp.max(s_b, axis=-1, keepdims=True), jnp.max(s_n, axis=-1, keepdims=True))
            p_b = jnp.where(mask_b, jnp.exp(s_b - mw), 0.0)
            p_n = jnp.where(new_ok, jnp.exp(s_n - mw), 0.0)
            den = jnp.sum(p_b, axis=-1, keepdims=True) + jnp.sum(p_n, axis=-1, keepdims=True)
            o_w = (_dot(p_b.astype(BF16), vwb) + _dot(p_n, vwn)) / jnp.maximum(den, TINY)
            o_c = oc_scr[rows, :]
            for g in range(GROUP):
                r0, r1 = g * sr, (g + 1) * sr
                c = (k * GROUP + g) * 3
                o = gt[:, c:c + 1] * o_c[r0:r1] + gt[:, c + 1:c + 2] * o_s[r0:r1] + gt[:, c + 2:c + 3] * o_w[r0:r1]
                o_ref[:, (k * GROUP + g) * HEAD_DIM:(k * GROUP + g + 1) * HEAD_DIM] = o


SAMPLE_PAGES_PER_STEP = 8


def _nsa_sample(ys8, cmp_s, cache5, layer, win4, gates8, page_table, s_len):
    db, n_pages = page_table.shape
    past = n_pages * PAGE_SIZE
    n_in = math.gcd(n_pages, SAMPLE_PAGES_PER_STEP)
    assert s_len <= SAMPLE_ROWS and s_len <= SEL_BLOCK and past % SEL_BLOCK == 0
    n_cr = cmp_s.shape[1]
    n_c = (past + s_len - CMP_BLOCK) // CMP_STRIDE + 1
    assert n_c + 1 <= n_cr
    n_s = -(-(past + s_len) // SEL_BLOCK)
    n_sc = -(-n_s // 128) * 128
    ov = _overlap_shifted(n_cr, n_c, n_s, n_sc)
    wb = win4.shape[1]
    ncol = ys8.shape[2]
    rows = N_KV_HEADS * GROUP * SAMPLE_ROWS
    page_specs = [pl.BlockSpec((None, None, PAGE_SIZE, 2 * N_KV_HEADS, HEAD_DIM),
                               functools.partial(lambda b, p, pt, i: (layer, pt[b, p * n_in + i], 0, 1, 0), i=i))
                  for i in range(n_in)]
    grid_spec = pltpu.PrefetchScalarGridSpec(
        num_scalar_prefetch=1,
        grid=(db, n_pages // n_in),
        in_specs=[pl.BlockSpec((None, SAMPLE_ROWS, ncol), lambda b, p, pt: (b, 0, 0)),
                  pl.BlockSpec((None, n_cr, 2 * NKV_COLS), lambda b, p, pt: (b, 0, 0))]
                 + page_specs
                 + [pl.BlockSpec((None, wb, 2 * N_KV_HEADS, HEAD_DIM), lambda b, p, pt: (b, 0, 0, 0)),
                    pl.BlockSpec((None, SAMPLE_ROWS, 3 * N_HEADS), lambda b, p, pt: (b, 0, 0)),
                    pl.BlockSpec((n_cr, n_sc), lambda b, p, pt: (0, 0))],
        out_specs=pl.BlockSpec((None, SAMPLE_ROWS, NQ_COLS), lambda b, p, pt: (b, 0, 0)),
        scratch_shapes=[pltpu.VMEM((N_KV_HEADS, GROUP * SAMPLE_ROWS, HEAD_DIM), BF16),
                        pltpu.VMEM((rows, n_sc), F32),
                        pltpu.VMEM((rows, 1), F32),
                        pltpu.VMEM((rows, 1), F32),
                        pltpu.VMEM((rows, HEAD_DIM), F32),
                        pltpu.VMEM((rows, HEAD_DIM), F32)],
    )
    return pl.pallas_call(
        functools.partial(_nsa_sample_kernel, past=past, s_len=s_len, n_s=n_s, n_in=n_in),
        grid_spec=grid_spec,
        out_shape=jax.ShapeDtypeStruct((db, SAMPLE_ROWS, NQ_COLS), F32),
        compiler_params=_params("parallel", "arbitrary"),
    )(page_table, ys8, cmp_s, *([cache5] * n_in), win4, gates8, ov)


WKV_DIAG_BLOCK = 16
WKV_CHUNK = 64


def _bdot(a, b):
    return _dot(a.astype(BF16), b.astype(BF16))


def _bdot_nt(a, b):
    return _dot_nt(a.astype(BF16), b.astype(BF16))


def _bdot_tn(a, b):
    return _dot_tn(a.astype(BF16), b.astype(BF16))


def _unit_lower_inverses(mats, c):
    blk = min(WKV_DIAG_BLOCK, c)
    ri = lax.broadcasted_iota(jnp.int32, (c, c), 0)
    ci = lax.broadcasted_iota(jnp.int32, (c, c), 1)
    eye = jnp.where(ri == ci, 1.0, 0.0)
    same = (ri // blk) == (ci // blk)
    dp = [jnp.where(same, a, 0.0) for a in mats]
    td = [eye + d for d in dp]
    for _ in range(int(math.log2(blk)) - 1):
        dp = [_bdot(d, d) for d in dp]
        td = [t + _bdot(d, t) for d, t in zip(dp, td)]
    if blk == c:
        return td
    npow = [_bdot(t, jnp.where(same, 0.0, a)) for t, a in zip(td, mats)]
    tinv = [t + _bdot(x, t) for x, t in zip(npow, td)]
    for _ in range(int(math.log2(c // blk)) - 1):
        npow = [_bdot(x, x) for x in npow]
        tinv = [t + _bdot(x, t) for x, t in zip(npow, tinv)]
    return tinv


def _wkv_kernel(r_ref, k_ref, v_ref, lw_ref, a_ref, g_ref, kk_ref, ka_ref, rk_ref, lg_ref, lb_ref, s0_ref,
                z_ref, s_ref, *, c, nc, hb):
    n = RWKV_HEAD

    @pl.when(pl.program_id(2) == 0)
    def _():
        s_ref[...] = s0_ref[...]

    ri = lax.broadcasted_iota(jnp.int32, (c, c), 0)
    ci = lax.broadcasted_iota(jnp.int32, (c, c), 1)
    strict = ri > ci
    incl = ri >= ci
    tri = jnp.where(incl, 1.0, 0.0)
    lw_all = lw_ref[...]
    cum = jnp.concatenate([_dot(tri, lw_all[i * c:(i + 1) * c], HI) for i in range(nc)], axis=0)
    e_in_all = jnp.exp(cum)
    e_neg_all = jnp.exp(-cum)
    e_ex_all = jnp.exp(cum - lw_all)
    heads = range(hb)
    chunks = range(nc)
    sls = [slice(hh * n, (hh + 1) * n) for hh in heads]
    rws = [slice(i * c, (i + 1) * c) for i in chunks]
    r, v, k2, r_t, a_t, b_t, k_t, vb = ([] for _ in range(8))
    for sl in sls:
        k = k_ref[:, sl]
        a = a_ref[:, sl]
        kk = k * kk_ref[:, sl]
        kk = kk * (1.0 / jnp.maximum(jnp.sqrt(jnp.sum(kk * kk, axis=-1, keepdims=True)), 1e-12))
        r.append(r_ref[:, sl])
        v.append(v_ref[:, sl])
        k2.append(k * (1.0 + (a - 1.0) * ka_ref[:, sl]))
        r_t.append(r[-1] * e_in_all[:, sl])
        a_t.append(-kk * e_ex_all[:, sl])
        b_t.append(kk * a * e_neg_all[:, sl])
        k_t.append(k2[-1] * e_neg_all[:, sl])
        vb.append(v[-1].astype(BF16))

    items = [(hh, i) for hh in heads for i in chunks]
    aa = [_bdot_nt(jnp.concatenate([a_t[hh][rws[i]], r_t[hh][rws[i]]], axis=0),
                   jnp.concatenate([b_t[hh][rws[i]], k_t[hh][rws[i]]], axis=0)) for hh, i in items]
    a_ab = [jnp.where(strict, x[:c, :c], 0.0) for x in aa]
    a_ak = [jnp.where(strict, x[:c, c:], 0.0) for x in aa]
    a_rb = [jnp.where(incl, x[c:, :c], 0.0).astype(BF16) for x in aa]
    a_rk = [jnp.where(incl, x[c:, c:], 0.0) for x in aa]
    akv = [_bdot(x, vb[hh][rws[i]]) for x, (hh, i) in zip(a_ak, items)]
    yv = [_bdot(x, vb[hh][rws[i]]) for x, (hh, i) in zip(a_rk, items)]
    w_c = [e_in_all[i * c + c - 1:(i + 1) * c, sls[hh]] for hh, i in items]
    kv = [_bdot_tn(vb[hh][rws[i]], k_t[hh][rws[i]] * w) for w, (hh, i) in zip(w_c, items)]
    bh = [(b_t[hh][rws[i]] * w).astype(BF16) for w, (hh, i) in zip(w_c, items)]
    tinv = _unit_lower_inverses(a_ab, c)
    wt = [_bdot(t, a_t[hh][rws[i]]) for t, (hh, i) in zip(tinv, items)]
    u = [_bdot(t, x) for t, x in zip(tinv, akv)]
    wr = [jnp.concatenate([w, r_t[hh][rws[i]]], axis=0).astype(BF16) for w, (hh, i) in zip(wt, items)]

    s = [s_ref[hh] for hh in heads]
    ys = [[] for _ in heads]
    for i in chunks:
        idx = [hh * nc + i for hh in heads]
        ls = [_dot_nt(wr[j], s[hh].astype(BF16)) for hh, j in zip(heads, idx)]
        p = [x[:c] + u[j] for x, j in zip(ls, idx)]
        for hh, j in zip(heads, idx):
            ys[hh].append(ls[hh][c:] + _bdot(a_rb[j], p[hh]) + yv[j])
        s = [s[hh] * w_c[j] + _bdot_tn(p[hh], bh[j]) + kv[j] for hh, j in zip(heads, idx)]
    zs = []
    for hh, sl in zip(heads, sls):
        s_ref[hh] = s[hh]
        y = ys[hh][0] if nc == 1 else jnp.concatenate(ys[hh], axis=0)
        mu = jnp.mean(y, axis=-1, keepdims=True)
        yc = y - mu
        yn = yc * lax.rsqrt(jnp.mean(yc * yc, axis=-1, keepdims=True) + LNX_EPS)
        yn = yn * lg_ref[:, sl] + lb_ref[:, sl]
        bonus = jnp.sum(r[hh] * k2[hh] * rk_ref[:, sl], axis=-1, keepdims=True) * v[hh]
        zs.append((yn + bonus) * g_ref[:, sl])
    z_ref[...] = jnp.concatenate(zs, axis=1).astype(z_ref.dtype)


def _wkv(rkv, lw, a, g, k_k, k_a, r_k, lnx_g, lnx_b, s0, n_seq, t, c, nc, hb):
    rows, d = lw.shape
    h = d // RWKV_HEAD
    hw = hb * RWKV_HEAD
    tb = nc * c
    nblk = t // tb
    tok = lambda s, hg, ch: (s * nblk + ch, hg)
    par = lambda s, hg, ch: (0, hg)
    st = lambda s, hg, ch: (s, hg, 0, 0)
    rkv_spec = lambda p: pl.BlockSpec((None, tb, hw), lambda s, hg, ch: (p, s * nblk + ch, hg))
    return pl.pallas_call(
        functools.partial(_wkv_kernel, c=c, nc=nc, hb=hb),
        grid=(n_seq, h // hb, nblk),
        in_specs=[rkv_spec(0), rkv_spec(1), rkv_spec(2),
                  pl.BlockSpec((tb, hw), tok), pl.BlockSpec((tb, hw), tok), pl.BlockSpec((tb, hw), tok),
                  pl.BlockSpec((1, hw), par), pl.BlockSpec((1, hw), par), pl.BlockSpec((1, hw), par),
                  pl.BlockSpec((1, hw), par), pl.BlockSpec((1, hw), par),
                  pl.BlockSpec((None, hb, RWKV_HEAD, RWKV_HEAD), st)],
        out_specs=[pl.BlockSpec((tb, hw), tok),
                   pl.BlockSpec((None, hb, RWKV_HEAD, RWKV_HEAD), st)],
        out_shape=[jax.ShapeDtypeStruct((n_seq * t, d), BF16),
                   jax.ShapeDtypeStruct((n_seq, h, RWKV_HEAD, RWKV_HEAD), F32)],
        compiler_params=_params("parallel", "parallel", "arbitrary"),
    )(rkv, rkv, rkv, lw, a, g, k_k, k_a, r_k, lnx_g, lnx_b, s0)


WKV_PROMPT_CHUNKS_PER_STEP = 2
WKV_PROMPT_HEADS_PER_STEP = 8


def _nsa_layer(x, np_tok, bsz, t, db, s_len, cache5, layer, win, page_table, w_in, pe, w1, w2, w_out,
               ln_g, ln_b, alpha):
    n_main = NQ_COLS + 6 * NKV_COLS
    y_main = _mm(x, w_in, layer, n_main, tn_pref=1024)
    w_gate = jnp.pad(w_in[layer:layer + 1, :, n_main:], ((0, 0), (0, 0), (0, 128 - 3 * N_HEADS)))
    gates = _mm(x, w_gate, 0, 128, act="sigmoid")[:, :3 * N_HEADS]

    wc, w2b = _cmp_weights(w1, w2)
    bias = _cmp_bias(pe, w1)
    cmp_p = _cmp_prompt(y_main, bsz, t, wc, bias, w2b)
    cmp_s = _cmp_sample(cache5, layer, page_table, wc, bias, w2b)

    o_p = _nsa_prompt(y_main, cmp_p, gates, bsz, t)

    pad_rows = ((0, 0), (0, SAMPLE_ROWS - s_len), (0, 0))
    ys = y_main[np_tok:].reshape(db, s_len, n_main)
    ys8 = jnp.pad(ys, pad_rows)
    gates8 = jnp.pad(gates[np_tok:].reshape(db, s_len, 3 * N_HEADS), pad_rows)
    wb = win.shape[1]
    win4 = win.reshape(db, wb, 2 * N_KV_HEADS, HEAD_DIM)
    o_s = _nsa_sample(ys8, cmp_s, cache5, layer, win4, gates8, page_table, s_len)
    o_s = o_s[:, :s_len].reshape(db * s_len, NQ_COLS).astype(BF16)

    o = jnp.concatenate([o_p, o_s], axis=0)
    h = _mm_ln(o, w_out, layer, x, ln_g, ln_b, alpha)

    kv0 = NQ_COLS
    kw0 = NQ_COLS + 4 * NKV_COLS
    kv_p = lax.slice(y_main, (0, kv0), (np_tok, kw0)).reshape(bsz, t, 4, N_KV_HEADS, HEAD_DIM)
    wlen = min(WINDOW, t)
    win_p = jnp.stack([lax.slice(y_main, (b * t + t - wlen, kw0), ((b + 1) * t, n_main)) for b in range(bsz)])
    win_p = win_p.reshape(bsz, wlen, 2, N_KV_HEADS, HEAD_DIM)
    kv_s = ys[:, :, kv0:kw0].reshape(db, s_len, 4, N_KV_HEADS, HEAD_DIM)
    kvw_s = ys[:, :, kw0:].reshape(db, s_len, 2, N_KV_HEADS, HEAD_DIM)
    win_s = jnp.concatenate([win, kvw_s], axis=1)[:, s_len:]
    return h, kv_p, kv_s, win_p, win_s


def _rwkv_layer(x, np_tok, bsz, t, db, s_len, shift_s, wkv_s, mu, w_rkv, w0, w1, w2, a0, a1, a2, g1, g2,
                k_k, k_a, r_k, lnx_g, lnx_b, w_out, layer, ln_g, ln_b, alpha):
    d = x.shape[1]
    heads = d // RWKV_HEAD
    xprev = jnp.concatenate([jnp.zeros((1, d), x.dtype), x[:-1]], axis=0)
    xprev = xprev.at[np.arange(1, bsz) * t].set(0.0)
    xprev = xprev.at[np_tok + np.arange(db) * s_len].set(shift_s.astype(x.dtype))

    row = lambda v: v.reshape(1, -1)
    rkv = _mm_mix(x, xprev, mu[:3, None, :], w_rkv)
    lw = _lora(x, xprev, row(mu[3]), w1, w2, row(w0), "decay")
    a = _lora(x, xprev, row(mu[4]), a1, a2, row(a0), "aaa")
    g = _lora(x, xprev, row(mu[5]), g1, g2, jnp.zeros((1, d), F32), "gate")

    pvec = (row(k_k), row(k_a), row(r_k), row(lnx_g), row(lnx_b))
    c_p = _tile(t, WKV_CHUNK, 8)
    nc_p = math.gcd(t // c_p, WKV_PROMPT_CHUNKS_PER_STEP)
    hb_p = math.gcd(heads, WKV_PROMPT_HEADS_PER_STEP)
    zero_state = jnp.zeros((bsz, heads, RWKV_HEAD, RWKV_HEAD), F32)
    z_p, st_p = _wkv(rkv, lw, a, g, *pvec, zero_state, bsz, t, c_p, nc_p, hb_p)

    c_s = -(-s_len // 8) * 8
    pad = lambda v: jnp.pad(v.reshape(v.shape[:-2] + (db, s_len, d)),
                            [(0, 0)] * (v.ndim - 1) + [(0, c_s - s_len), (0, 0)]
                            ).reshape(v.shape[:-2] + (db * c_s, d))
    z_s, st_s = _wkv(pad(rkv[:, np_tok:]), pad(lw[np_tok:]), pad(a[np_tok:]), pad(g[np_tok:]), *pvec,
                     wkv_s.astype(F32), db, c_s, c_s, 1, heads)
    z_s = z_s.reshape(db, c_s, d)[:, :s_len].reshape(db * s_len, d)

    z = jnp.concatenate([z_p, z_s], axis=0)
    h = _mm_ln(z, w_out, layer, x, ln_g, ln_b, alpha)
    last_p = x[np.arange(1, bsz + 1) * t - 1]
    last_s = x[np_tok + np.arange(1, db + 1) * s_len - 1]
    return h, st_p, st_s, last_p, last_s


def _mlp(h, w_up, w_down, layer, ln_g, ln_b, alpha):
    d_ff = w_up.shape[2]
    u = _mm(h, w_up, layer, d_ff, act="relu2", out_dtype=BF16)
    return _mm_ln(u, w_down, layer, h, ln_g, ln_b, alpha)


def kernel(x_prompt, x_sample, cache_nsa_kv, state_nsa_win, state_rwkv_wkv, state_rwkv_shift, page_table,
           nsa_w_in, nsa_cmp_pe, nsa_cmp_w1, nsa_cmp_w2, nsa_w_out,
           rwkv_mu, rwkv_w_rkv, rwkv_w0, rwkv_w1, rwkv_w2, rwkv_a0, rwkv_a1, rwkv_a2, rwkv_g1, rwkv_g2,
           rwkv_k_k, rwkv_k_a, rwkv_r_k, rwkv_lnx_g, rwkv_lnx_b, rwkv_w_out,
           ffn_w_up, ffn_w_down, ln_g, ln_b):
    bsz, t, d = x_prompt.shape
    db, s_len, _ = x_sample.shape
    depth = ffn_w_up.shape[0]
    alpha = (2 * depth) ** 0.25
    np_tok = bsz * t
    x = jnp.concatenate([x_prompt.reshape(np_tok, d), x_sample.reshape(db * s_len, d)], axis=0)
    n_l, n_pool = cache_nsa_kv.shape[:2]
    cache5 = cache_nsa_kv.reshape(n_l, n_pool, PAGE_SIZE, 4 * N_KV_HEADS, HEAD_DIM)
    page_table = page_table.astype(jnp.int32)

    kv_p, kv_s, win_p, win_s, wkv_p, wkv_s, sh_p, sh_s = ([] for _ in range(8))
    row = lambda v: v.reshape(1, -1)
    for i in range(depth):
        j = i // 2
        if i % 2 == 0:
            h, kvp_new, kvs_new, wp_new, ws_new = _nsa_layer(
                x, np_tok, bsz, t, db, s_len, cache5, j, state_nsa_win[j], page_table,
                nsa_w_in, nsa_cmp_pe[j], nsa_cmp_w1[j], nsa_cmp_w2[j], nsa_w_out,
                row(ln_g[i, 0]), row(ln_b[i, 0]), alpha)
            kv_p.append(kvp_new)
            kv_s.append(kvs_new)
            win_p.append(wp_new)
            win_s.append(ws_new)
        else:
            h, sp_new, ss_new, hp_new, hs_new = _rwkv_layer(
                x, np_tok, bsz, t, db, s_len, state_rwkv_shift[j], state_rwkv_wkv[j],
                rwkv_mu[j], rwkv_w_rkv[j], rwkv_w0[j], rwkv_w1[j], rwkv_w2[j], rwkv_a0[j], rwkv_a1[j],
                rwkv_a2[j], rwkv_g1[j], rwkv_g2[j], rwkv_k_k[j], rwkv_k_a[j], rwkv_r_k[j],
                rwkv_lnx_g[j], rwkv_lnx_b[j], rwkv_w_out, j, row(ln_g[i, 0]), row(ln_b[i, 0]), alpha)
            wkv_p.append(sp_new)
            wkv_s.append(ss_new)
            sh_p.append(hp_new)
            sh_s.append(hs_new)
        x = _mlp(h, ffn_w_up, ffn_w_down, i, row(ln_g[i, 1]), row(ln_b[i, 1]), alpha)
    return (x[:np_tok].reshape(bsz, t, d), x[np_tok:].reshape(db, s_len, d),
            jnp.stack(kv_p), jnp.stack(kv_s), jnp.stack(win_p), jnp.stack(win_s),
            jnp.stack(wkv_p), jnp.stack(wkv_s), jnp.stack(sh_p), jnp.stack(sh_s))
```

```python
import functools
import math

import jax
import jax.numpy as jnp
import numpy as np
from jax import lax
from jax.experimental import pallas as pl
from jax.experimental.pallas import tpu as pltpu

F32 = jnp.float32
BF16 = jnp.bfloat16

HEAD_DIM = 128
N_KV_HEADS = 4
GROUP = 4
N_HEADS = N_KV_HEADS * GROUP
QW = GROUP * HEAD_DIM
NQ_COLS = N_HEADS * HEAD_DIM
NKV_COLS = N_KV_HEADS * HEAD_DIM
CMP_BLOCK = 32
CMP_STRIDE = 16
SEL_BLOCK = 64
N_SEL = 16
WINDOW = 512
Q_BLOCK = 128
PAGE_SIZE = 128
RWKV_HEAD = 64
LN_EPS = 1e-5
LNX_EPS = 64e-5
FORCE = 1e4
NEG_INF = -1e30
TINY = 1e-30
CHUNKS_PER_PAGE = PAGE_SIZE // CMP_STRIDE

VMEM_LIMIT_BYTES = 56 * 1024 * 1024
HI = lax.Precision.HIGHEST


def _params(*sem):
    return pltpu.CompilerParams(dimension_semantics=sem, vmem_limit_bytes=VMEM_LIMIT_BYTES)


def _tile(n, pref, mult=16):
    if n <= pref:
        return n
    for t in range(pref - pref % mult, 0, -mult):
        if n % t == 0:
            return t
    return n


def _dot(a, b, precision=None):
    return jnp.dot(a, b, preferred_element_type=F32, precision=precision)


def _dot_nt(a, b, precision=None):
    return lax.dot_general(a, b, (((1,), (1,)), ((), ())), preferred_element_type=F32, precision=precision)


def _dot_tn(a, b, precision=None):
    return lax.dot_general(a, b, (((0,), (0,)), ((), ())), preferred_element_type=F32, precision=precision)


def _masked_softmax(s, mask):
    s = jnp.where(mask, s, NEG_INF)
    m = jnp.max(s, axis=-1, keepdims=True)
    p = jnp.where(mask, jnp.exp(s - m), 0.0)
    return p * (1.0 / jnp.maximum(jnp.sum(p, axis=-1, keepdims=True), TINY))


def _layer_norm_rows(s, g, b, eps):
    mu = jnp.mean(s, axis=-1, keepdims=True)
    c = s - mu
    var = jnp.mean(c * c, axis=-1, keepdims=True)
    return c * lax.rsqrt(var + eps) * g + b


def _mm_kernel(x_ref, w_ref, *refs, act):
    out_refs, xb_ref = refs[:-1], refs[-1]

    @pl.when(pl.program_id(1) == 0)
    def _():
        xb_ref[...] = x_ref[...].astype(BF16)

    y = _dot(xb_ref[...], w_ref[...].astype(BF16))
    if act == "relu2":
        y = jnp.maximum(y, 0.0)
        y = y * y
    elif act == "sigmoid":
        y = jax.nn.sigmoid(y)
    for o_ref in out_refs:
        o_ref[...] = y.astype(o_ref.dtype)


def _mm(x, w, layer, n_out, *, act=None, out_dtypes=(F32,), tm_pref=1040, tn_pref=1024):
    m, k = x.shape
    tm = _tile(m, tm_pref)
    tn = _tile(n_out, tn_pref, 128)
    outs = pl.pallas_call(
        functools.partial(_mm_kernel, act=act),
        grid=(m // tm, n_out // tn),
        in_specs=[pl.BlockSpec((tm, k), lambda i, j: (i, 0)),
                  pl.BlockSpec((None, k, tn), lambda i, j: (layer, 0, j))],
        out_specs=[pl.BlockSpec((tm, tn), lambda i, j: (i, j)) for _ in out_dtypes],
        out_shape=[jax.ShapeDtypeStruct((m, n_out), dt) for dt in out_dtypes],
        scratch_shapes=[pltpu.VMEM((tm, k), BF16)],
        compiler_params=_params("parallel", "arbitrary"),
    )(x, w)
    return outs[0] if len(outs) == 1 else outs


def _mm_mix_kernel(x_ref, xp_ref, mu_ref, w_ref, o_ref, xb_ref):
    @pl.when(pl.program_id(2) == 0)
    def _():
        x = x_ref[...]
        xb_ref[...] = (x + (xp_ref[...] - x) * mu_ref[...]).astype(BF16)

    o_ref[...] = _dot(xb_ref[...], w_ref[...].astype(BF16))


def _mm_mix(x, xp, mu, w, *, tm_pref=832, tn_pref=512):
    m, k = x.shape
    npar, _, n = w.shape
    tm = _tile(m, tm_pref)
    tn = _tile(n, tn_pref, 128)
    return pl.pallas_call(
        _mm_mix_kernel,
        grid=(m // tm, npar, n // tn),
        in_specs=[pl.BlockSpec((tm, k), lambda i, p, j: (i, 0)),
                  pl.BlockSpec((tm, k), lambda i, p, j: (i, 0)),
                  pl.BlockSpec((None, 1, k), lambda i, p, j: (p, 0, 0)),
                  pl.BlockSpec((None, k, tn), lambda i, p, j: (p, 0, j))],
        out_specs=pl.BlockSpec((None, tm, tn), lambda i, p, j: (p, i, j)),
        out_shape=jax.ShapeDtypeStruct((npar, m, n), F32),
        scratch_shapes=[pltpu.VMEM((tm, k), BF16)],
        compiler_params=_params("parallel", "arbitrary", "arbitrary"),
    )(x, xp, mu, w)


def _lora_kernel(x_ref, xp_ref, mu_ref, a_ref, b_ref, bias_ref, o_ref, *, kind):
    x = x_ref[...]
    xm = (x + (xp_ref[...] - x) * mu_ref[...]).astype(BF16)
    h = _dot(xm, a_ref[...].astype(BF16))
    if kind == "decay":
        h = jnp.tanh(h)
    elif kind == "gate":
        h = jax.nn.sigmoid(h)
    z = _dot(h.astype(BF16), b_ref[...].astype(BF16))
    if kind == "decay":
        u = -(z + bias_ref[...])
        softplus = jnp.maximum(u, 0.0) + jnp.log(1.0 + jnp.exp(-jnp.abs(u)))
        z = -jnp.exp(-softplus - 0.5)
    elif kind == "aaa":
        z = jax.nn.sigmoid(z + bias_ref[...])
    o_ref[...] = z


def _lora(x, xp, mu, a, b, bias, kind, *, tm_pref=320):
    m, k = x.shape
    r = a.shape[1]
    n = b.shape[1]
    tm = _tile(m, tm_pref)
    row = lambda i: (i, 0)
    fixed = lambda i: (0, 0)
    return pl.pallas_call(
        functools.partial(_lora_kernel, kind=kind),
        grid=(m // tm,),
        in_specs=[pl.BlockSpec((tm, k), row), pl.BlockSpec((tm, k), row), pl.BlockSpec((1, k), fixed),
                  pl.BlockSpec((k, r), fixed), pl.BlockSpec((r, n), fixed), pl.BlockSpec((1, n), fixed)],
        out_specs=pl.BlockSpec((tm, n), row),
        out_shape=jax.ShapeDtypeStruct((m, n), F32),
        compiler_params=_params("parallel"),
    )(x, xp, mu, a, b, bias)


def _mm_ln_kernel(z_ref, w_ref, res_ref, g_ref, b_ref, o_ref, *, alpha):
    kk = pl.program_id(1)
    part = _dot(z_ref[...].astype(BF16), w_ref[...].astype(BF16))

    @pl.when(kk == 0)
    def _():
        o_ref[...] = part

    @pl.when(kk > 0)
    def _():
        o_ref[...] += part

    @pl.when(kk == pl.num_programs(1) - 1)
    def _():
        s = alpha * res_ref[...] + o_ref[...]
        o_ref[...] = _layer_norm_rows(s, g_ref[...], b_ref[...], LN_EPS)


def _mm_ln(z, w, layer, res, g, b, alpha, *, tm_pref=640, tk_pref=1024):
    m, k = z.shape
    n = w.shape[2]
    tm = _tile(m, tm_pref)
    tk = _tile(k, tk_pref, 128)
    return pl.pallas_call(
        functools.partial(_mm_ln_kernel, alpha=alpha),
        grid=(m // tm, k // tk),
        in_specs=[pl.BlockSpec((tm, tk), lambda i, kk: (i, kk)),
                  pl.BlockSpec((None, tk, n), lambda i, kk: (layer, kk, 0)),
                  pl.BlockSpec((tm, n), lambda i, kk: (i, 0)),
                  pl.BlockSpec((1, n), lambda i, kk: (0, 0)),
                  pl.BlockSpec((1, n), lambda i, kk: (0, 0))],
        out_specs=pl.BlockSpec((tm, n), lambda i, kk: (i, 0)),
        out_shape=jax.ShapeDtypeStruct((m, n), F32),
        compiler_params=_params("parallel", "arbitrary"),
    )(z, w, res, g, b)


def _cmp_bias_kernel(pe_ref, w1_ref, o_ref):
    acc = jnp.zeros((1, HEAD_DIM), F32)
    for rj in range(CMP_BLOCK):
        acc = acc + _dot(pe_ref[rj:rj + 1, :], w1_ref[rj], HI)
    o_ref[...] = acc


def _cmp_bias(pe, w1):
    return pl.pallas_call(
        _cmp_bias_kernel,
        grid=(2,),
        in_specs=[pl.BlockSpec((None, CMP_BLOCK, HEAD_DIM), lambda p: (p, 0, 0)),
                  pl.BlockSpec((None, CMP_BLOCK, HEAD_DIM, HEAD_DIM), lambda p: (p, 0, 0, 0))],
        out_specs=pl.BlockSpec((None, 1, HEAD_DIM), lambda p: (p, 0, 0)),
        out_shape=jax.ShapeDtypeStruct((2, 1, HEAD_DIM), F32),
        compiler_params=_params("parallel"),
    )(pe, w1)


def _cmp_kernel(*refs, n_in):
    pages = refs[-5 - n_in:-5]
    wc_ref, bias_ref, w2_ref, out_ref, carry_ref = refs[-5:]
    mk = CHUNKS_PER_PAGE * n_in
    rows = N_KV_HEADS * mk

    @pl.when(pl.program_id(1) == 0)
    def _():
        carry_ref[...] = jnp.zeros_like(carry_ref)

    first = (lax.broadcasted_iota(jnp.int32, (rows, 1), 0) % mk) == 0
    if len(pages[0].shape) == 3:
        split = [pages[i][...].reshape(CHUNKS_PER_PAGE, CMP_STRIDE, 2 * N_KV_HEADS, HEAD_DIM)
                 for i in range(n_in)]
        by_slot = [[jnp.swapaxes(split[i][:, j], 0, 1) for j in range(CMP_STRIDE)] for i in range(n_in)]
        piece = lambda i, j, slot: by_slot[i][j][slot]
    else:
        r_out = lax.broadcasted_iota(jnp.int32, (PAGE_SIZE, PAGE_SIZE), 0)
        r_in = lax.broadcasted_iota(jnp.int32, (PAGE_SIZE, PAGE_SIZE), 1)
        perm = jnp.where(r_in == (r_out % CHUNKS_PER_PAGE) * CMP_STRIDE + r_out // CHUNKS_PER_PAGE, 1.0, 0.0)
        grouped = [_dot(perm.astype(BF16), pages[i][...].astype(BF16)) for i in range(n_in)]
        piece = lambda i, j, slot: grouped[i][j * CHUNKS_PER_PAGE:(j + 1) * CHUNKS_PER_PAGE,
                                              slot * HEAD_DIM:(slot + 1) * HEAD_DIM]
    for p in range(2):
        lhs = jnp.concatenate(
            [jnp.concatenate([piece(i, j, p * N_KV_HEADS + k).astype(BF16) for j in range(CMP_STRIDE)], axis=1)
             for k in range(N_KV_HEADS) for i in range(n_in)], axis=0)
        acc = _dot(lhs, wc_ref[p])
        part0 = acc[:, :HEAD_DIM]
        part1 = acc[:, HEAD_DIM:]
        prev = jnp.where(first, carry_ref[p], pltpu.roll(part0, 1, 0))
        carry_ref[p] = pltpu.roll(part0, rows - (mk - 1), 0)
        h = prev + part1 + bias_ref[p]
        c = _dot(jax.nn.gelu(h).astype(BF16), w2_ref[p])
        for k in range(N_KV_HEADS):
            col = (p * N_KV_HEADS + k) * HEAD_DIM
            out_ref[:, col:col + HEAD_DIM] = c[k * mk:(k + 1) * mk]


def _cmp_weights(w1, w2):
    n_r = CMP_BLOCK // CMP_STRIDE
    w1r = w1.reshape(2, n_r, CMP_STRIDE, HEAD_DIM, HEAD_DIM)
    wc = jnp.concatenate([w1r[:, r] for r in range(n_r)], axis=-1).astype(BF16)
    return wc.reshape(2, CMP_STRIDE * HEAD_DIM, 2 * HEAD_DIM), w2.astype(BF16)


def _cmp_call(n_seq, n_steps, n_in, page_specs, operands, wc, bias, w2b, num_prefetch, prefetch):
    mk = CHUNKS_PER_PAGE * n_in
    if num_prefetch:
        fixed3 = lambda s, t, pt: (0, 0, 0)
        out_map = lambda s, t, pt: (s, t, 0)
    else:
        fixed3 = lambda s, t: (0, 0, 0)
        out_map = lambda s, t: (s, t, 0)
    grid_spec = pltpu.PrefetchScalarGridSpec(
        num_scalar_prefetch=num_prefetch,
        grid=(n_seq, n_steps),
        in_specs=page_specs + [
            pl.BlockSpec((2, CMP_STRIDE * HEAD_DIM, 2 * HEAD_DIM), fixed3),
            pl.BlockSpec((2, 1, HEAD_DIM), fixed3),
            pl.BlockSpec((2, HEAD_DIM, HEAD_DIM), fixed3)],
        out_specs=pl.BlockSpec((None, mk, 2 * NKV_COLS), out_map),
        scratch_shapes=[pltpu.VMEM((2, N_KV_HEADS * mk, HEAD_DIM), F32)],
    )
    return pl.pallas_call(
        functools.partial(_cmp_kernel, n_in=n_in),
        grid_spec=grid_spec,
        out_shape=jax.ShapeDtypeStruct((n_seq, n_steps * mk, 2 * NKV_COLS), F32),
        compiler_params=_params("parallel", "arbitrary"),
    )(*prefetch, *operands, wc, bias, w2b)


CMP_PAGES_PER_STEP = 8


def _cmp_prompt(y_main, bsz, t, wc, bias, w2b):
    pages_per_seq = t // PAGE_SIZE
    n_in = math.gcd(pages_per_seq, CMP_PAGES_PER_STEP)
    n_steps = pages_per_seq // n_in
    col_blk = NQ_COLS // (2 * NKV_COLS)
    specs = [pl.BlockSpec((PAGE_SIZE, 2 * NKV_COLS),
                          functools.partial(lambda s, st, i: (s * pages_per_seq + st * n_in + i, col_blk), i=i))
             for i in range(n_in)]
    return _cmp_call(bsz, n_steps, n_in, specs, [y_main] * n_in, wc, bias, w2b, 0, ())


def _cmp_sample(cache5, layer, page_table, wc, bias, w2b):
    n_in = math.gcd(page_table.shape[1], CMP_PAGES_PER_STEP)
    db, n_pages = page_table.shape
    n_steps = n_pages // n_in
    specs = [pl.BlockSpec((None, None, PAGE_SIZE, 2 * N_KV_HEADS, HEAD_DIM),
                          functools.partial(lambda s, st, pt, i: (layer, pt[s, st * n_in + i], 0, 0, 0), i=i))
             for i in range(n_in)]
    return _cmp_call(db, n_steps, n_in, specs, [cache5] * n_in, wc, bias, w2b, 1, (page_table,))


def _overlap_shifted(n_rows, n_c, n_s, n_cols):
    m = np.arange(n_rows)[:, None]
    j = np.arange(n_cols)[None, :]
    c0 = (m - 1) * CMP_STRIDE
    s0 = j * SEL_BLOCK
    ov = (c0 < s0 + SEL_BLOCK) & (c0 + CMP_BLOCK > s0) & (m >= 1) & (m <= n_c) & (j < n_s)
    return jnp.asarray(ov.astype(np.float32))


def _select_blocks(imp, qpos, n_s, n_cols):
    j = lax.broadcasted_iota(jnp.int32, (1, n_cols), 1)
    jf = j.astype(F32)
    cur = qpos // SEL_BLOCK
    forced = (j == 0) | (j == cur) | (j == cur - 1)
    valid = j * SEL_BLOCK <= qpos
    score = jnp.where(forced, FORCE, jnp.where(valid, imp, -FORCE))
    score = jnp.where(j < n_s, score, -jnp.inf)
    sel = jnp.zeros(score.shape, F32)
    for _ in range(min(N_SEL, n_s)):
        mx = jnp.max(score, axis=-1, keepdims=True)
        idx = jnp.min(jnp.where(score == mx, jf, float(n_cols)), axis=-1, keepdims=True)
        hit = jf == idx
        sel = jnp.where(hit, 1.0, sel)
        score = jnp.where(hit, -jnp.inf, score)
    return sel


SEL_KEY_TILE = 512


def _softmax_cols(s, mask):
    s = jnp.where(mask, s, NEG_INF)
    m = jnp.max(s, axis=0, keepdims=True)
    p = jnp.where(mask, jnp.exp2(s - m), 0.0)
    return p * (1.0 / jnp.maximum(jnp.sum(p, axis=0, keepdims=True), TINY))


def _select_blocks_cols(imp, qpos, n_s):
    j = lax.broadcasted_iota(jnp.int32, (n_s, 1), 0)
    cur = qpos // SEL_BLOCK
    forced = (j == 0) | (j == cur) | (j == cur - 1)
    valid = j * SEL_BLOCK <= qpos
    score = jnp.where(forced, FORCE, jnp.where(valid, imp, -FORCE))
    sub = 8
    blocks = [score[b * sub:(b + 1) * sub] for b in range(n_s // sub)]
    ranks = [jnp.zeros(blk.shape, F32) for blk in blocks]
    row = lax.broadcasted_iota(jnp.int32, (sub, 1), 0)
    for i in range(n_s):
        bi, ri = divmod(i, sub)
        si = blocks[bi][ri:ri + 1, :]
        for b, blk in enumerate(blocks):
            if b < bi:
                beats = si > blk
            elif b > bi:
                beats = si >= blk
            else:
                beats = jnp.where(row > ri, jnp.where(si >= blk, 1.0, 0.0), jnp.where(si > blk, 1.0, 0.0)) > 0.5
            ranks[b] = ranks[b] + jnp.where(beats, 1.0, 0.0)
    rank = jnp.concatenate(ranks, axis=0)
    return jnp.where(rank < float(min(N_SEL, n_s)), 1.0, 0.0)


def _nsa_prompt_kernel(q_ref, kc_ref, vct_ref, ks_ref, vst_ref, kw_ref, vwt_ref, g_ref, ovt_ref, o_ref,
                       selbias_ref, *, t):
    start = pl.program_id(2) * Q_BLOCK
    n_cols = GROUP * Q_BLOCK
    q = q_ref[...] * (HEAD_DIM ** -0.5 * math.log2(math.e))
    q4 = jnp.concatenate([q[:, g * HEAD_DIM:(g + 1) * HEAD_DIM] for g in range(GROUP)], axis=0).astype(BF16)
    qpos1 = start + lax.broadcasted_iota(jnp.int32, (1, Q_BLOCK), 1)
    qpos4 = start + lax.broadcasted_iota(jnp.int32, (1, n_cols), 1) % Q_BLOCK

    n_cr = kc_ref.shape[0]
    n_s = ovt_ref.shape[0]
    s = _dot_nt(kc_ref[...].astype(BF16), q4)
    m_idx = lax.broadcasted_iota(jnp.int32, (n_cr, 1), 0)
    cmask = (m_idx >= 1) & (m_idx * CMP_STRIDE + (CMP_BLOCK - CMP_STRIDE - 1) <= qpos4)
    p_c = _softmax_cols(s, cmask)
    o_c = _dot(vct_ref[...].astype(BF16), p_c.astype(BF16))
    psum = p_c[:, 0:Q_BLOCK]
    for g in range(1, GROUP):
        psum = psum + p_c[:, g * Q_BLOCK:(g + 1) * Q_BLOCK]
    imp = _dot(ovt_ref[...], psum, HI)
    sel = _select_blocks_cols(imp, qpos1, n_s)

    wlen = WINDOW + Q_BLOCK
    w0 = pl.multiple_of(jnp.clip(start - WINDOW, 0, t - wlen), Q_BLOCK)
    sw = _dot_nt(kw_ref[pl.ds(w0, wlen), :], q4)
    dpos = qpos4 - (w0 + lax.broadcasted_iota(jnp.int32, (wlen, 1), 0))
    p_w = _softmax_cols(sw, (dpos >= 0) & (dpos < WINDOW))
    wblk = w0 // PAGE_SIZE
    vwt = jnp.concatenate([vwt_ref[wblk + i] for i in range(wlen // PAGE_SIZE)], axis=1)
    o_w = _dot(vwt, p_w.astype(BF16))

    tk = SEL_KEY_TILE
    blocks_per_tile = tk // SEL_BLOCK
    selbias_ref[...] = jnp.where(sel > 0.5, 0.0, NEG_INF)

    def scores(kt):
        return _dot_nt(ks_ref[pl.ds(pl.multiple_of(kt * tk, tk), tk), :], q4)

    def tile(kt, carry, last):
        m, l, acc, sc = carry
        sc_next = sc if last else scores(kt + 1)
        off = kt * tk
        rows = selbias_ref[pl.ds(pl.multiple_of(kt * blocks_per_tile, blocks_per_tile), blocks_per_tile), :]
        bias = jnp.concatenate([jnp.broadcast_to(rows[i:i + 1, :], (SEL_BLOCK, Q_BLOCK))
                                for i in range(blocks_per_tile)], axis=0)
        if last:
            kpos = off + lax.broadcasted_iota(jnp.int32, (tk, Q_BLOCK), 0)
            bias = jnp.where(kpos <= qpos1, bias, NEG_INF)
        sc = sc + jnp.concatenate([bias] * GROUP, axis=1)
        m_new = jnp.maximum(m, jnp.max(sc, axis=0, keepdims=True))
        alpha = jnp.exp2(m - m_new)
        p = jnp.exp2(sc - m_new)
        l = alpha * l + jnp.sum(p, axis=0, keepdims=True)
        vblk = kt * (tk // PAGE_SIZE)
        vst = jnp.concatenate([vst_ref[vblk + i] for i in range(tk // PAGE_SIZE)], axis=1)
        acc = alpha * acc + _dot(vst, p.astype(BF16))
        return m_new, l, acc, sc_next

    n_tiles = (start + Q_BLOCK + tk - 1) // tk
    m0 = jnp.full((1, n_cols), NEG_INF, F32)
    l0 = jnp.zeros((1, n_cols), F32)
    a0 = jnp.zeros((HEAD_DIM, n_cols), F32)
    carry = lax.fori_loop(0, n_tiles - 1, functools.partial(tile, last=False), (m0, l0, a0, scores(0)))
    _, l, acc, _ = tile(n_tiles - 1, carry, last=True)
    o_s = acc * (1.0 / jnp.maximum(l, TINY))

    gt = g_ref[...]
    o = gt[0:1, :] * o_c + gt[1:2, :] * o_s + gt[2:3, :] * o_w
    o_ref[...] = jnp.concatenate([o[:, g * Q_BLOCK:(g + 1) * Q_BLOCK].T for g in range(GROUP)],
                                 axis=1).astype(o_ref.dtype)


def _values_t(y_b, bsz, t, col0):
    v = lax.slice(y_b, (0, col0), (bsz * t, col0 + NKV_COLS))
    v = v.reshape(bsz, t // PAGE_SIZE, PAGE_SIZE, N_KV_HEADS, HEAD_DIM)
    return v.transpose(0, 3, 1, 4, 2)


def _nsa_prompt(y_main, y_b, cmp_p, gates, bsz, t, n_tok):
    assert t % SEL_KEY_TILE == 0 and t >= WINDOW + Q_BLOCK and SEL_KEY_TILE % PAGE_SIZE == 0
    np_tok = bsz * t
    nq = t // Q_BLOCK
    n_cr = t // CMP_STRIDE
    n_c = (t - CMP_BLOCK) // CMP_STRIDE + 1
    n_s = t // SEL_BLOCK
    assert n_s >= N_SEL and n_s % 8 == 0
    ovt = _overlap_shifted(n_cr, n_c, n_s, n_s).T
    n_pg = t // PAGE_SIZE
    v_sel_t = _values_t(y_b, bsz, t, NQ_COLS + 3 * NKV_COLS)
    v_win_t = _values_t(y_b, bsz, t, NQ_COLS + 5 * NKV_COLS)
    cmp_t = cmp_p.transpose(0, 2, 1)
    cb = NQ_COLS // HEAD_DIM
    g_t = gates[:np_tok].reshape(bsz, nq, Q_BLOCK, N_KV_HEADS, GROUP, 3).transpose(0, 3, 1, 5, 4, 2)
    g_t = g_t.reshape(bsz, N_KV_HEADS, nq, 3, GROUP * Q_BLOCK)
    seq_key = lambda c4: pl.BlockSpec((t, HEAD_DIM), lambda b, k, i: (b, cb + c4 * N_KV_HEADS + k))
    seq_val = pl.BlockSpec((None, None, n_pg, HEAD_DIM, PAGE_SIZE), lambda b, k, i: (b, k, 0, 0, 0))
    return pl.pallas_call(
        functools.partial(_nsa_prompt_kernel, t=t),
        grid=(bsz, N_KV_HEADS, nq),
        in_specs=[pl.BlockSpec((Q_BLOCK, QW), lambda b, k, i: (b * nq + i, k)),
                  pl.BlockSpec((None, n_cr, HEAD_DIM), lambda b, k, i: (b, 0, k)),
                  pl.BlockSpec((None, HEAD_DIM, n_cr), lambda b, k, i: (b, N_KV_HEADS + k, 0)),
                  seq_key(2), seq_val, seq_key(4), seq_val,
                  pl.BlockSpec((None, None, None, 3, GROUP * Q_BLOCK), lambda b, k, i: (b, k, i, 0, 0)),
                  pl.BlockSpec((n_s, n_cr), lambda b, k, i: (0, 0))],
        out_specs=pl.BlockSpec((Q_BLOCK, QW), lambda b, k, i: (b * nq + i, k)),
        out_shape=jax.ShapeDtypeStruct((n_tok, NQ_COLS), BF16),
        scratch_shapes=[pltpu.VMEM((n_s, Q_BLOCK), F32)],
        compiler_params=_params("parallel", "parallel", "arbitrary"),
    )(y_main, cmp_p, cmp_t, y_b, v_sel_t, y_b, v_win_t, g_t, ovt)


SAMPLE_ROWS = 8


def _nsa_sample_kernel(*refs, past, s_len, n_s, n_in):
    y_ref, cmp_ref = refs[1:3]
    pages = refs[3:3 + n_in]
    win_ref, g_ref, ov_ref, o_ref, q_scr, sel_scr, m_scr, l_scr, acc_scr, oc_scr = refs[3 + n_in:]
    pg = pl.program_id(1)
    n_pages = pl.num_programs(1)
    sr = SAMPLE_ROWS
    rk = GROUP * sr
    scale = HEAD_DIM ** -0.5
    tpos = lax.broadcasted_iota(jnp.int32, (rk, 1), 0) % sr
    qpos = past + tpos
    n_sc = sel_scr.shape[1]
    nbp = past // SEL_BLOCK
    kv0 = NQ_COLS

    @pl.when(pg == 0)
    def _():
        y = y_ref[...]
        n_cr = cmp_ref.shape[0]
        m_idx = lax.broadcasted_iota(jnp.int32, (1, n_cr), 1)
        cmask = (m_idx >= 1) & (m_idx * CMP_STRIDE + (CMP_BLOCK - CMP_STRIDE - 1) <= qpos)
        psums = []
        for k in range(N_KV_HEADS):
            qk = jnp.concatenate([y[:, (k * GROUP + g) * HEAD_DIM:(k * GROUP + g + 1) * HEAD_DIM]
                                  for g in range(GROUP)], axis=0) * scale
            qk = qk.astype(BF16)
            q_scr[k] = qk
            kc = cmp_ref[:, k * HEAD_DIM:(k + 1) * HEAD_DIM].astype(BF16)
            vc = cmp_ref[:, (N_KV_HEADS + k) * HEAD_DIM:(N_KV_HEADS + k + 1) * HEAD_DIM].astype(BF16)
            p_c = _masked_softmax(_dot_nt(qk, kc), cmask)
            oc_scr[k * rk:(k + 1) * rk, :] = _dot(p_c.astype(BF16), vc)
            psum = p_c[0:sr]
            for g in range(1, GROUP):
                psum = psum + p_c[g * sr:(g + 1) * sr]
            psums.append(psum)
        imp = _dot(jnp.concatenate(psums, axis=0), ov_ref[...], HI)
        qpos_sel = past + lax.broadcasted_iota(jnp.int32, (N_KV_HEADS * sr, 1), 0) % sr
        sel = _select_blocks(imp, qpos_sel, n_s, n_sc)
        for k in range(N_KV_HEADS):
            for g in range(GROUP):
                sel_scr[k * rk + g * sr:k * rk + (g + 1) * sr, :] = sel[k * sr:(k + 1) * sr]
        m_scr[...] = jnp.full(m_scr.shape, NEG_INF, F32)
        l_scr[...] = jnp.zeros(l_scr.shape, F32)
        acc_scr[...] = jnp.zeros(acc_scr.shape, F32)

    def online_update(k, sc, mask, v):
        rows = slice(k * rk, (k + 1) * rk)
        sc = jnp.where(mask, sc, NEG_INF)
        m_old = m_scr[rows, :]
        m_new = jnp.maximum(m_old, jnp.max(sc, axis=-1, keepdims=True))
        alpha = jnp.exp(m_old - m_new)
        p = jnp.where(mask, jnp.exp(sc - m_new), 0.0)
        l_scr[rows, :] = alpha * l_scr[rows, :] + jnp.sum(p, axis=-1, keepdims=True)
        acc_scr[rows, :] = alpha * acc_scr[rows, :] + _dot(p.astype(v.dtype), v)
        m_scr[rows, :] = m_new

    n_keys = n_in * PAGE_SIZE
    kpos = pg * n_keys + lax.broadcasted_iota(jnp.int32, (1, n_keys), 1)
    blk_row = lax.broadcasted_iota(jnp.int32, (n_sc, 1), 0)
    expand = jnp.where(blk_row == kpos // SEL_BLOCK, 1.0, 0.0).astype(BF16)
    selx = _dot(sel_scr[...].astype(BF16), expand)
    by_slot = [jnp.swapaxes(pages[i][...], 0, 1).astype(BF16) for i in range(n_in)]
    for k in range(N_KV_HEADS):
        kp = jnp.concatenate([by_slot[i][k] for i in range(n_in)], axis=0)
        vp = jnp.concatenate([by_slot[i][N_KV_HEADS + k] for i in range(n_in)], axis=0)
        mask = (selx[k * rk:(k + 1) * rk] > 0.5) & (kpos <= qpos)
        online_update(k, _dot_nt(q_scr[k], kp), mask, vp)

    @pl.when(pg == n_pages - 1)
    def _():
        y = y_ref[...]
        rpos = lax.broadcasted_iota(jnp.int32, (1, sr), 1)
        new_ok = (rpos < s_len) & (rpos <= tpos)
        gt = g_ref[...]
        win_by_slot = jnp.swapaxes(win_ref[...], 0, 1).astype(BF16)
        for k in range(N_KV_HEADS):
            rows = slice(k * rk, (k + 1) * rk)
            qk = q_scr[k]
            qf = qk.astype(F32)
            c_ks = kv0 + (2 * N_KV_HEADS + k) * HEAD_DIM
            c_vs = kv0 + (3 * N_KV_HEADS + k) * HEAD_DIM
            kn = y[:, c_ks:c_ks + HEAD_DIM]
            vn = y[:, c_vs:c_vs + HEAD_DIM]
            new_sel = sel_scr[rows, nbp:nbp + 1] > 0.5
            online_update(k, _dot_nt(qf, kn), new_sel & new_ok, vn)
            o_s = acc_scr[rows, :] / jnp.maximum(l_scr[rows, :], TINY)
            wb = win_ref.shape[0]
            c_kw = kv0 + (4 * N_KV_HEADS + k) * HEAD_DIM
            c_vw = kv0 + (5 * N_KV_HEADS + k) * HEAD_DIM
            kwb = win_by_slot[k]
            vwb = win_by_slot[N_KV_HEADS + k]
            kwn = y[:, c_kw:c_kw + HEAD_DIM]
            vwn = y[:, c_vw:c_vw + HEAD_DIM]
            dpos = qpos - (past - wb + lax.broadcasted_iota(jnp.int32, (1, wb), 1))
            mask_b = (dpos >= 0) & (dpos < WINDOW)
            s_b = jnp.where(mask_b, _dot_nt(qk, kwb), NEG_INF)
            s_n = jnp.where(new_ok, _dot_nt(qf, kwn), NEG_INF)
            mw = jnp.maximum(jnp.max(s_b, axis=-1, keepdims=True), jnp.max(s_n, axis=-1, keepdims=True))
            p_b = jnp.where(mask_b, jnp.exp(s_b - mw), 0.0)
            p_n = jnp.where(new_ok, jnp.exp(s_n - mw), 0.0)
            den = jnp.sum(p_b, axis=-1, keepdims=True) + jnp.sum(p_n, axis=-1, keepdims=True)
            o_w = (_dot(p_b.astype(BF16), vwb) + _dot(p_n, vwn)) / jnp.maximum(den, TINY)
            o_c = oc_scr[rows, :]
            for g in range(GROUP):
                r0, r1 = g * sr, (g + 1) * sr
                c = (k * GROUP + g) * 3
                o = gt[:, c:c + 1] * o_c[r0:r1] + gt[:, c + 1:c + 2] * o_s[r0:r1] + gt[:, c + 2:c + 3] * o_w[r0:r1]
                o_ref[:, (k * GROUP + g) * HEAD_DIM:(k * GROUP + g + 1) * HEAD_DIM] = o


SAMPLE_PAGES_PER_STEP = 8


def _nsa_sample(ys8, cmp_s, cache5, layer, win4, gates8, page_table, s_len):
    db, n_pages = page_table.shape
    past = n_pages * PAGE_SIZE
    n_in = math.gcd(n_pages, SAMPLE_PAGES_PER_STEP)
    assert s_len <= SAMPLE_ROWS and s_len <= SEL_BLOCK and past % SEL_BLOCK == 0
    n_cr = cmp_s.shape[1]
    n_c = (past + s_len - CMP_BLOCK) // CMP_STRIDE + 1
    assert n_c + 1 <= n_cr
    n_s = -(-(past + s_len) // SEL_BLOCK)
    n_sc = -(-n_s // 128) * 128
    ov = _overlap_shifted(n_cr, n_c, n_s, n_sc)
    wb = win4.shape[1]
    ncol = ys8.shape[2]
    rows = N_KV_HEADS * GROUP * SAMPLE_ROWS
    page_specs = [pl.BlockSpec((None, None, PAGE_SIZE, 2 * N_KV_HEADS, HEAD_DIM),
                               functools.partial(lambda b, p, pt, i: (layer, pt[b, p * n_in + i], 0, 1, 0), i=i))
                  for i in range(n_in)]
    grid_spec = pltpu.PrefetchScalarGridSpec(
        num_scalar_prefetch=1,
        grid=(db, n_pages // n_in),
        in_specs=[pl.BlockSpec((None, SAMPLE_ROWS, ncol), lambda b, p, pt: (b, 0, 0)),
                  pl.BlockSpec((None, n_cr, 2 * NKV_COLS), lambda b, p, pt: (b, 0, 0))]
                 + page_specs
                 + [pl.BlockSpec((None, wb, 2 * N_KV_HEADS, HEAD_DIM), lambda b, p, pt: (b, 0, 0, 0)),
                    pl.BlockSpec((None, SAMPLE_ROWS, 3 * N_HEADS), lambda b, p, pt: (b, 0, 0)),
                    pl.BlockSpec((n_cr, n_sc), lambda b, p, pt: (0, 0))],
        out_specs=pl.BlockSpec((None, SAMPLE_ROWS, NQ_COLS), lambda b, p, pt: (b, 0, 0)),
        scratch_shapes=[pltpu.VMEM((N_KV_HEADS, GROUP * SAMPLE_ROWS, HEAD_DIM), BF16),
                        pltpu.VMEM((rows, n_sc), F32),
                        pltpu.VMEM((rows, 1), F32),
                        pltpu.VMEM((rows, 1), F32),
                        pltpu.VMEM((rows, HEAD_DIM), F32),
                        pltpu.VMEM((rows, HEAD_DIM), F32)],
    )
    return pl.pallas_call(
        functools.partial(_nsa_sample_kernel, past=past, s_len=s_len, n_s=n_s, n_in=n_in),
        grid_spec=grid_spec,
        out_shape=jax.ShapeDtypeStruct((db, SAMPLE_ROWS, NQ_COLS), F32),
        compiler_params=_params("parallel", "arbitrary"),
    )(page_table, ys8, cmp_s, *([cache5] * n_in), win4, gates8, ov)


WKV_DIAG_BLOCK = 16
WKV_CHUNK = 64


def _bdot(a, b):
    return _dot(a.astype(BF16), b.astype(BF16))


def _bdot_nt(a, b):
    return _dot_nt(a.astype(BF16), b.astype(BF16))


def _bdot_tn(a, b):
    return _dot_tn(a.astype(BF16), b.astype(BF16))


def _unit_lower_inverses(mats, c):
    blk = min(WKV_DIAG_BLOCK, c)
    ri = lax.broadcasted_iota(jnp.int32, (c, c), 0)
    ci = lax.broadcasted_iota(jnp.int32, (c, c), 1)
    eye = jnp.where(ri == ci, 1.0, 0.0)
    same = (ri // blk) == (ci // blk)
    dp = [jnp.where(same, a, 0.0) for a in mats]
    td = [eye + d for d in dp]
    for _ in range(int(math.log2(blk)) - 1):
        dp = [_bdot(d, d) for d in dp]
        td = [t + _bdot(d, t) for d, t in zip(dp, td)]
    if blk == c:
        return td
    npow = [_bdot(t, jnp.where(same, 0.0, a)) for t, a in zip(td, mats)]
    tinv = [t + _bdot(x, t) for x, t in zip(npow, td)]
    for _ in range(int(math.log2(c // blk)) - 1):
        npow = [_bdot(x, x) for x in npow]
        tinv = [t + _bdot(x, t) for x, t in zip(npow, tinv)]
    return tinv


def _wkv_kernel(r_ref, k_ref, v_ref, lw_ref, a_ref, g_ref, kk_ref, ka_ref, rk_ref, lg_ref, lb_ref, s0_ref,
                z_ref, s_ref, *, c, nc, hb):
    n = RWKV_HEAD

    @pl.when(pl.program_id(2) == 0)
    def _():
        s_ref[...] = s0_ref[...]

    ri = lax.broadcasted_iota(jnp.int32, (c, c), 0)
    ci = lax.broadcasted_iota(jnp.int32, (c, c), 1)
    strict = ri > ci
    incl = ri >= ci
    tri = jnp.where(incl, 1.0, 0.0)
    lw_all = lw_ref[...]
    cum = jnp.concatenate([_dot(tri, lw_all[i * c:(i + 1) * c], HI) for i in range(nc)], axis=0)
    e_in_all = jnp.exp(cum)
    e_neg_all = jnp.exp(-cum)
    e_ex_all = jnp.exp(cum - lw_all)
    heads = range(hb)
    chunks = range(nc)
    sls = [slice(hh * n, (hh + 1) * n) for hh in heads]
    rws = [slice(i * c, (i + 1) * c) for i in chunks]
    r, v, k2, r_t, a_t, b_t, k_t, vb = ([] for _ in range(8))
    for sl in sls:
        k = k_ref[:, sl]
        a = a_ref[:, sl]
        kk = k * kk_ref[:, sl]
        kk = kk * (1.0 / jnp.maximum(jnp.sqrt(jnp.sum(kk * kk, axis=-1, keepdims=True)), 1e-12))
        r.append(r_ref[:, sl])
        v.append(v_ref[:, sl])
        k2.append(k * (1.0 + (a - 1.0) * ka_ref[:, sl]))
        r_t.append(r[-1] * e_in_all[:, sl])
        a_t.append(-kk * e_ex_all[:, sl])
        b_t.append(kk * a * e_neg_all[:, sl])
        k_t.append(k2[-1] * e_neg_all[:, sl])
        vb.append(v[-1].astype(BF16))

    items = [(hh, i) for hh in heads for i in chunks]
    aa = [_bdot_nt(jnp.concatenate([a_t[hh][rws[i]], r_t[hh][rws[i]]], axis=0),
                   jnp.concatenate([b_t[hh][rws[i]], k_t[hh][rws[i]]], axis=0)) for hh, i in items]
    a_ab = [jnp.where(strict, x[:c, :c], 0.0) for x in aa]
    a_ak = [jnp.where(strict, x[:c, c:], 0.0) for x in aa]
    a_rb = [jnp.where(incl, x[c:, :c], 0.0).astype(BF16) for x in aa]
    a_rk = [jnp.where(incl, x[c:, c:], 0.0) for x in aa]
    akv = [_bdot(x, vb[hh][rws[i]]) for x, (hh, i) in zip(a_ak, items)]
    yv = [_bdot(x, vb[hh][rws[i]]) for x, (hh, i) in zip(a_rk, items)]
    w_c = [e_in_all[i * c + c - 1:(i + 1) * c, sls[hh]] for hh, i in items]
    kv = [_bdot_tn(vb[hh][rws[i]], k_t[hh][rws[i]] * w) for w, (hh, i) in zip(w_c, items)]
    bh = [(b_t[hh][rws[i]] * w).astype(BF16) for w, (hh, i) in zip(w_c, items)]
    tinv = _unit_lower_inverses(a_ab, c)
    wt = [_bdot(t, a_t[hh][rws[i]]) for t, (hh, i) in zip(tinv, items)]
    u = [_bdot(t, x) for t, x in zip(tinv, akv)]
    wr = [jnp.concatenate([w, r_t[hh][rws[i]]], axis=0).astype(BF16) for w, (hh, i) in zip(wt, items)]

    s = [s_ref[hh] for hh in heads]
    ys = [[] for _ in heads]
    for i in chunks:
        idx = [hh * nc + i for hh in heads]
        ls = [_dot_nt(wr[j], s[hh].astype(BF16)) for hh, j in zip(heads, idx)]
        p = [x[:c] + u[j] for x, j in zip(ls, idx)]
        for hh, j in zip(heads, idx):
            ys[hh].append(ls[hh][c:] + _bdot(a_rb[j], p[hh]) + yv[j])
        s = [s[hh] * w_c[j] + _bdot_tn(p[hh], bh[j]) + kv[j] for hh, j in zip(heads, idx)]
    zs = []
    for hh, sl in zip(heads, sls):
        s_ref[hh] = s[hh]
        y = ys[hh][0] if nc == 1 else jnp.concatenate(ys[hh], axis=0)
        mu = jnp.mean(y, axis=-1, keepdims=True)
        yc = y - mu
        yn = yc * lax.rsqrt(jnp.mean(yc * yc, axis=-1, keepdims=True) + LNX_EPS)
        yn = yn * lg_ref[:, sl] + lb_ref[:, sl]
        bonus = jnp.sum(r[hh] * k2[hh] * rk_ref[:, sl], axis=-1, keepdims=True) * v[hh]
        zs.append((yn + bonus) * g_ref[:, sl])
    z_ref[...] = jnp.concatenate(zs, axis=1).astype(z_ref.dtype)


def _wkv(rkv, lw, a, g, k_k, k_a, r_k, lnx_g, lnx_b, s0, n_seq, t, c, nc, hb):
    rows, d = lw.shape
    h = d // RWKV_HEAD
    hw = hb * RWKV_HEAD
    tb = nc * c
    nblk = t // tb
    tok = lambda s, hg, ch: (s * nblk + ch, hg)
    par = lambda s, hg, ch: (0, hg)
    st = lambda s, hg, ch: (s, hg, 0, 0)
    rkv_spec = lambda p: pl.BlockSpec((None, tb, hw), lambda s, hg, ch: (p, s * nblk + ch, hg))
    return pl.pallas_call(
        functools.partial(_wkv_kernel, c=c, nc=nc, hb=hb),
        grid=(n_seq, h // hb, nblk),
        in_specs=[rkv_spec(0), rkv_spec(1), rkv_spec(2),
                  pl.BlockSpec((tb, hw), tok), pl.BlockSpec((tb, hw), tok), pl.BlockSpec((tb, hw), tok),
                  pl.BlockSpec((1, hw), par), pl.BlockSpec((1, hw), par), pl.BlockSpec((1, hw), par),
                  pl.BlockSpec((1, hw), par), pl.BlockSpec((1, hw), par),
                  pl.BlockSpec((None, hb, RWKV_HEAD, RWKV_HEAD), st)],
        out_specs=[pl.BlockSpec((tb, hw), tok),
                   pl.BlockSpec((None, hb, RWKV_HEAD, RWKV_HEAD), st)],
        out_shape=[jax.ShapeDtypeStruct((rows, d), BF16),
                   jax.ShapeDtypeStruct((n_seq, h, RWKV_HEAD, RWKV_HEAD), F32)],
        compiler_params=_params("parallel", "parallel", "arbitrary"),
    )(rkv, rkv, rkv, lw, a, g, k_k, k_a, r_k, lnx_g, lnx_b, s0)


WKV_PROMPT_CHUNKS_PER_STEP = 2
WKV_PROMPT_HEADS_PER_STEP = 8


def _to_slots_kernel(x_ref, o_ref):
    x = x_ref[...]
    by_slot = jnp.stack([x[:, s * HEAD_DIM:(s + 1) * HEAD_DIM] for s in range(o_ref.shape[1])], axis=0)
    o_ref[...] = jnp.swapaxes(by_slot, 0, 1)


def _to_slots(y, rows, col0, n_slots, *, tm_pref=512):
    width = n_slots * HEAD_DIM
    assert col0 % width == 0
    tm = _tile(rows, tm_pref)
    return pl.pallas_call(
        _to_slots_kernel,
        grid=(rows // tm,),
        in_specs=[pl.BlockSpec((tm, width), lambda i: (i, col0 // width))],
        out_specs=pl.BlockSpec((tm, n_slots, HEAD_DIM), lambda i: (i, 0, 0)),
        out_shape=jax.ShapeDtypeStruct((rows, n_slots, HEAD_DIM), y.dtype),
        compiler_params=_params("parallel"),
    )(y)


def _nsa_layer(x, np_tok, bsz, t, db, s_len, cache5, layer, win, page_table, w_in, pe, w1, w2, w_out,
               ln_g, ln_b, alpha):
    n_main = NQ_COLS + 6 * NKV_COLS
    y_main, y_b = _mm(x, w_in, layer, n_main, out_dtypes=(F32, BF16), tn_pref=512)
    w_gate = jnp.pad(w_in[layer:layer + 1, :, n_main:], ((0, 0), (0, 0), (0, 128 - 3 * N_HEADS)))
    gates = _mm(x, w_gate, 0, 128, act="sigmoid")[:, :3 * N_HEADS]

    wc, w2b = _cmp_weights(w1, w2)
    bias = _cmp_bias(pe, w1)
    cmp_p = _cmp_prompt(y_main, bsz, t, wc, bias, w2b)
    cmp_s = _cmp_sample(cache5, layer, page_table, wc, bias, w2b)

    o_p = _nsa_prompt(y_main, y_b, cmp_p, gates, bsz, t, x.shape[0])

    pad_rows = ((0, 0), (0, SAMPLE_ROWS - s_len), (0, 0))
    ys = y_main[np_tok:].reshape(db, s_len, n_main)
    ys8 = jnp.pad(ys, pad_rows)
    gates8 = jnp.pad(gates[np_tok:].reshape(db, s_len, 3 * N_HEADS), pad_rows)
    wb = win.shape[1]
    win4 = win.reshape(db, wb, 2 * N_KV_HEADS, HEAD_DIM)
    o_s = _nsa_sample(ys8, cmp_s, cache5, layer, win4, gates8, page_table, s_len)
    o_s = o_s[:, :s_len].reshape(db * s_len, NQ_COLS).astype(BF16)

    o = lax.dynamic_update_slice(o_p, o_s, (np_tok, 0))
    h = _mm_ln(o, w_out, layer, x, ln_g, ln_b, alpha)

    kv0 = NQ_COLS
    kw0 = NQ_COLS + 4 * NKV_COLS
    kv_p = _to_slots(y_main, np_tok, kv0, 4 * N_KV_HEADS).reshape(bsz, t, 4, N_KV_HEADS, HEAD_DIM)
    wlen = min(WINDOW, t)
    win_p = jnp.stack([lax.slice(y_main, (b * t + t - wlen, kw0), ((b + 1) * t, n_main)) for b in range(bsz)])
    win_p = win_p.reshape(bsz, wlen, 2, N_KV_HEADS, HEAD_DIM)
    kv_s = ys[:, :, kv0:kw0].reshape(db, s_len, 4, N_KV_HEADS, HEAD_DIM)
    kvw_s = ys[:, :, kw0:].reshape(db, s_len, 2, N_KV_HEADS, HEAD_DIM)
    win_s = jnp.concatenate([win, kvw_s], axis=1)[:, s_len:]
    return h, kv_p, kv_s, win_p, win_s


def _rwkv_layer(x, np_tok, bsz, t, db, s_len, shift_s, wkv_s, mu, w_rkv, w0, w1, w2, a0, a1, a2, g1, g2,
                k_k, k_a, r_k, lnx_g, lnx_b, w_out, layer, ln_g, ln_b, alpha):
    d = x.shape[1]
    heads = d // RWKV_HEAD
    xprev = jnp.concatenate([jnp.zeros((1, d), x.dtype), x[:-1]], axis=0)
    xprev = xprev.at[np.arange(1, bsz) * t].set(0.0)
    xprev = xprev.at[np_tok + np.arange(db) * s_len].set(shift_s.astype(x.dtype))

    row = lambda v: v.reshape(1, -1)
    rkv = _mm_mix(x, xprev, mu[:3, None, :], w_rkv)
    lw = _lora(x, xprev, row(mu[3]), w1, w2, row(w0), "decay")
    a = _lora(x, xprev, row(mu[4]), a1, a2, row(a0), "aaa")
    g = _lora(x, xprev, row(mu[5]), g1, g2, jnp.zeros((1, d), F32), "gate")

    pvec = (row(k_k), row(k_a), row(r_k), row(lnx_g), row(lnx_b))
    c_p = _tile(t, WKV_CHUNK, 8)
    nc_p = math.gcd(t // c_p, WKV_PROMPT_CHUNKS_PER_STEP)
    hb_p = math.gcd(heads, WKV_PROMPT_HEADS_PER_STEP)
    zero_state = jnp.zeros((bsz, heads, RWKV_HEAD, RWKV_HEAD), F32)
    z_p, st_p = _wkv(rkv, lw, a, g, *pvec, zero_state, bsz, t, c_p, nc_p, hb_p)

    c_s = -(-s_len // 8) * 8
    pad = lambda v: jnp.pad(v.reshape(v.shape[:-2] + (db, s_len, d)),
                            [(0, 0)] * (v.ndim - 1) + [(0, c_s - s_len), (0, 0)]
                            ).reshape(v.shape[:-2] + (db * c_s, d))
    z_s, st_s = _wkv(pad(rkv[:, np_tok:]), pad(lw[np_tok:]), pad(a[np_tok:]), pad(g[np_tok:]), *pvec,
                     wkv_s.astype(F32), db, c_s, c_s, 1, heads)
    z_s = z_s.reshape(db, c_s, d)[:, :s_len].reshape(db * s_len, d)

    z = lax.dynamic_update_slice(z_p, z_s, (np_tok, 0))
    h = _mm_ln(z, w_out, layer, x, ln_g, ln_b, alpha)
    last_p = x[np.arange(1, bsz + 1) * t - 1]
    last_s = x[np_tok + np.arange(1, db + 1) * s_len - 1]
    return h, st_p, st_s, last_p, last_s


def _mlp(h, w_up, w_down, layer, ln_g, ln_b, alpha):
    d_ff = w_up.shape[2]
    u = _mm(h, w_up, layer, d_ff, act="relu2", out_dtypes=(BF16,))
    return _mm_ln(u, w_down, layer, h, ln_g, ln_b, alpha)


def kernel(x_prompt, x_sample, cache_nsa_kv, state_nsa_win, state_rwkv_wkv, state_rwkv_shift, page_table,
           nsa_w_in, nsa_cmp_pe, nsa_cmp_w1, nsa_cmp_w2, nsa_w_out,
           rwkv_mu, rwkv_w_rkv, rwkv_w0, rwkv_w1, rwkv_w2, rwkv_a0, rwkv_a1, rwkv_a2, rwkv_g1, rwkv_g2,
           rwkv_k_k, rwkv_k_a, rwkv_r_k, rwkv_lnx_g, rwkv_lnx_b, rwkv_w_out,
           ffn_w_up, ffn_w_down, ln_g, ln_b):
    bsz, t, d = x_prompt.shape
    db, s_len, _ = x_sample.shape
    depth = ffn_w_up.shape[0]
    alpha = (2 * depth) ** 0.25
    np_tok = bsz * t
    x = jnp.concatenate([x_prompt.reshape(np_tok, d), x_sample.reshape(db * s_len, d)], axis=0)
    n_l, n_pool = cache_nsa_kv.shape[:2]
    cache5 = cache_nsa_kv.reshape(n_l, n_pool, PAGE_SIZE, 4 * N_KV_HEADS, HEAD_DIM)
    page_table = page_table.astype(jnp.int32)

    kv_p, kv_s, win_p, win_s, wkv_p, wkv_s, sh_p, sh_s = ([] for _ in range(8))
    row = lambda v: v.reshape(1, -1)
    for i in range(depth):
        j = i // 2
        if i % 2 == 0:
            h, kvp_new, kvs_new, wp_new, ws_new = _nsa_layer(
                x, np_tok, bsz, t, db, s_len, cache5, j, state_nsa_win[j], page_table,
                nsa_w_in, nsa_cmp_pe[j], nsa_cmp_w1[j], nsa_cmp_w2[j], nsa_w_out,
                row(ln_g[i, 0]), row(ln_b[i, 0]), alpha)
            kv_p.append(kvp_new)
            kv_s.append(kvs_new)
            win_p.append(wp_new)
            win_s.append(ws_new)
        else:
            h, sp_new, ss_new, hp_new, hs_new = _rwkv_layer(
                x, np_tok, bsz, t, db, s_len, state_rwkv_shift[j], state_rwkv_wkv[j],
                rwkv_mu[j], rwkv_w_rkv[j], rwkv_w0[j], rwkv_w1[j], rwkv_w2[j], rwkv_a0[j], rwkv_a1[j],
                rwkv_a2[j], rwkv_g1[j], rwkv_g2[j], rwkv_k_k[j], rwkv_k_a[j], rwkv_r_k[j],
                rwkv_lnx_g[j], rwkv_lnx_b[j], rwkv_w_out, j, row(ln_g[i, 0]), row(ln_b[i, 0]), alpha)
            wkv_p.append(sp_new)
            wkv_s.append(ss_new)
            sh_p.append(hp_new)
            sh_s.append(hs_new)
        x = _mlp(h, ffn_w_up, ffn_w_down, i, row(ln_g[i, 1]), row(ln_b[i, 1]), alpha)
    return (x[:np_tok].reshape(bsz, t, d), x[np_tok:].reshape(db, s_len, d),
            jnp.stack(kv_p), jnp.stack(kv_s), jnp.stack(win_p), jnp.stack(win_s),
            jnp.stack(wkv_p), jnp.stack(wkv_s), jnp.stack(sh_p), jnp.stack(sh_s))
```

```python
import functools
import math

import jax
import jax.numpy as jnp
import numpy as np
from jax import lax
from jax.experimental import pallas as pl
from jax.experimental.pallas import tpu as pltpu

F32 = jnp.float32
BF16 = jnp.bfloat16

HEAD_DIM = 128
N_KV_HEADS = 4
GROUP = 4
N_HEADS = N_KV_HEADS * GROUP
QW = GROUP * HEAD_DIM
NQ_COLS = N_HEADS * HEAD_DIM
NKV_COLS = N_KV_HEADS * HEAD_DIM
CMP_BLOCK = 32
CMP_STRIDE = 16
SEL_BLOCK = 64
N_SEL = 16
WINDOW = 512
Q_BLOCK = 256
PAGE_SIZE = 128
RWKV_HEAD = 64
LN_EPS = 1e-5
LNX_EPS = 64e-5
FORCE = 1e4
NEG_INF = -1e30
TINY = 1e-30
CHUNKS_PER_PAGE = PAGE_SIZE // CMP_STRIDE

VMEM_LIMIT_BYTES = 56 * 1024 * 1024
HI = lax.Precision.HIGHEST


def _params(*sem):
    return pltpu.CompilerParams(dimension_semantics=sem, vmem_limit_bytes=VMEM_LIMIT_BYTES)


def _tile(n, pref, mult=16):
    if n <= pref:
        return n
    for t in range(pref - pref % mult, 0, -mult):
        if n % t == 0:
            return t
    return n


def _dot(a, b, precision=None):
    return jnp.dot(a, b, preferred_element_type=F32, precision=precision)


def _dot_nt(a, b, precision=None):
    return lax.dot_general(a, b, (((1,), (1,)), ((), ())), preferred_element_type=F32, precision=precision)


def _dot_tn(a, b, precision=None):
    return lax.dot_general(a, b, (((0,), (0,)), ((), ())), preferred_element_type=F32, precision=precision)


def _masked_softmax(s, mask):
    s = jnp.where(mask, s, NEG_INF)
    m = jnp.max(s, axis=-1, keepdims=True)
    p = jnp.where(mask, jnp.exp(s - m), 0.0)
    return p * (1.0 / jnp.maximum(jnp.sum(p, axis=-1, keepdims=True), TINY))


def _layer_norm_rows(s, g, b, eps):
    mu = jnp.mean(s, axis=-1, keepdims=True)
    c = s - mu
    var = jnp.mean(c * c, axis=-1, keepdims=True)
    return c * lax.rsqrt(var + eps) * g + b


def _mm_kernel(x_ref, w_ref, *refs, act):
    out_refs, xb_ref = refs[:-1], refs[-1]

    @pl.when(pl.program_id(1) == 0)
    def _():
        xb_ref[...] = x_ref[...].astype(BF16)

    y = _dot(xb_ref[...], w_ref[...].astype(BF16))
    if act == "relu2":
        y = jnp.maximum(y, 0.0)
        y = y * y
    elif act == "sigmoid":
        y = jax.nn.sigmoid(y)
    for o_ref in out_refs:
        o_ref[...] = y.astype(o_ref.dtype)


def _mm(x, w, layer, n_out, *, act=None, out_dtypes=(F32,), tm_pref=1040, tn_pref=1024):
    m, k = x.shape
    tm = _tile(m, tm_pref)
    tn = _tile(n_out, tn_pref, 128)
    outs = pl.pallas_call(
        functools.partial(_mm_kernel, act=act),
        grid=(m // tm, n_out // tn),
        in_specs=[pl.BlockSpec((tm, k), lambda i, j: (i, 0)),
                  pl.BlockSpec((None, k, tn), lambda i, j: (layer, 0, j))],
        out_specs=[pl.BlockSpec((tm, tn), lambda i, j: (i, j)) for _ in out_dtypes],
        out_shape=[jax.ShapeDtypeStruct((m, n_out), dt) for dt in out_dtypes],
        scratch_shapes=[pltpu.VMEM((tm, k), BF16)],
        compiler_params=_params("parallel", "arbitrary"),
    )(x, w)
    return outs[0] if len(outs) == 1 else outs


def _mm_mix_kernel(x_ref, xp_ref, mu_ref, w_ref, o_ref, xb_ref):
    @pl.when(pl.program_id(2) == 0)
    def _():
        x = x_ref[...]
        xb_ref[...] = (x + (xp_ref[...] - x) * mu_ref[...]).astype(BF16)

    o_ref[...] = _dot(xb_ref[...], w_ref[...].astype(BF16))


def _mm_mix(x, xp, mu, w, *, tm_pref=832, tn_pref=512):
    m, k = x.shape
    npar, _, n = w.shape
    tm = _tile(m, tm_pref)
    tn = _tile(n, tn_pref, 128)
    return pl.pallas_call(
        _mm_mix_kernel,
        grid=(m // tm, npar, n // tn),
        in_specs=[pl.BlockSpec((tm, k), lambda i, p, j: (i, 0)),
                  pl.BlockSpec((tm, k), lambda i, p, j: (i, 0)),
                  pl.BlockSpec((None, 1, k), lambda i, p, j: (p, 0, 0)),
                  pl.BlockSpec((None, k, tn), lambda i, p, j: (p, 0, j))],
        out_specs=pl.BlockSpec((None, tm, tn), lambda i, p, j: (p, i, j)),
        out_shape=jax.ShapeDtypeStruct((npar, m, n), F32),
        scratch_shapes=[pltpu.VMEM((tm, k), BF16)],
        compiler_params=_params("parallel", "arbitrary", "arbitrary"),
    )(x, xp, mu, w)


def _loras_kernel(x_ref, xp_ref, mu_ref, w1_ref, w2_ref, w0_ref, a1_ref, a2_ref, a0_ref, g1_ref, g2_ref,
                  lw_ref, a_ref, g_ref):
    x = x_ref[...]
    xx = xp_ref[...] - x
    mix = lambda p: (x + xx * mu_ref[p:p + 1, :]).astype(BF16)
    low = lambda xm, w: _dot(xm, w[...].astype(BF16))
    up = lambda h, w: _dot(h.astype(BF16), w[...].astype(BF16))
    u = -(up(jnp.tanh(low(mix(0), w1_ref)), w2_ref) + w0_ref[...])
    softplus = jnp.maximum(u, 0.0) + jnp.log(1.0 + jnp.exp(-jnp.abs(u)))
    lw_ref[...] = -jnp.exp(-softplus - 0.5)
    a_ref[...] = jax.nn.sigmoid(up(low(mix(1), a1_ref), a2_ref) + a0_ref[...])
    g_ref[...] = up(jax.nn.sigmoid(low(mix(2), g1_ref)), g2_ref)


def _loras(x, xp, mu, w1, w2, w0, a1, a2, a0, g1, g2, *, tm_pref=320):
    m, k = x.shape
    n = w2.shape[1]
    tm = _tile(m, tm_pref)
    row = lambda i: (i, 0)
    fixed = lambda i: (0, 0)
    whole = lambda v: pl.BlockSpec(v.shape, fixed)
    out = jax.ShapeDtypeStruct((m, n), F32)
    return pl.pallas_call(
        _loras_kernel,
        grid=(m // tm,),
        in_specs=[pl.BlockSpec((tm, k), row), pl.BlockSpec((tm, k), row), whole(mu),
                  whole(w1), whole(w2), whole(w0), whole(a1), whole(a2), whole(a0), whole(g1), whole(g2)],
        out_specs=[pl.BlockSpec((tm, n), row)] * 3,
        out_shape=[out, out, out],
        compiler_params=_params("parallel"),
    )(x, xp, mu, w1, w2, w0, a1, a2, a0, g1, g2)


def _mm_ln_kernel(z_ref, w_ref, res_ref, g_ref, b_ref, o_ref, *, alpha):
    kk = pl.program_id(1)
    part = _dot(z_ref[...].astype(BF16), w_ref[...].astype(BF16))

    @pl.when(kk == 0)
    def _():
        o_ref[...] = part

    @pl.when(kk > 0)
    def _():
        o_ref[...] += part

    @pl.when(kk == pl.num_programs(1) - 1)
    def _():
        s = alpha * res_ref[...] + o_ref[...]
        o_ref[...] = _layer_norm_rows(s, g_ref[...], b_ref[...], LN_EPS)


def _mm_ln(z, w, layer, res, g, b, alpha, *, tm_pref=640, tk_pref=1024):
    m, k = z.shape
    n = w.shape[2]
    tm = _tile(m, tm_pref)
    tk = _tile(k, tk_pref, 128)
    return pl.pallas_call(
        functools.partial(_mm_ln_kernel, alpha=alpha),
        grid=(m // tm, k // tk),
        in_specs=[pl.BlockSpec((tm, tk), lambda i, kk: (i, kk)),
                  pl.BlockSpec((None, tk, n), lambda i, kk: (layer, kk, 0)),
                  pl.BlockSpec((tm, n), lambda i, kk: (i, 0)),
                  pl.BlockSpec((1, n), lambda i, kk: (0, 0)),
                  pl.BlockSpec((1, n), lambda i, kk: (0, 0))],
        out_specs=pl.BlockSpec((tm, n), lambda i, kk: (i, 0)),
        out_shape=jax.ShapeDtypeStruct((m, n), F32),
        compiler_params=_params("parallel", "arbitrary"),
    )(z, w, res, g, b)


def _cmp_bias_kernel(pe_ref, w1_ref, o_ref):
    acc = jnp.zeros((1, HEAD_DIM), F32)
    for rj in range(CMP_BLOCK):
        acc = acc + _dot(pe_ref[rj:rj + 1, :], w1_ref[rj], HI)
    o_ref[...] = acc


def _cmp_bias(pe, w1):
    return pl.pallas_call(
        _cmp_bias_kernel,
        grid=(2,),
        in_specs=[pl.BlockSpec((None, CMP_BLOCK, HEAD_DIM), lambda p: (p, 0, 0)),
                  pl.BlockSpec((None, CMP_BLOCK, HEAD_DIM, HEAD_DIM), lambda p: (p, 0, 0, 0))],
        out_specs=pl.BlockSpec((None, 1, HEAD_DIM), lambda p: (p, 0, 0)),
        out_shape=jax.ShapeDtypeStruct((2, 1, HEAD_DIM), F32),
        compiler_params=_params("parallel"),
    )(pe, w1)


def _cmp_kernel(*refs, n_in):
    pages = refs[-5 - n_in:-5]
    wc_ref, bias_ref, w2_ref, out_ref, carry_ref = refs[-5:]
    mk = CHUNKS_PER_PAGE * n_in
    rows = N_KV_HEADS * mk

    @pl.when(pl.program_id(1) == 0)
    def _():
        carry_ref[...] = jnp.zeros_like(carry_ref)

    first = (lax.broadcasted_iota(jnp.int32, (rows, 1), 0) % mk) == 0
    if len(pages[0].shape) == 3:
        split = [pages[i][...].reshape(CHUNKS_PER_PAGE, CMP_STRIDE, 2 * N_KV_HEADS, HEAD_DIM)
                 for i in range(n_in)]
        by_slot = [[jnp.swapaxes(split[i][:, j], 0, 1) for j in range(CMP_STRIDE)] for i in range(n_in)]
        piece = lambda i, j, slot: by_slot[i][j][slot]
    else:
        r_out = lax.broadcasted_iota(jnp.int32, (PAGE_SIZE, PAGE_SIZE), 0)
        r_in = lax.broadcasted_iota(jnp.int32, (PAGE_SIZE, PAGE_SIZE), 1)
        perm = jnp.where(r_in == (r_out % CHUNKS_PER_PAGE) * CMP_STRIDE + r_out // CHUNKS_PER_PAGE, 1.0, 0.0)
        grouped = [_dot(perm.astype(BF16), pages[i][...].astype(BF16)) for i in range(n_in)]
        piece = lambda i, j, slot: grouped[i][j * CHUNKS_PER_PAGE:(j + 1) * CHUNKS_PER_PAGE,
                                              slot * HEAD_DIM:(slot + 1) * HEAD_DIM]
    for p in range(2):
        lhs = jnp.concatenate(
            [jnp.concatenate([piece(i, j, p * N_KV_HEADS + k).astype(BF16) for j in range(CMP_STRIDE)], axis=1)
             for k in range(N_KV_HEADS) for i in range(n_in)], axis=0)
        acc = _dot(lhs, wc_ref[p])
        part0 = acc[:, :HEAD_DIM]
        part1 = acc[:, HEAD_DIM:]
        prev = jnp.where(first, carry_ref[p], pltpu.roll(part0, 1, 0))
        carry_ref[p] = pltpu.roll(part0, rows - (mk - 1), 0)
        h = prev + part1 + bias_ref[p]
        c = _dot(jax.nn.gelu(h).astype(BF16), w2_ref[p])
        for k in range(N_KV_HEADS):
            col = (p * N_KV_HEADS + k) * HEAD_DIM
            out_ref[:, col:col + HEAD_DIM] = c[k * mk:(k + 1) * mk]


def _cmp_weights(w1, w2):
    n_r = CMP_BLOCK // CMP_STRIDE
    w1r = w1.reshape(2, n_r, CMP_STRIDE, HEAD_DIM, HEAD_DIM)
    wc = jnp.concatenate([w1r[:, r] for r in range(n_r)], axis=-1).astype(BF16)
    return wc.reshape(2, CMP_STRIDE * HEAD_DIM, 2 * HEAD_DIM), w2.astype(BF16)


def _cmp_call(n_seq, n_steps, n_in, page_specs, operands, wc, bias, w2b, num_prefetch, prefetch):
    mk = CHUNKS_PER_PAGE * n_in
    if num_prefetch:
        fixed3 = lambda s, t, pt: (0, 0, 0)
        out_map = lambda s, t, pt: (s, t, 0)
    else:
        fixed3 = lambda s, t: (0, 0, 0)
        out_map = lambda s, t: (s, t, 0)
    grid_spec = pltpu.PrefetchScalarGridSpec(
        num_scalar_prefetch=num_prefetch,
        grid=(n_seq, n_steps),
        in_specs=page_specs + [
            pl.BlockSpec((2, CMP_STRIDE * HEAD_DIM, 2 * HEAD_DIM), fixed3),
            pl.BlockSpec((2, 1, HEAD_DIM), fixed3),
            pl.BlockSpec((2, HEAD_DIM, HEAD_DIM), fixed3)],
        out_specs=pl.BlockSpec((None, mk, 2 * NKV_COLS), out_map),
        scratch_shapes=[pltpu.VMEM((2, N_KV_HEADS * mk, HEAD_DIM), F32)],
    )
    return pl.pallas_call(
        functools.partial(_cmp_kernel, n_in=n_in),
        grid_spec=grid_spec,
        out_shape=jax.ShapeDtypeStruct((n_seq, n_steps * mk, 2 * NKV_COLS), F32),
        compiler_params=_params("parallel", "arbitrary"),
    )(*prefetch, *operands, wc, bias, w2b)


CMP_PAGES_PER_STEP = 8


def _cmp_prompt(y_main, bsz, t, wc, bias, w2b):
    pages_per_seq = t // PAGE_SIZE
    n_in = math.gcd(pages_per_seq, CMP_PAGES_PER_STEP)
    n_steps = pages_per_seq // n_in
    col_blk = NQ_COLS // (2 * NKV_COLS)
    specs = [pl.BlockSpec((PAGE_SIZE, 2 * NKV_COLS),
                          functools.partial(lambda s, st, i: (s * pages_per_seq + st * n_in + i, col_blk), i=i))
             for i in range(n_in)]
    return _cmp_call(bsz, n_steps, n_in, specs, [y_main] * n_in, wc, bias, w2b, 0, ())


def _cmp_sample(cache5, layer, page_table, wc, bias, w2b):
    n_in = math.gcd(page_table.shape[1], CMP_PAGES_PER_STEP)
    db, n_pages = page_table.shape
    n_steps = n_pages // n_in
    specs = [pl.BlockSpec((None, None, PAGE_SIZE, 2 * N_KV_HEADS, HEAD_DIM),
                          functools.partial(lambda s, st, pt, i: (layer, pt[s, st * n_in + i], 0, 0, 0), i=i))
             for i in range(n_in)]
    return _cmp_call(db, n_steps, n_in, specs, [cache5] * n_in, wc, bias, w2b, 1, (page_table,))


def _overlap_shifted(n_rows, n_c, n_s, n_cols):
    m = np.arange(n_rows)[:, None]
    j = np.arange(n_cols)[None, :]
    c0 = (m - 1) * CMP_STRIDE
    s0 = j * SEL_BLOCK
    ov = (c0 < s0 + SEL_BLOCK) & (c0 + CMP_BLOCK > s0) & (m >= 1) & (m <= n_c) & (j < n_s)
    return jnp.asarray(ov.astype(np.float32))


def _select_blocks(imp, qpos, n_s, n_cols):
    j = lax.broadcasted_iota(jnp.int32, (1, n_cols), 1)
    jf = j.astype(F32)
    cur = qpos // SEL_BLOCK
    forced = (j == 0) | (j == cur) | (j == cur - 1)
    valid = j * SEL_BLOCK <= qpos
    score = jnp.where(forced, FORCE, jnp.where(valid, imp, -FORCE))
    score = jnp.where(j < n_s, score, -jnp.inf)
    sel = jnp.zeros(score.shape, F32)
    for _ in range(min(N_SEL, n_s)):
        mx = jnp.max(score, axis=-1, keepdims=True)
        idx = jnp.min(jnp.where(score == mx, jf, float(n_cols)), axis=-1, keepdims=True)
        hit = jf == idx
        sel = jnp.where(hit, 1.0, sel)
        score = jnp.where(hit, -jnp.inf, score)
    return sel


SEL_KEY_TILE = 512


def _softmax_cols(s, mask):
    s = jnp.where(mask, s, NEG_INF)
    m = jnp.max(s, axis=0, keepdims=True)
    p = jnp.where(mask, jnp.exp2(s - m), 0.0)
    return p * (1.0 / jnp.maximum(jnp.sum(p, axis=0, keepdims=True), TINY))


def _select_blocks_cols(imp, qpos, n_s):
    j = lax.broadcasted_iota(jnp.int32, (n_s, 1), 0)
    cur = qpos // SEL_BLOCK
    forced = (j == 0) | (j == cur) | (j == cur - 1)
    valid = j * SEL_BLOCK <= qpos
    score = jnp.where(forced, FORCE, jnp.where(valid, imp, -FORCE))
    sub = 8
    blocks = [score[b * sub:(b + 1) * sub] for b in range(n_s // sub)]
    ranks = [jnp.zeros(blk.shape, F32) for blk in blocks]
    row = lax.broadcasted_iota(jnp.int32, (sub, 1), 0)
    for i in range(n_s):
        bi, ri = divmod(i, sub)
        si = blocks[bi][ri:ri + 1, :]
        for b, blk in enumerate(blocks):
            if b < bi:
                beats = si > blk
            elif b > bi:
                beats = si >= blk
            else:
                beats = jnp.where(row > ri, jnp.where(si >= blk, 1.0, 0.0), jnp.where(si > blk, 1.0, 0.0)) > 0.5
            ranks[b] = ranks[b] + jnp.where(beats, 1.0, 0.0)
    rank = jnp.concatenate(ranks, axis=0)
    return jnp.where(rank < float(min(N_SEL, n_s)), 1.0, 0.0)


def _nsa_prompt_kernel(q_ref, kc_ref, vct_ref, ks_ref, vst_ref, kw_ref, vwt_ref, g_ref, ovt_ref, o_ref,
                       selbias_ref, *, t):
    start = pl.program_id(2) * Q_BLOCK
    n_cols = GROUP * Q_BLOCK
    q = q_ref[...] * (HEAD_DIM ** -0.5 * math.log2(math.e))
    q4 = jnp.concatenate([q[:, g * HEAD_DIM:(g + 1) * HEAD_DIM] for g in range(GROUP)], axis=0).astype(BF16)
    qpos1 = start + lax.broadcasted_iota(jnp.int32, (1, Q_BLOCK), 1)
    qpos4 = start + lax.broadcasted_iota(jnp.int32, (1, n_cols), 1) % Q_BLOCK

    n_cr = kc_ref.shape[0]
    n_s = ovt_ref.shape[0]
    s = _dot_nt(kc_ref[...].astype(BF16), q4)
    m_idx = lax.broadcasted_iota(jnp.int32, (n_cr, 1), 0)
    cmask = (m_idx >= 1) & (m_idx * CMP_STRIDE + (CMP_BLOCK - CMP_STRIDE - 1) <= qpos4)
    p_c = _softmax_cols(s, cmask)
    o_c = _dot(vct_ref[...].astype(BF16), p_c.astype(BF16))
    psum = p_c[:, 0:Q_BLOCK]
    for g in range(1, GROUP):
        psum = psum + p_c[:, g * Q_BLOCK:(g + 1) * Q_BLOCK]
    imp = _dot(ovt_ref[...], psum, HI)
    sel = _select_blocks_cols(imp, qpos1, n_s)

    wlen = WINDOW + Q_BLOCK
    w0 = pl.multiple_of(jnp.clip(start - WINDOW, 0, t - wlen), PAGE_SIZE)
    sw = _dot_nt(kw_ref[pl.ds(w0, wlen), :], q4)
    dpos = qpos4 - (w0 + lax.broadcasted_iota(jnp.int32, (wlen, 1), 0))
    p_w = _softmax_cols(sw, (dpos >= 0) & (dpos < WINDOW))
    wblk = w0 // PAGE_SIZE
    vwt = jnp.concatenate([vwt_ref[wblk + i] for i in range(wlen // PAGE_SIZE)], axis=1)
    o_w = _dot(vwt, p_w.astype(BF16))

    tk = SEL_KEY_TILE
    blocks_per_tile = tk // SEL_BLOCK
    selbias_ref[...] = jnp.where(sel > 0.5, 0.0, NEG_INF)

    def scores(kt):
        return _dot_nt(ks_ref[pl.ds(pl.multiple_of(kt * tk, tk), tk), :], q4)

    def tile(kt, carry, last):
        m, l, acc, sc = carry
        sc_next = sc if last else scores(kt + 1)
        off = kt * tk
        rows = selbias_ref[pl.ds(pl.multiple_of(kt * blocks_per_tile, blocks_per_tile), blocks_per_tile), :]
        bias = jnp.concatenate([jnp.broadcast_to(rows[i:i + 1, :], (SEL_BLOCK, Q_BLOCK))
                                for i in range(blocks_per_tile)], axis=0)
        if last:
            kpos = off + lax.broadcasted_iota(jnp.int32, (tk, Q_BLOCK), 0)
            bias = jnp.where(kpos <= qpos1, bias, NEG_INF)
        sc = sc + jnp.concatenate([bias] * GROUP, axis=1)
        m_new = jnp.maximum(m, jnp.max(sc, axis=0, keepdims=True))
        alpha = jnp.exp2(m - m_new)
        p = jnp.exp2(sc - m_new)
        l = alpha * l + jnp.sum(p, axis=0, keepdims=True)
        vblk = kt * (tk // PAGE_SIZE)
        vst = jnp.concatenate([vst_ref[vblk + i] for i in range(tk // PAGE_SIZE)], axis=1)
        acc = alpha * acc + _dot(vst, p.astype(BF16))
        return m_new, l, acc, sc_next

    n_tiles = (start + Q_BLOCK + tk - 1) // tk
    m0 = jnp.full((1, n_cols), NEG_INF, F32)
    l0 = jnp.zeros((1, n_cols), F32)
    a0 = jnp.zeros((HEAD_DIM, n_cols), F32)
    carry = lax.fori_loop(0, n_tiles - 1, functools.partial(tile, last=False), (m0, l0, a0, scores(0)))
    _, l, acc, _ = tile(n_tiles - 1, carry, last=True)
    o_s = acc * (1.0 / jnp.maximum(l, TINY))

    gt = g_ref[...]
    o = gt[0:1, :] * o_c + gt[1:2, :] * o_s + gt[2:3, :] * o_w
    o_ref[...] = jnp.concatenate([o[:, g * Q_BLOCK:(g + 1) * Q_BLOCK].T for g in range(GROUP)],
                                 axis=1).astype(o_ref.dtype)


def _values_t(y_b, bsz, t, col0):
    v = lax.slice(y_b, (0, col0), (bsz * t, col0 + NKV_COLS))
    v = v.reshape(bsz, t // PAGE_SIZE, PAGE_SIZE, N_KV_HEADS, HEAD_DIM)
    return v.transpose(0, 3, 1, 4, 2)


def _nsa_prompt(y_main, y_b, cmp_p, gates, bsz, t, n_tok):
    assert t % SEL_KEY_TILE == 0 and t >= WINDOW + Q_BLOCK and SEL_KEY_TILE % PAGE_SIZE == 0
    np_tok = bsz * t
    nq = t // Q_BLOCK
    n_cr = t // CMP_STRIDE
    n_c = (t - CMP_BLOCK) // CMP_STRIDE + 1
    n_s = t // SEL_BLOCK
    assert n_s >= N_SEL and n_s % 8 == 0
    ovt = _overlap_shifted(n_cr, n_c, n_s, n_s).T
    n_pg = t // PAGE_SIZE
    v_sel_t = _values_t(y_b, bsz, t, NQ_COLS + 3 * NKV_COLS)
    v_win_t = _values_t(y_b, bsz, t, NQ_COLS + 5 * NKV_COLS)
    cmp_t = cmp_p.transpose(0, 2, 1)
    cb = NQ_COLS // HEAD_DIM
    g_t = gates[:np_tok].reshape(bsz, nq, Q_BLOCK, N_KV_HEADS, GROUP, 3).transpose(0, 3, 1, 5, 4, 2)
    g_t = g_t.reshape(bsz, N_KV_HEADS, nq, 3, GROUP * Q_BLOCK)
    seq_key = lambda c4: pl.BlockSpec((t, HEAD_DIM), lambda b, k, i: (b, cb + c4 * N_KV_HEADS + k))
    seq_val = pl.BlockSpec((None, None, n_pg, HEAD_DIM, PAGE_SIZE), lambda b, k, i: (b, k, 0, 0, 0))
    return pl.pallas_call(
        functools.partial(_nsa_prompt_kernel, t=t),
        grid=(bsz, N_KV_HEADS, nq),
        in_specs=[pl.BlockSpec((Q_BLOCK, QW), lambda b, k, i: (b * nq + i, k)),
                  pl.BlockSpec((None, n_cr, HEAD_DIM), lambda b, k, i: (b, 0, k)),
                  pl.BlockSpec((None, HEAD_DIM, n_cr), lambda b, k, i: (b, N_KV_HEADS + k, 0)),
                  seq_key(2), seq_val, seq_key(4), seq_val,
                  pl.BlockSpec((None, None, None, 3, GROUP * Q_BLOCK), lambda b, k, i: (b, k, i, 0, 0)),
                  pl.BlockSpec((n_s, n_cr), lambda b, k, i: (0, 0))],
        out_specs=pl.BlockSpec((Q_BLOCK, QW), lambda b, k, i: (b * nq + i, k)),
        out_shape=jax.ShapeDtypeStruct((n_tok, NQ_COLS), BF16),
        scratch_shapes=[pltpu.VMEM((n_s, Q_BLOCK), F32)],
        compiler_params=_params("parallel", "parallel", "arbitrary"),
    )(y_main, cmp_p, cmp_t, y_b, v_sel_t, y_b, v_win_t, g_t, ovt)


SAMPLE_ROWS = 8


def _nsa_sample_kernel(*refs, past, s_len, n_s, n_in):
    y_ref, cmp_ref = refs[1:3]
    pages = refs[3:3 + n_in]
    win_ref, g_ref, ov_ref, o_ref, q_scr, sel_scr, m_scr, l_scr, acc_scr, oc_scr = refs[3 + n_in:]
    pg = pl.program_id(1)
    n_pages = pl.num_programs(1)
    sr = SAMPLE_ROWS
    rk = GROUP * sr
    scale = HEAD_DIM ** -0.5
    tpos = lax.broadcasted_iota(jnp.int32, (rk, 1), 0) % sr
    qpos = past + tpos
    n_sc = sel_scr.shape[1]
    nbp = past // SEL_BLOCK
    kv0 = NQ_COLS

    @pl.when(pg == 0)
    def _():
        y = y_ref[...]
        n_cr = cmp_ref.shape[0]
        m_idx = lax.broadcasted_iota(jnp.int32, (1, n_cr), 1)
        cmask = (m_idx >= 1) & (m_idx * CMP_STRIDE + (CMP_BLOCK - CMP_STRIDE - 1) <= qpos)
        psums = []
        for k in range(N_KV_HEADS):
            qk = jnp.concatenate([y[:, (k * GROUP + g) * HEAD_DIM:(k * GROUP + g + 1) * HEAD_DIM]
                                  for g in range(GROUP)], axis=0) * scale
            qk = qk.astype(BF16)
            q_scr[k] = qk
            kc = cmp_ref[:, k * HEAD_DIM:(k + 1) * HEAD_DIM].astype(BF16)
            vc = cmp_ref[:, (N_KV_HEADS + k) * HEAD_DIM:(N_KV_HEADS + k + 1) * HEAD_DIM].astype(BF16)
            p_c = _masked_softmax(_dot_nt(qk, kc), cmask)
            oc_scr[k * rk:(k + 1) * rk, :] = _dot(p_c.astype(BF16), vc)
            psum = p_c[0:sr]
            for g in range(1, GROUP):
                psum = psum + p_c[g * sr:(g + 1) * sr]
            psums.append(psum)
        imp = _dot(jnp.concatenate(psums, axis=0), ov_ref[...], HI)
        qpos_sel = past + lax.broadcasted_iota(jnp.int32, (N_KV_HEADS * sr, 1), 0) % sr
        sel = _select_blocks(imp, qpos_sel, n_s, n_sc)
        for k in range(N_KV_HEADS):
            for g in range(GROUP):
                sel_scr[k * rk + g * sr:k * rk + (g + 1) * sr, :] = sel[k * sr:(k + 1) * sr]
        m_scr[...] = jnp.full(m_scr.shape, NEG_INF, F32)
        l_scr[...] = jnp.zeros(l_scr.shape, F32)
        acc_scr[...] = jnp.zeros(acc_scr.shape, F32)

    def online_update(k, sc, mask, v):
        rows = slice(k * rk, (k + 1) * rk)
        sc = jnp.where(mask, sc, NEG_INF)
        m_old = m_scr[rows, :]
        m_new = jnp.maximum(m_old, jnp.max(sc, axis=-1, keepdims=True))
        alpha = jnp.exp(m_old - m_new)
        p = jnp.where(mask, jnp.exp(sc - m_new), 0.0)
        l_scr[rows, :] = alpha * l_scr[rows, :] + jnp.sum(p, axis=-1, keepdims=True)
        acc_scr[rows, :] = alpha * acc_scr[rows, :] + _dot(p.astype(v.dtype), v)
        m_scr[rows, :] = m_new

    n_keys = n_in * PAGE_SIZE
    kpos = pg * n_keys + lax.broadcasted_iota(jnp.int32, (1, n_keys), 1)
    blk_row = lax.broadcasted_iota(jnp.int32, (n_sc, 1), 0)
    expand = jnp.where(blk_row == kpos // SEL_BLOCK, 1.0, 0.0).astype(BF16)
    selx = _dot(sel_scr[...].astype(BF16), expand)
    by_slot = [jnp.swapaxes(pages[i][...], 0, 1).astype(BF16) for i in range(n_in)]
    for k in range(N_KV_HEADS):
        kp = jnp.concatenate([by_slot[i][k] for i in range(n_in)], axis=0)
        vp = jnp.concatenate([by_slot[i][N_KV_HEADS + k] for i in range(n_in)], axis=0)
        mask = (selx[k * rk:(k + 1) * rk] > 0.5) & (kpos <= qpos)
        online_update(k, _dot_nt(q_scr[k], kp), mask, vp)

    @pl.when(pg == n_pages - 1)
    def _():
        y = y_ref[...]
        rpos = lax.broadcasted_iota(jnp.int32, (1, sr), 1)
        new_ok = (rpos < s_len) & (rpos <= tpos)
        gt = g_ref[...]
        win_by_slot = jnp.swapaxes(win_ref[...], 0, 1).astype(BF16)
        for k in range(N_KV_HEADS):
            rows = slice(k * rk, (k + 1) * rk)
            qk = q_scr[k]
            qf = qk.astype(F32)
            c_ks = kv0 + (2 * N_KV_HEADS + k) * HEAD_DIM
            c_vs = kv0 + (3 * N_KV_HEADS + k) * HEAD_DIM
            kn = y[:, c_ks:c_ks + HEAD_DIM]
            vn = y[:, c_vs:c_vs + HEAD_DIM]
            new_sel = sel_scr[rows, nbp:nbp + 1] > 0.5
            online_update(k, _dot_nt(qf, kn), new_sel & new_ok, vn)
            o_s = acc_scr[rows, :] / jnp.maximum(l_scr[rows, :], TINY)
            wb = win_ref.shape[0]
            c_kw = kv0 + (4 * N_KV_HEADS + k) * HEAD_DIM
            c_vw = kv0 + (5 * N_KV_HEADS + k) * HEAD_DIM
            kwb = win_by_slot[k]
            vwb = win_by_slot[N_KV_HEADS + k]
            kwn = y[:, c_kw:c_kw + HEAD_DIM]
            vwn = y[:, c_vw:c_vw + HEAD_DIM]
            dpos = qpos - (past - wb + lax.broadcasted_iota(jnp.int32, (1, wb), 1))
            mask_b = (dpos >= 0) & (dpos < WINDOW)
            s_b = jnp.where(mask_b, _dot_nt(qk, kwb), NEG_INF)
            s_n = jnp.where(new_ok, _dot_nt(qf, kwn), NEG_INF)
            mw = jnp.maximum(jnp.max(s_b, axis=-1, keepdims=True), jnp.max(s_n, axis=-1, keepdims=True))
            p_b = jnp.where(mask_b, jnp.exp(s_b - mw), 0.0)
            p_n = jnp.where(new_ok, jnp.exp(s_n - mw), 0.0)
            den = jnp.sum(p_b, axis=-1, keepdims=True) + jnp.sum(p_n, axis=-1, keepdims=True)
            o_w = (_dot(p_b.astype(BF16), vwb) + _dot(p_n, vwn)) / jnp.maximum(den, TINY)
            o_c = oc_scr[rows, :]
            for g in range(GROUP):
                r0, r1 = g * sr, (g + 1) * sr
                c = (k * GROUP + g) * 3
                o = gt[:, c:c + 1] * o_c[r0:r1] + gt[:, c + 1:c + 2] * o_s[r0:r1] + gt[:, c + 2:c + 3] * o_w[r0:r1]
                o_ref[:, (k * GROUP + g) * HEAD_DIM:(k * GROUP + g + 1) * HEAD_DIM] = o


SAMPLE_PAGES_PER_STEP = 8


def _nsa_sample(ys8, cmp_s, cache5, layer, win4, gates8, page_table, s_len):
    db, n_pages = page_table.shape
    past = n_pages * PAGE_SIZE
    n_in = math.gcd(n_pages, SAMPLE_PAGES_PER_STEP)
    assert s_len <= SAMPLE_ROWS and s_len <= SEL_BLOCK and past % SEL_BLOCK == 0
    n_cr = cmp_s.shape[1]
    n_c = (past + s_len - CMP_BLOCK) // CMP_STRIDE + 1
    assert n_c + 1 <= n_cr
    n_s = -(-(past + s_len) // SEL_BLOCK)
    n_sc = -(-n_s // 128) * 128
    ov = _overlap_shifted(n_cr, n_c, n_s, n_sc)
    wb = win4.shape[1]
    ncol = ys8.shape[2]
    rows = N_KV_HEADS * GROUP * SAMPLE_ROWS
    page_specs = [pl.BlockSpec((None, None, PAGE_SIZE, 2 * N_KV_HEADS, HEAD_DIM),
                               functools.partial(lambda b, p, pt, i: (layer, pt[b, p * n_in + i], 0, 1, 0), i=i))
                  for i in range(n_in)]
    grid_spec = pltpu.PrefetchScalarGridSpec(
        num_scalar_prefetch=1,
        grid=(db, n_pages // n_in),
        in_specs=[pl.BlockSpec((None, SAMPLE_ROWS, ncol), lambda b, p, pt: (b, 0, 0)),
                  pl.BlockSpec((None, n_cr, 2 * NKV_COLS), lambda b, p, pt: (b, 0, 0))]
                 + page_specs
                 + [pl.BlockSpec((None, wb, 2 * N_KV_HEADS, HEAD_DIM), lambda b, p, pt: (b, 0, 0, 0)),
                    pl.BlockSpec((None, SAMPLE_ROWS, 3 * N_HEADS), lambda b, p, pt: (b, 0, 0)),
                    pl.BlockSpec((n_cr, n_sc), lambda b, p, pt: (0, 0))],
        out_specs=pl.BlockSpec((None, SAMPLE_ROWS, NQ_COLS), lambda b, p, pt: (b, 0, 0)),
        scratch_shapes=[pltpu.VMEM((N_KV_HEADS, GROUP * SAMPLE_ROWS, HEAD_DIM), BF16),
                        pltpu.VMEM((rows, n_sc), F32),
                        pltpu.VMEM((rows, 1), F32),
                        pltpu.VMEM((rows, 1), F32),
                        pltpu.VMEM((rows, HEAD_DIM), F32),
                        pltpu.VMEM((rows, HEAD_DIM), F32)],
    )
    return pl.pallas_call(
        functools.partial(_nsa_sample_kernel, past=past, s_len=s_len, n_s=n_s, n_in=n_in),
        grid_spec=grid_spec,
        out_shape=jax.ShapeDtypeStruct((db, SAMPLE_ROWS, NQ_COLS), F32),
        compiler_params=_params("parallel", "arbitrary"),
    )(page_table, ys8, cmp_s, *([cache5] * n_in), win4, gates8, ov)


WKV_DIAG_BLOCK = 16
WKV_CHUNK = 64


def _bdot(a, b):
    return _dot(a.astype(BF16), b.astype(BF16))


def _bdot_nt(a, b):
    return _dot_nt(a.astype(BF16), b.astype(BF16))


def _bdot_tn(a, b):
    return _dot_tn(a.astype(BF16), b.astype(BF16))


def _unit_lower_inverses(mats, c):
    blk = min(WKV_DIAG_BLOCK, c)
    ri = lax.broadcasted_iota(jnp.int32, (c, c), 0)
    ci = lax.broadcasted_iota(jnp.int32, (c, c), 1)
    eye = jnp.where(ri == ci, 1.0, 0.0)
    same = (ri // blk) == (ci // blk)
    d = [jnp.where(same, a, 0.0) for a in mats]
    td = [eye + x for x in d]
    n_sq = int(math.log2(blk)) - 1
    pw = [_bdot(x, x) for x in d]
    for level in range(n_sq):
        if level + 1 < n_sq:
            both = [_bdot(jnp.concatenate([p, t], axis=0), p) for p, t in zip(pw, td)]
            pw = [x[:c] for x in both]
            td = [t + x[c:] for t, x in zip(td, both)]
        else:
            td = [t + _bdot(t, p) for t, p in zip(td, pw)]
    if blk == c:
        return td
    npow = [_bdot(t, jnp.where(same, 0.0, a)) for t, a in zip(td, mats)]
    tinv = [t + _bdot(x, t) for x, t in zip(npow, td)]
    for _ in range(int(math.log2(c // blk)) - 1):
        npow = [_bdot(x, x) for x in npow]
        tinv = [t + _bdot(x, t) for x, t in zip(npow, tinv)]
    return tinv


def _wkv_kernel(r_ref, k_ref, v_ref, lw_ref, a_ref, g_ref, kk_ref, ka_ref, rk_ref, lg_ref, lb_ref, s0_ref,
                z_ref, s_ref, *, c, nc, hb):
    n = RWKV_HEAD

    @pl.when(pl.program_id(2) == 0)
    def _():
        s_ref[...] = s0_ref[...]

    ri = lax.broadcasted_iota(jnp.int32, (c, c), 0)
    ci = lax.broadcasted_iota(jnp.int32, (c, c), 1)
    strict = ri > ci
    incl = ri >= ci
    tri = jnp.where(incl, 1.0, 0.0)
    lw_all = lw_ref[...]
    cum = jnp.concatenate([_dot(tri, lw_all[i * c:(i + 1) * c], HI) for i in range(nc)], axis=0)
    e_in_all = jnp.exp(cum)
    e_neg_all = jnp.exp(-cum)
    e_ex_all = jnp.exp(cum - lw_all)
    heads = range(hb)
    chunks = range(nc)
    sls = [slice(hh * n, (hh + 1) * n) for hh in heads]
    rws = [slice(i * c, (i + 1) * c) for i in chunks]
    r, v, k2, r_t, a_t, b_t, k_t, vb = ([] for _ in range(8))
    for sl in sls:
        k = k_ref[:, sl]
        a = a_ref[:, sl]
        kk = k * kk_ref[:, sl]
        kk = kk * (1.0 / jnp.maximum(jnp.sqrt(jnp.sum(kk * kk, axis=-1, keepdims=True)), 1e-12))
        r.append(r_ref[:, sl])
        v.append(v_ref[:, sl])
        k2.append(k * (1.0 + (a - 1.0) * ka_ref[:, sl]))
        r_t.append(r[-1] * e_in_all[:, sl])
        a_t.append(-kk * e_ex_all[:, sl])
        b_t.append(kk * a * e_neg_all[:, sl])
        k_t.append(k2[-1] * e_neg_all[:, sl])
        vb.append(v[-1].astype(BF16))

    items = [(hh, i) for hh in heads for i in chunks]
    aa = [_bdot_nt(jnp.concatenate([a_t[hh][rws[i]], r_t[hh][rws[i]]], axis=0),
                   jnp.concatenate([b_t[hh][rws[i]], k_t[hh][rws[i]]], axis=0)) for hh, i in items]
    r2 = lax.broadcasted_iota(jnp.int32, (2 * c, c), 0)
    c2 = lax.broadcasted_iota(jnp.int32, (2 * c, c), 1)
    tri2 = jnp.where(r2 < c, r2, r2 - c + 1) > c2
    a_ab = [jnp.where(strict, x[:c, :c], 0.0) for x in aa]
    a_rb = [jnp.where(incl, x[c:, :c], 0.0).astype(BF16) for x in aa]
    xkv = [_bdot(jnp.where(tri2, x[:, c:], 0.0), vb[hh][rws[i]]) for x, (hh, i) in zip(aa, items)]
    yv = [x[c:] for x in xkv]
    w_c = [e_in_all[i * c + c - 1:(i + 1) * c, sls[hh]] for hh, i in items]
    kv = [_bdot_tn(vb[hh][rws[i]], k_t[hh][rws[i]] * w) for w, (hh, i) in zip(w_c, items)]
    bh = [(b_t[hh][rws[i]] * w).astype(BF16) for w, (hh, i) in zip(w_c, items)]
    tinv = _unit_lower_inverses(a_ab, c)
    wt = [_bdot(t, a_t[hh][rws[i]]) for t, (hh, i) in zip(tinv, items)]
    u = [_bdot(t, x[:c]) for t, x in zip(tinv, xkv)]
    wr = [jnp.concatenate([w, r_t[hh][rws[i]]], axis=0).astype(BF16) for w, (hh, i) in zip(wt, items)]

    s = [s_ref[hh] for hh in heads]
    ys = [[] for _ in heads]
    for i in chunks:
        idx = [hh * nc + i for hh in heads]
        ls = [_dot_nt(wr[j], s[hh].astype(BF16)) for hh, j in zip(heads, idx)]
        p = [x[:c] + u[j] for x, j in zip(ls, idx)]
        for hh, j in zip(heads, idx):
            ys[hh].append(ls[hh][c:] + _bdot(a_rb[j], p[hh]) + yv[j])
        s = [s[hh] * w_c[j] + _bdot_tn(p[hh], bh[j]) + kv[j] for hh, j in zip(heads, idx)]
    zs = []
    for hh, sl in zip(heads, sls):
        s_ref[hh] = s[hh]
        y = ys[hh][0] if nc == 1 else jnp.concatenate(ys[hh], axis=0)
        mu = jnp.mean(y, axis=-1, keepdims=True)
        yc = y - mu
        yn = yc * lax.rsqrt(jnp.mean(yc * yc, axis=-1, keepdims=True) + LNX_EPS)
        yn = yn * lg_ref[:, sl] + lb_ref[:, sl]
        bonus = jnp.sum(r[hh] * k2[hh] * rk_ref[:, sl], axis=-1, keepdims=True) * v[hh]
        zs.append((yn + bonus) * g_ref[:, sl])
    z_ref[...] = jnp.concatenate(zs, axis=1).astype(z_ref.dtype)


def _wkv(rkv, lw, a, g, k_k, k_a, r_k, lnx_g, lnx_b, s0, n_seq, t, c, nc, hb):
    rows, d = lw.shape
    h = d // RWKV_HEAD
    hw = hb * RWKV_HEAD
    tb = nc * c
    nblk = t // tb
    tok = lambda s, hg, ch: (s * nblk + ch, hg)
    par = lambda s, hg, ch: (0, hg)
    st = lambda s, hg, ch: (s, hg, 0, 0)
    rkv_spec = lambda p: pl.BlockSpec((None, tb, hw), lambda s, hg, ch: (p, s * nblk + ch, hg))
    return pl.pallas_call(
        functools.partial(_wkv_kernel, c=c, nc=nc, hb=hb),
        grid=(n_seq, h // hb, nblk),
        in_specs=[rkv_spec(0), rkv_spec(1), rkv_spec(2),
                  pl.BlockSpec((tb, hw), tok), pl.BlockSpec((tb, hw), tok), pl.BlockSpec((tb, hw), tok),
                  pl.BlockSpec((1, hw), par), pl.BlockSpec((1, hw), par), pl.BlockSpec((1, hw), par),
                  pl.BlockSpec((1, hw), par), pl.BlockSpec((1, hw), par),
                  pl.BlockSpec((None, hb, RWKV_HEAD, RWKV_HEAD), st)],
        out_specs=[pl.BlockSpec((tb, hw), tok),
                   pl.BlockSpec((None, hb, RWKV_HEAD, RWKV_HEAD), st)],
        out_shape=[jax.ShapeDtypeStruct((rows, d), BF16),
                   jax.ShapeDtypeStruct((n_seq, h, RWKV_HEAD, RWKV_HEAD), F32)],
        compiler_params=_params("parallel", "parallel", "arbitrary"),
    )(rkv, rkv, rkv, lw, a, g, k_k, k_a, r_k, lnx_g, lnx_b, s0)


WKV_PROMPT_CHUNKS_PER_STEP = 2
WKV_PROMPT_HEADS_PER_STEP = 8


def _to_slots_kernel(x_ref, o_ref):
    x = x_ref[...]
    by_slot = jnp.stack([x[:, s * HEAD_DIM:(s + 1) * HEAD_DIM] for s in range(o_ref.shape[1])], axis=0)
    o_ref[...] = jnp.swapaxes(by_slot, 0, 1)


def _to_slots(y, rows, col0, n_slots, *, tm_pref=512):
    width = n_slots * HEAD_DIM
    assert col0 % width == 0
    tm = _tile(rows, tm_pref)
    return pl.pallas_call(
        _to_slots_kernel,
        grid=(rows // tm,),
        in_specs=[pl.BlockSpec((tm, width), lambda i: (i, col0 // width))],
        out_specs=pl.BlockSpec((tm, n_slots, HEAD_DIM), lambda i: (i, 0, 0)),
        out_shape=jax.ShapeDtypeStruct((rows, n_slots, HEAD_DIM), y.dtype),
        compiler_params=_params("parallel"),
    )(y)


def _nsa_layer(x, np_tok, bsz, t, db, s_len, cache5, layer, win, page_table, w_in, pe, w1, w2, w_out,
               ln_g, ln_b, alpha):
    n_main = NQ_COLS + 6 * NKV_COLS
    y_main, y_b = _mm(x, w_in, layer, n_main, out_dtypes=(F32, BF16), tn_pref=512)
    w_gate = jnp.pad(w_in[layer:layer + 1, :, n_main:], ((0, 0), (0, 0), (0, 128 - 3 * N_HEADS)))
    gates = _mm(x, w_gate, 0, 128, act="sigmoid")[:, :3 * N_HEADS]

    wc, w2b = _cmp_weights(w1, w2)
    bias = _cmp_bias(pe, w1)
    cmp_p = _cmp_prompt(y_main, bsz, t, wc, bias, w2b)
    cmp_s = _cmp_sample(cache5, layer, page_table, wc, bias, w2b)

    o_p = _nsa_prompt(y_main, y_b, cmp_p, gates, bsz, t, x.shape[0])

    pad_rows = ((0, 0), (0, SAMPLE_ROWS - s_len), (0, 0))
    ys = y_main[np_tok:].reshape(db, s_len, n_main)
    ys8 = jnp.pad(ys, pad_rows)
    gates8 = jnp.pad(gates[np_tok:].reshape(db, s_len, 3 * N_HEADS), pad_rows)
    wb = win.shape[1]
    win4 = win.reshape(db, wb, 2 * N_KV_HEADS, HEAD_DIM)
    o_s = _nsa_sample(ys8, cmp_s, cache5, layer, win4, gates8, page_table, s_len)
    o_s = o_s[:, :s_len].reshape(db * s_len, NQ_COLS).astype(BF16)

    o = lax.dynamic_update_slice(o_p, o_s, (np_tok, 0))
    h = _mm_ln(o, w_out, layer, x, ln_g, ln_b, alpha)

    kv0 = NQ_COLS
    kw0 = NQ_COLS + 4 * NKV_COLS
    kv_p = _to_slots(y_main, np_tok, kv0, 4 * N_KV_HEADS).reshape(bsz, t, 4, N_KV_HEADS, HEAD_DIM)
    wlen = min(WINDOW, t)
    win_p = jnp.stack([lax.slice(y_main, (b * t + t - wlen, kw0), ((b + 1) * t, n_main)) for b in range(bsz)])
    win_p = win_p.reshape(bsz, wlen, 2, N_KV_HEADS, HEAD_DIM)
    kv_s = ys[:, :, kv0:kw0].reshape(db, s_len, 4, N_KV_HEADS, HEAD_DIM)
    kvw_s = ys[:, :, kw0:].reshape(db, s_len, 2, N_KV_HEADS, HEAD_DIM)
    win_s = jnp.concatenate([win, kvw_s], axis=1)[:, s_len:]
    return h, kv_p, kv_s, win_p, win_s


def _rwkv_layer(x, np_tok, bsz, t, db, s_len, shift_s, wkv_s, mu, w_rkv, w0, w1, w2, a0, a1, a2, g1, g2,
                k_k, k_a, r_k, lnx_g, lnx_b, w_out, layer, ln_g, ln_b, alpha):
    d = x.shape[1]
    heads = d // RWKV_HEAD
    xprev = jnp.concatenate([jnp.zeros((1, d), x.dtype), x[:-1]], axis=0)
    xprev = xprev.at[np.arange(1, bsz) * t].set(0.0)
    xprev = xprev.at[np_tok + np.arange(db) * s_len].set(shift_s.astype(x.dtype))

    row = lambda v: v.reshape(1, -1)
    rkv = _mm_mix(x, xprev, mu[:3, None, :], w_rkv)
    lw, a, g = _loras(x, xprev, mu[3:6], w1, w2, row(w0), a1, a2, row(a0), g1, g2)

    pvec = (row(k_k), row(k_a), row(r_k), row(lnx_g), row(lnx_b))
    c_p = _tile(t, WKV_CHUNK, 8)
    nc_p = math.gcd(t // c_p, WKV_PROMPT_CHUNKS_PER_STEP)
    hb_p = math.gcd(heads, WKV_PROMPT_HEADS_PER_STEP)
    zero_state = jnp.zeros((bsz, heads, RWKV_HEAD, RWKV_HEAD), F32)
    z_p, st_p = _wkv(rkv, lw, a, g, *pvec, zero_state, bsz, t, c_p, nc_p, hb_p)

    c_s = -(-s_len // 8) * 8
    pad = lambda v: jnp.pad(v.reshape(v.shape[:-2] + (db, s_len, d)),
                            [(0, 0)] * (v.ndim - 1) + [(0, c_s - s_len), (0, 0)]
                            ).reshape(v.shape[:-2] + (db * c_s, d))
    z_s, st_s = _wkv(pad(rkv[:, np_tok:]), pad(lw[np_tok:]), pad(a[np_tok:]), pad(g[np_tok:]), *pvec,
                     wkv_s.astype(F32), db, c_s, c_s, 1, heads)
    z_s = z_s.reshape(db, c_s, d)[:, :s_len].reshape(db * s_len, d)

    z = lax.dynamic_update_slice(z_p, z_s, (np_tok, 0))
    h = _mm_ln(z, w_out, layer, x, ln_g, ln_b, alpha)
    last_p = x[np.arange(1, bsz + 1) * t - 1]
    last_s = x[np_tok + np.arange(1, db + 1) * s_len - 1]
    return h, st_p, st_s, last_p, last_s


def _mlp(h, w_up, w_down, layer, ln_g, ln_b, alpha):
    d_ff = w_up.shape[2]
    u = _mm(h, w_up, layer, d_ff, act="relu2", out_dtypes=(BF16,))
    return _mm_ln(u, w_down, layer, h, ln_g, ln_b, alpha)


def kernel(x_prompt, x_sample, cache_nsa_kv, state_nsa_win, state_rwkv_wkv, state_rwkv_shift, page_table,
           nsa_w_in, nsa_cmp_pe, nsa_cmp_w1, nsa_cmp_w2, nsa_w_out,
           rwkv_mu, rwkv_w_rkv, rwkv_w0, rwkv_w1, rwkv_w2, rwkv_a0, rwkv_a1, rwkv_a2, rwkv_g1, rwkv_g2,
           rwkv_k_k, rwkv_k_a, rwkv_r_k, rwkv_lnx_g, rwkv_lnx_b, rwkv_w_out,
           ffn_w_up, ffn_w_down, ln_g, ln_b):
    bsz, t, d = x_prompt.shape
    db, s_len, _ = x_sample.shape
    depth = ffn_w_up.shape[0]
    alpha = (2 * depth) ** 0.25
    np_tok = bsz * t
    x = jnp.concatenate([x_prompt.reshape(np_tok, d), x_sample.reshape(db * s_len, d)], axis=0)
    n_l, n_pool = cache_nsa_kv.shape[:2]
    cache5 = cache_nsa_kv.reshape(n_l, n_pool, PAGE_SIZE, 4 * N_KV_HEADS, HEAD_DIM)
    page_table = page_table.astype(jnp.int32)

    kv_p, kv_s, win_p, win_s, wkv_p, wkv_s, sh_p, sh_s = ([] for _ in range(8))
    row = lambda v: v.reshape(1, -1)
    for i in range(depth):
        j = i // 2
        if i % 2 == 0:
            h, kvp_new, kvs_new, wp_new, ws_new = _nsa_layer(
                x, np_tok, bsz, t, db, s_len, cache5, j, state_nsa_win[j], page_table,
                nsa_w_in, nsa_cmp_pe[j], nsa_cmp_w1[j], nsa_cmp_w2[j], nsa_w_out,
                row(ln_g[i, 0]), row(ln_b[i, 0]), alpha)
            kv_p.append(kvp_new)
            kv_s.append(kvs_new)
            win_p.append(wp_new)
            win_s.append(ws_new)
        else:
            h, sp_new, ss_new, hp_new, hs_new = _rwkv_layer(
                x, np_tok, bsz, t, db, s_len, state_rwkv_shift[j], state_rwkv_wkv[j],
                rwkv_mu[j], rwkv_w_rkv[j], rwkv_w0[j], rwkv_w1[j], rwkv_w2[j], rwkv_a0[j], rwkv_a1[j],
                rwkv_a2[j], rwkv_g1[j], rwkv_g2[j], rwkv_k_k[j], rwkv_k_a[j], rwkv_r_k[j],
                rwkv_lnx_g[j], rwkv_lnx_b[j], rwkv_w_out, j, row(ln_g[i, 0]), row(ln_b[i, 0]), alpha)
            wkv_p.append(sp_new)
            wkv_s.append(ss_new)
            sh_p.append(hp_new)
            sh_s.append(hs_new)
        x = _mlp(h, ffn_w_up, ffn_w_down, i, row(ln_g[i, 1]), row(ln_b[i, 1]), alpha)
    return (x[:np_tok].reshape(bsz, t, d), x[np_tok:].reshape(db, s_len, d),
            jnp.stack(kv_p), jnp.stack(kv_s), jnp.stack(win_p), jnp.stack(win_s),
            jnp.stack(wkv_p), jnp.stack(wkv_s), jnp.stack(sh_p), jnp.stack(sh_s))
```

```python
import functools
import math

import jax
import jax.numpy as jnp
import numpy as np
from jax import lax
from jax.experimental import pallas as pl
from jax.experimental.pallas import tpu as pltpu

F32 = jnp.float32
BF16 = jnp.bfloat16

HEAD_DIM = 128
N_KV_HEADS = 4
GROUP = 4
N_HEADS = N_KV_HEADS * GROUP
QW = GROUP * HEAD_DIM
NQ_COLS = N_HEADS * HEAD_DIM
NKV_COLS = N_KV_HEADS * HEAD_DIM
CMP_BLOCK = 32
CMP_STRIDE = 16
SEL_BLOCK = 64
N_SEL = 16
WINDOW = 512
Q_BLOCK = 256
PAGE_SIZE = 128
RWKV_HEAD = 64
LN_EPS = 1e-5
LNX_EPS = 64e-5
FORCE = 1e4
NEG_INF = -1e30
TINY = 1e-30
CHUNKS_PER_PAGE = PAGE_SIZE // CMP_STRIDE

VMEM_LIMIT_BYTES = 56 * 1024 * 1024
HI = lax.Precision.HIGHEST


def _params(*sem):
    return pltpu.CompilerParams(dimension_semantics=sem, vmem_limit_bytes=VMEM_LIMIT_BYTES)


def _tile(n, pref, mult=16):
    if n <= pref:
        return n
    for t in range(pref - pref % mult, 0, -mult):
        if n % t == 0:
            return t
    return n


def _dot(a, b, precision=None):
    return jnp.dot(a, b, preferred_element_type=F32, precision=precision)


def _dot_nt(a, b, precision=None):
    return lax.dot_general(a, b, (((1,), (1,)), ((), ())), preferred_element_type=F32, precision=precision)


def _dot_tn(a, b, precision=None):
    return lax.dot_general(a, b, (((0,), (0,)), ((), ())), preferred_element_type=F32, precision=precision)


def _masked_softmax(s, mask):
    s = jnp.where(mask, s, NEG_INF)
    m = jnp.max(s, axis=-1, keepdims=True)
    p = jnp.where(mask, jnp.exp(s - m), 0.0)
    return p * (1.0 / jnp.maximum(jnp.sum(p, axis=-1, keepdims=True), TINY))


def _layer_norm_rows(s, g, b, eps):
    mu = jnp.mean(s, axis=-1, keepdims=True)
    c = s - mu
    var = jnp.mean(c * c, axis=-1, keepdims=True)
    return c * lax.rsqrt(var + eps) * g + b


def _mm_kernel(x_ref, w_ref, *refs, act):
    out_refs, xb_ref = refs[:-1], refs[-1]

    @pl.when(pl.program_id(1) == 0)
    def _():
        xb_ref[...] = x_ref[...].astype(BF16)

    y = _dot(xb_ref[...], w_ref[...].astype(BF16))
    if act == "relu2":
        y = jnp.maximum(y, 0.0)
        y = y * y
    elif act == "sigmoid":
        y = jax.nn.sigmoid(y)
    for o_ref in out_refs:
        o_ref[...] = y.astype(o_ref.dtype)


def _mm(x, w, layer, n_out, *, act=None, out_dtypes=(F32,), tm_pref=1040, tn_pref=1024):
    m, k = x.shape
    tm = _tile(m, tm_pref)
    tn = _tile(n_out, tn_pref, 128)
    outs = pl.pallas_call(
        functools.partial(_mm_kernel, act=act),
        grid=(m // tm, n_out // tn),
        in_specs=[pl.BlockSpec((tm, k), lambda i, j: (i, 0)),
                  pl.BlockSpec((None, k, tn), lambda i, j: (layer, 0, j))],
        out_specs=[pl.BlockSpec((tm, tn), lambda i, j: (i, j)) for _ in out_dtypes],
        out_shape=[jax.ShapeDtypeStruct((m, n_out), dt) for dt in out_dtypes],
        scratch_shapes=[pltpu.VMEM((tm, k), BF16)],
        compiler_params=_params("parallel", "arbitrary"),
    )(x, w)
    return outs[0] if len(outs) == 1 else outs


def _mm_mix_kernel(x_ref, xp_ref, mu_ref, w_ref, o_ref, xb_ref):
    @pl.when(pl.program_id(2) == 0)
    def _():
        x = x_ref[...]
        xb_ref[...] = (x + (xp_ref[...] - x) * mu_ref[...]).astype(BF16)

    o_ref[...] = _dot(xb_ref[...], w_ref[...].astype(BF16))


def _mm_mix(x, xp, mu, w, *, tm_pref=832, tn_pref=512):
    m, k = x.shape
    npar, _, n = w.shape
    tm = _tile(m, tm_pref)
    tn = _tile(n, tn_pref, 128)
    return pl.pallas_call(
        _mm_mix_kernel,
        grid=(m // tm, npar, n // tn),
        in_specs=[pl.BlockSpec((tm, k), lambda i, p, j: (i, 0)),
                  pl.BlockSpec((tm, k), lambda i, p, j: (i, 0)),
                  pl.BlockSpec((None, 1, k), lambda i, p, j: (p, 0, 0)),
                  pl.BlockSpec((None, k, tn), lambda i, p, j: (p, 0, j))],
        out_specs=pl.BlockSpec((None, tm, tn), lambda i, p, j: (p, i, j)),
        out_shape=jax.ShapeDtypeStruct((npar, m, n), F32),
        scratch_shapes=[pltpu.VMEM((tm, k), BF16)],
        compiler_params=_params("parallel", "arbitrary", "arbitrary"),
    )(x, xp, mu, w)


def _loras_kernel(x_ref, xp_ref, mu_ref, w1_ref, w2_ref, w0_ref, a1_ref, a2_ref, a0_ref, g1_ref, g2_ref,
                  lw_ref, a_ref, g_ref):
    x = x_ref[...]
    xx = xp_ref[...] - x
    mix = lambda p: (x + xx * mu_ref[p:p + 1, :]).astype(BF16)
    low = lambda xm, w: _dot(xm, w[...].astype(BF16))
    up = lambda h, w: _dot(h.astype(BF16), w[...].astype(BF16))
    u = -(up(jnp.tanh(low(mix(0), w1_ref)), w2_ref) + w0_ref[...])
    softplus = jnp.maximum(u, 0.0) + jnp.log(1.0 + jnp.exp(-jnp.abs(u)))
    lw_ref[...] = -jnp.exp(-softplus - 0.5)
    a_ref[...] = jax.nn.sigmoid(up(low(mix(1), a1_ref), a2_ref) + a0_ref[...])
    g_ref[...] = up(jax.nn.sigmoid(low(mix(2), g1_ref)), g2_ref)


def _loras(x, xp, mu, w1, w2, w0, a1, a2, a0, g1, g2, *, tm_pref=320):
    m, k = x.shape
    n = w2.shape[1]
    tm = _tile(m, tm_pref)
    row = lambda i: (i, 0)
    fixed = lambda i: (0, 0)
    whole = lambda v: pl.BlockSpec(v.shape, fixed)
    out = jax.ShapeDtypeStruct((m, n), F32)
    return pl.pallas_call(
        _loras_kernel,
        grid=(m // tm,),
        in_specs=[pl.BlockSpec((tm, k), row), pl.BlockSpec((tm, k), row), whole(mu),
                  whole(w1), whole(w2), whole(w0), whole(a1), whole(a2), whole(a0), whole(g1), whole(g2)],
        out_specs=[pl.BlockSpec((tm, n), row)] * 3,
        out_shape=[out, out, out],
        compiler_params=_params("parallel"),
    )(x, xp, mu, w1, w2, w0, a1, a2, a0, g1, g2)


def _mm_ln_kernel(z_ref, w_ref, res_ref, g_ref, b_ref, o_ref, *, alpha):
    kk = pl.program_id(1)
    part = _dot(z_ref[...].astype(BF16), w_ref[...].astype(BF16))

    @pl.when(kk == 0)
    def _():
        o_ref[...] = part

    @pl.when(kk > 0)
    def _():
        o_ref[...] += part

    @pl.when(kk == pl.num_programs(1) - 1)
    def _():
        s = alpha * res_ref[...] + o_ref[...]
        o_ref[...] = _layer_norm_rows(s, g_ref[...], b_ref[...], LN_EPS)


def _mm_ln_cols_kernel(z_ref, w_ref, res_ref, g_ref, b_ref, o_ref, *, alpha, tn):
    j = pl.program_id(1)
    col = pl.multiple_of(j * tn, tn)
    o_ref[:, pl.ds(col, tn)] = _dot(z_ref[...], w_ref[...].astype(BF16))

    @pl.when(j == pl.num_programs(1) - 1)
    def _():
        s = alpha * res_ref[...] + o_ref[...]
        o_ref[...] = _layer_norm_rows(s, g_ref[...], b_ref[...], LN_EPS)


def _mm_ln_cols(z, w, layer, res, g, b, alpha, *, tm_pref=832, tn_pref=512):
    m, k = z.shape
    n = w.shape[2]
    tm = _tile(m, tm_pref)
    tn = _tile(n, tn_pref, 128)
    return pl.pallas_call(
        functools.partial(_mm_ln_cols_kernel, alpha=alpha, tn=tn),
        grid=(m // tm, n // tn),
        in_specs=[pl.BlockSpec((tm, k), lambda i, j: (i, 0)),
                  pl.BlockSpec((None, k, tn), lambda i, j: (layer, 0, j)),
                  pl.BlockSpec((tm, n), lambda i, j: (i, 0)),
                  pl.BlockSpec((1, n), lambda i, j: (0, 0)),
                  pl.BlockSpec((1, n), lambda i, j: (0, 0))],
        out_specs=pl.BlockSpec((tm, n), lambda i, j: (i, 0)),
        out_shape=jax.ShapeDtypeStruct((m, n), F32),
        compiler_params=_params("parallel", "arbitrary"),
    )(z, w, res, g, b)


def _mm_ln(z, w, layer, res, g, b, alpha, *, tm_pref=640, tk_pref=1024):
    m, k = z.shape
    n = w.shape[2]
    tm = _tile(m, tm_pref)
    tk = _tile(k, tk_pref, 128)
    return pl.pallas_call(
        functools.partial(_mm_ln_kernel, alpha=alpha),
        grid=(m // tm, k // tk),
        in_specs=[pl.BlockSpec((tm, tk), lambda i, kk: (i, kk)),
                  pl.BlockSpec((None, tk, n), lambda i, kk: (layer, kk, 0)),
                  pl.BlockSpec((tm, n), lambda i, kk: (i, 0)),
                  pl.BlockSpec((1, n), lambda i, kk: (0, 0)),
                  pl.BlockSpec((1, n), lambda i, kk: (0, 0))],
        out_specs=pl.BlockSpec((tm, n), lambda i, kk: (i, 0)),
        out_shape=jax.ShapeDtypeStruct((m, n), F32),
        compiler_params=_params("parallel", "arbitrary"),
    )(z, w, res, g, b)


def _cmp_bias_kernel(pe_ref, w1_ref, o_ref):
    acc = jnp.zeros((1, HEAD_DIM), F32)
    for rj in range(CMP_BLOCK):
        acc = acc + _dot(pe_ref[rj:rj + 1, :], w1_ref[rj], HI)
    o_ref[...] = acc


def _cmp_bias(pe, w1):
    return pl.pallas_call(
        _cmp_bias_kernel,
        grid=(2,),
        in_specs=[pl.BlockSpec((None, CMP_BLOCK, HEAD_DIM), lambda p: (p, 0, 0)),
                  pl.BlockSpec((None, CMP_BLOCK, HEAD_DIM, HEAD_DIM), lambda p: (p, 0, 0, 0))],
        out_specs=pl.BlockSpec((None, 1, HEAD_DIM), lambda p: (p, 0, 0)),
        out_shape=jax.ShapeDtypeStruct((2, 1, HEAD_DIM), F32),
        compiler_params=_params("parallel"),
    )(pe, w1)


def _cmp_kernel(*refs, n_in):
    pages = refs[-5 - n_in:-5]
    wc_ref, bias_ref, w2_ref, out_ref, carry_ref = refs[-5:]
    mk = CHUNKS_PER_PAGE * n_in
    rows = N_KV_HEADS * mk

    @pl.when(pl.program_id(1) == 0)
    def _():
        carry_ref[...] = jnp.zeros_like(carry_ref)

    first = (lax.broadcasted_iota(jnp.int32, (rows, 1), 0) % mk) == 0
    if len(pages[0].shape) == 3:
        split = [pages[i][...].reshape(CHUNKS_PER_PAGE, CMP_STRIDE, 2 * N_KV_HEADS, HEAD_DIM)
                 for i in range(n_in)]
        by_slot = [[jnp.swapaxes(split[i][:, j], 0, 1) for j in range(CMP_STRIDE)] for i in range(n_in)]
        piece = lambda i, j, slot: by_slot[i][j][slot]
    else:
        r_out = lax.broadcasted_iota(jnp.int32, (PAGE_SIZE, PAGE_SIZE), 0)
        r_in = lax.broadcasted_iota(jnp.int32, (PAGE_SIZE, PAGE_SIZE), 1)
        perm = jnp.where(r_in == (r_out % CHUNKS_PER_PAGE) * CMP_STRIDE + r_out // CHUNKS_PER_PAGE, 1.0, 0.0)
        grouped = [_dot(perm.astype(BF16), pages[i][...].astype(BF16)) for i in range(n_in)]
        piece = lambda i, j, slot: grouped[i][j * CHUNKS_PER_PAGE:(j + 1) * CHUNKS_PER_PAGE,
                                              slot * HEAD_DIM:(slot + 1) * HEAD_DIM]
    for p in range(2):
        lhs = jnp.concatenate(
            [jnp.concatenate([piece(i, j, p * N_KV_HEADS + k).astype(BF16) for j in range(CMP_STRIDE)], axis=1)
             for k in range(N_KV_HEADS) for i in range(n_in)], axis=0)
        acc = _dot(lhs, wc_ref[p])
        part0 = acc[:, :HEAD_DIM]
        part1 = acc[:, HEAD_DIM:]
        prev = jnp.where(first, carry_ref[p], pltpu.roll(part0, 1, 0))
        carry_ref[p] = pltpu.roll(part0, rows - (mk - 1), 0)
        h = prev + part1 + bias_ref[p]
        c = _dot(jax.nn.gelu(h).astype(BF16), w2_ref[p])
        for k in range(N_KV_HEADS):
            col = (p * N_KV_HEADS + k) * HEAD_DIM
            out_ref[:, col:col + HEAD_DIM] = c[k * mk:(k + 1) * mk]


def _cmp_weights(w1, w2):
    n_r = CMP_BLOCK // CMP_STRIDE
    w1r = w1.reshape(2, n_r, CMP_STRIDE, HEAD_DIM, HEAD_DIM)
    wc = jnp.concatenate([w1r[:, r] for r in range(n_r)], axis=-1).astype(BF16)
    return wc.reshape(2, CMP_STRIDE * HEAD_DIM, 2 * HEAD_DIM), w2.astype(BF16)


def _cmp_call(n_seq, n_steps, n_in, page_specs, operands, wc, bias, w2b, num_prefetch, prefetch):
    mk = CHUNKS_PER_PAGE * n_in
    if num_prefetch:
        fixed3 = lambda s, t, pt: (0, 0, 0)
        out_map = lambda s, t, pt: (s, t, 0)
    else:
        fixed3 = lambda s, t: (0, 0, 0)
        out_map = lambda s, t: (s, t, 0)
    grid_spec = pltpu.PrefetchScalarGridSpec(
        num_scalar_prefetch=num_prefetch,
        grid=(n_seq, n_steps),
        in_specs=page_specs + [
            pl.BlockSpec((2, CMP_STRIDE * HEAD_DIM, 2 * HEAD_DIM), fixed3),
            pl.BlockSpec((2, 1, HEAD_DIM), fixed3),
            pl.BlockSpec((2, HEAD_DIM, HEAD_DIM), fixed3)],
        out_specs=pl.BlockSpec((None, mk, 2 * NKV_COLS), out_map),
        scratch_shapes=[pltpu.VMEM((2, N_KV_HEADS * mk, HEAD_DIM), F32)],
    )
    return pl.pallas_call(
        functools.partial(_cmp_kernel, n_in=n_in),
        grid_spec=grid_spec,
        out_shape=jax.ShapeDtypeStruct((n_seq, n_steps * mk, 2 * NKV_COLS), F32),
        compiler_params=_params("parallel", "arbitrary"),
    )(*prefetch, *operands, wc, bias, w2b)


CMP_PAGES_PER_STEP = 8


def _cmp_prompt(y_main, bsz, t, wc, bias, w2b):
    pages_per_seq = t // PAGE_SIZE
    n_in = math.gcd(pages_per_seq, CMP_PAGES_PER_STEP)
    n_steps = pages_per_seq // n_in
    col_blk = NQ_COLS // (2 * NKV_COLS)
    specs = [pl.BlockSpec((PAGE_SIZE, 2 * NKV_COLS),
                          functools.partial(lambda s, st, i: (s * pages_per_seq + st * n_in + i, col_blk), i=i))
             for i in range(n_in)]
    return _cmp_call(bsz, n_steps, n_in, specs, [y_main] * n_in, wc, bias, w2b, 0, ())


def _cmp_sample(cache5, layer, page_table, wc, bias, w2b):
    n_in = math.gcd(page_table.shape[1], CMP_PAGES_PER_STEP)
    db, n_pages = page_table.shape
    n_steps = n_pages // n_in
    specs = [pl.BlockSpec((None, None, PAGE_SIZE, 2 * N_KV_HEADS, HEAD_DIM),
                          functools.partial(lambda s, st, pt, i: (layer, pt[s, st * n_in + i], 0, 0, 0), i=i))
             for i in range(n_in)]
    return _cmp_call(db, n_steps, n_in, specs, [cache5] * n_in, wc, bias, w2b, 1, (page_table,))


def _overlap_shifted(n_rows, n_c, n_s, n_cols):
    m = np.arange(n_rows)[:, None]
    j = np.arange(n_cols)[None, :]
    c0 = (m - 1) * CMP_STRIDE
    s0 = j * SEL_BLOCK
    ov = (c0 < s0 + SEL_BLOCK) & (c0 + CMP_BLOCK > s0) & (m >= 1) & (m <= n_c) & (j < n_s)
    return jnp.asarray(ov.astype(np.float32))


def _select_blocks(imp, qpos, n_s, n_cols):
    j = lax.broadcasted_iota(jnp.int32, (1, n_cols), 1)
    jf = j.astype(F32)
    cur = qpos // SEL_BLOCK
    forced = (j == 0) | (j == cur) | (j == cur - 1)
    valid = j * SEL_BLOCK <= qpos
    score = jnp.where(forced, FORCE, jnp.where(valid, imp, -FORCE))
    score = jnp.where(j < n_s, score, -jnp.inf)
    sel = jnp.zeros(score.shape, F32)
    for _ in range(min(N_SEL, n_s)):
        mx = jnp.max(score, axis=-1, keepdims=True)
        idx = jnp.min(jnp.where(score == mx, jf, float(n_cols)), axis=-1, keepdims=True)
        hit = jf == idx
        sel = jnp.where(hit, 1.0, sel)
        score = jnp.where(hit, -jnp.inf, score)
    return sel


SEL_KEY_TILE = 512


def _softmax_cols(s, mask):
    s = jnp.where(mask, s, NEG_INF)
    m = jnp.max(s, axis=0, keepdims=True)
    p = jnp.where(mask, jnp.exp2(s - m), 0.0)
    return p * (1.0 / jnp.maximum(jnp.sum(p, axis=0, keepdims=True), TINY))


def _select_blocks_cols(imp, qpos, n_s):
    j = lax.broadcasted_iota(jnp.int32, (n_s, 1), 0)
    cur = qpos // SEL_BLOCK
    forced = (j == 0) | (j == cur) | (j == cur - 1)
    valid = j * SEL_BLOCK <= qpos
    score = jnp.where(forced, FORCE, jnp.where(valid, imp, -FORCE))
    sub = 8
    blocks = [score[b * sub:(b + 1) * sub] for b in range(n_s // sub)]
    ranks = [jnp.zeros(blk.shape, F32) for blk in blocks]
    row = lax.broadcasted_iota(jnp.int32, (sub, 1), 0)
    for i in range(n_s):
        bi, ri = divmod(i, sub)
        si = blocks[bi][ri:ri + 1, :]
        for b, blk in enumerate(blocks):
            if b < bi:
                beats = si > blk
            elif b > bi:
                beats = si >= blk
            else:
                beats = jnp.where(row > ri, jnp.where(si >= blk, 1.0, 0.0), jnp.where(si > blk, 1.0, 0.0)) > 0.5
            ranks[b] = ranks[b] + jnp.where(beats, 1.0, 0.0)
    rank = jnp.concatenate(ranks, axis=0)
    return jnp.where(rank < float(min(N_SEL, n_s)), 1.0, 0.0)


def _nsa_prompt_kernel(q_ref, kc_ref, vct_ref, ks_ref, vst_ref, kw_ref, vwt_ref, g_ref, ovt_ref, o_ref,
                       selbias_ref, *, t):
    start = pl.program_id(2) * Q_BLOCK
    n_cols = GROUP * Q_BLOCK
    q = q_ref[...] * (HEAD_DIM ** -0.5 * math.log2(math.e))
    q4 = jnp.concatenate([q[:, g * HEAD_DIM:(g + 1) * HEAD_DIM] for g in range(GROUP)], axis=0).astype(BF16)
    qpos1 = start + lax.broadcasted_iota(jnp.int32, (1, Q_BLOCK), 1)
    qpos4 = start + lax.broadcasted_iota(jnp.int32, (1, n_cols), 1) % Q_BLOCK

    n_cr = kc_ref.shape[0]
    n_s = ovt_ref.shape[0]
    s = _dot_nt(kc_ref[...].astype(BF16), q4)
    m_idx = lax.broadcasted_iota(jnp.int32, (n_cr, 1), 0)
    cmask = (m_idx >= 1) & (m_idx * CMP_STRIDE + (CMP_BLOCK - CMP_STRIDE - 1) <= qpos4)
    p_c = _softmax_cols(s, cmask)
    o_c = _dot(vct_ref[...].astype(BF16), p_c.astype(BF16))
    psum = p_c[:, 0:Q_BLOCK]
    for g in range(1, GROUP):
        psum = psum + p_c[:, g * Q_BLOCK:(g + 1) * Q_BLOCK]
    imp = _dot(ovt_ref[...], psum, HI)
    sel = _select_blocks_cols(imp, qpos1, n_s)

    wlen = WINDOW + Q_BLOCK
    w0 = pl.multiple_of(jnp.clip(start - WINDOW, 0, t - wlen), PAGE_SIZE)
    sw = _dot_nt(kw_ref[pl.ds(w0, wlen), :], q4)
    dpos = qpos4 - (w0 + lax.broadcasted_iota(jnp.int32, (wlen, 1), 0))
    p_w = _softmax_cols(sw, (dpos >= 0) & (dpos < WINDOW))
    wblk = w0 // PAGE_SIZE
    vwt = jnp.concatenate([vwt_ref[wblk + i] for i in range(wlen // PAGE_SIZE)], axis=1)
    o_w = _dot(vwt, p_w.astype(BF16))

    tk = SEL_KEY_TILE
    blocks_per_tile = tk // SEL_BLOCK
    selbias_ref[...] = jnp.where(sel > 0.5, 0.0, NEG_INF)

    def scores(kt):
        return _dot_nt(ks_ref[pl.ds(pl.multiple_of(kt * tk, tk), tk), :], q4)

    def tile(kt, carry, last):
        m, l, acc, sc = carry
        sc_next = sc if last else scores(kt + 1)
        off = kt * tk
        rows = selbias_ref[pl.ds(pl.multiple_of(kt * blocks_per_tile, blocks_per_tile), blocks_per_tile), :]
        bias = jnp.concatenate([jnp.broadcast_to(rows[i:i + 1, :], (SEL_BLOCK, Q_BLOCK))
                                for i in range(blocks_per_tile)], axis=0)
        if last:
            kpos = off + lax.broadcasted_iota(jnp.int32, (tk, Q_BLOCK), 0)
            bias = jnp.where(kpos <= qpos1, bias, NEG_INF)
        sc = sc + jnp.concatenate([bias] * GROUP, axis=1)
        m_new = jnp.maximum(m, jnp.max(sc, axis=0, keepdims=True))
        alpha = jnp.exp2(m - m_new)
        p = jnp.exp2(sc - m_new)
        l = alpha * l + jnp.sum(p, axis=0, keepdims=True)
        vblk = kt * (tk // PAGE_SIZE)
        vst = jnp.concatenate([vst_ref[vblk + i] for i in range(tk // PAGE_SIZE)], axis=1)
        acc = alpha * acc + _dot(vst, p.astype(BF16))
        return m_new, l, acc, sc_next

    n_tiles = (start + Q_BLOCK + tk - 1) // tk
    m0 = jnp.full((1, n_cols), NEG_INF, F32)
    l0 = jnp.zeros((1, n_cols), F32)
    a0 = jnp.zeros((HEAD_DIM, n_cols), F32)
    carry = lax.fori_loop(0, n_tiles - 1, functools.partial(tile, last=False), (m0, l0, a0, scores(0)))
    _, l, acc, _ = tile(n_tiles - 1, carry, last=True)
    o_s = acc * (1.0 / jnp.maximum(l, TINY))

    gt = g_ref[...]
    o = gt[0:1, :] * o_c + gt[1:2, :] * o_s + gt[2:3, :] * o_w
    o_ref[...] = jnp.concatenate([o[:, g * Q_BLOCK:(g + 1) * Q_BLOCK].T for g in range(GROUP)],
                                 axis=1).astype(o_ref.dtype)


def _values_t(y_b, bsz, t, col0):
    v = lax.slice(y_b, (0, col0), (bsz * t, col0 + NKV_COLS))
    v = v.reshape(bsz, t // PAGE_SIZE, PAGE_SIZE, N_KV_HEADS, HEAD_DIM)
    return v.transpose(0, 3, 1, 4, 2)


def _nsa_prompt(y_main, y_b, cmp_p, gates, bsz, t, n_tok):
    assert t % SEL_KEY_TILE == 0 and t >= WINDOW + Q_BLOCK and SEL_KEY_TILE % PAGE_SIZE == 0
    np_tok = bsz * t
    nq = t // Q_BLOCK
    n_cr = t // CMP_STRIDE
    n_c = (t - CMP_BLOCK) // CMP_STRIDE + 1
    n_s = t // SEL_BLOCK
    assert n_s >= N_SEL and n_s % 8 == 0
    ovt = _overlap_shifted(n_cr, n_c, n_s, n_s).T
    n_pg = t // PAGE_SIZE
    v_sel_t = _values_t(y_b, bsz, t, NQ_COLS + 3 * NKV_COLS)
    v_win_t = _values_t(y_b, bsz, t, NQ_COLS + 5 * NKV_COLS)
    cmp_t = cmp_p.transpose(0, 2, 1)
    cb = NQ_COLS // HEAD_DIM
    g_t = gates[:np_tok].reshape(bsz, nq, Q_BLOCK, N_KV_HEADS, GROUP, 3).transpose(0, 3, 1, 5, 4, 2)
    g_t = g_t.reshape(bsz, N_KV_HEADS, nq, 3, GROUP * Q_BLOCK)
    seq_key = lambda c4: pl.BlockSpec((t, HEAD_DIM), lambda b, k, i: (b, cb + c4 * N_KV_HEADS + k))
    seq_val = pl.BlockSpec((None, None, n_pg, HEAD_DIM, PAGE_SIZE), lambda b, k, i: (b, k, 0, 0, 0))
    return pl.pallas_call(
        functools.partial(_nsa_prompt_kernel, t=t),
        grid=(bsz, N_KV_HEADS, nq),
        in_specs=[pl.BlockSpec((Q_BLOCK, QW), lambda b, k, i: (b * nq + i, k)),
                  pl.BlockSpec((None, n_cr, HEAD_DIM), lambda b, k, i: (b, 0, k)),
                  pl.BlockSpec((None, HEAD_DIM, n_cr), lambda b, k, i: (b, N_KV_HEADS + k, 0)),
                  seq_key(2), seq_val, seq_key(4), seq_val,
                  pl.BlockSpec((None, None, None, 3, GROUP * Q_BLOCK), lambda b, k, i: (b, k, i, 0, 0)),
                  pl.BlockSpec((n_s, n_cr), lambda b, k, i: (0, 0))],
        out_specs=pl.BlockSpec((Q_BLOCK, QW), lambda b, k, i: (b * nq + i, k)),
        out_shape=jax.ShapeDtypeStruct((n_tok, NQ_COLS), BF16),
        scratch_shapes=[pltpu.VMEM((n_s, Q_BLOCK), F32)],
        compiler_params=_params("parallel", "parallel", "arbitrary"),
    )(y_main, cmp_p, cmp_t, y_b, v_sel_t, y_b, v_win_t, g_t, ovt)


SAMPLE_ROWS = 8


def _nsa_sample_kernel(*refs, past, s_len, n_s, n_in):
    y_ref, cmp_ref = refs[1:3]
    pages = refs[3:3 + n_in]
    win_ref, g_ref, ov_ref, o_ref, q_scr, sel_scr, m_scr, l_scr, acc_scr, oc_scr = refs[3 + n_in:]
    pg = pl.program_id(1)
    n_pages = pl.num_programs(1)
    sr = SAMPLE_ROWS
    rk = GROUP * sr
    scale = HEAD_DIM ** -0.5
    tpos = lax.broadcasted_iota(jnp.int32, (rk, 1), 0) % sr
    qpos = past + tpos
    n_sc = sel_scr.shape[1]
    nbp = past // SEL_BLOCK
    kv0 = NQ_COLS

    @pl.when(pg == 0)
    def _():
        y = y_ref[...]
        n_cr = cmp_ref.shape[0]
        m_idx = lax.broadcasted_iota(jnp.int32, (1, n_cr), 1)
        cmask = (m_idx >= 1) & (m_idx * CMP_STRIDE + (CMP_BLOCK - CMP_STRIDE - 1) <= qpos)
        psums = []
        for k in range(N_KV_HEADS):
            qk = jnp.concatenate([y[:, (k * GROUP + g) * HEAD_DIM:(k * GROUP + g + 1) * HEAD_DIM]
                                  for g in range(GROUP)], axis=0) * scale
            qk = qk.astype(BF16)
            q_scr[k] = qk
            kc = cmp_ref[:, k * HEAD_DIM:(k + 1) * HEAD_DIM].astype(BF16)
            vc = cmp_ref[:, (N_KV_HEADS + k) * HEAD_DIM:(N_KV_HEADS + k + 1) * HEAD_DIM].astype(BF16)
            p_c = _masked_softmax(_dot_nt(qk, kc), cmask)
            oc_scr[k * rk:(k + 1) * rk, :] = _dot(p_c.astype(BF16), vc)
            psum = p_c[0:sr]
            for g in range(1, GROUP):
                psum = psum + p_c[g * sr:(g + 1) * sr]
            psums.append(psum)
        imp = _dot(jnp.concatenate(psums, axis=0), ov_ref[...], HI)
        qpos_sel = past + lax.broadcasted_iota(jnp.int32, (N_KV_HEADS * sr, 1), 0) % sr
        sel = _select_blocks(imp, qpos_sel, n_s, n_sc)
        for k in range(N_KV_HEADS):
            for g in range(GROUP):
                sel_scr[k * rk + g * sr:k * rk + (g + 1) * sr, :] = sel[k * sr:(k + 1) * sr]
        m_scr[...] = jnp.full(m_scr.shape, NEG_INF, F32)
        l_scr[...] = jnp.zeros(l_scr.shape, F32)
        acc_scr[...] = jnp.zeros(acc_scr.shape, F32)

    def online_update(k, sc, mask, v):
        rows = slice(k * rk, (k + 1) * rk)
        sc = jnp.where(mask, sc, NEG_INF)
        m_old = m_scr[rows, :]
        m_new = jnp.maximum(m_old, jnp.max(sc, axis=-1, keepdims=True))
        alpha = jnp.exp(m_old - m_new)
        p = jnp.where(mask, jnp.exp(sc - m_new), 0.0)
        l_scr[rows, :] = alpha * l_scr[rows, :] + jnp.sum(p, axis=-1, keepdims=True)
        acc_scr[rows, :] = alpha * acc_scr[rows, :] + _dot(p.astype(v.dtype), v)
        m_scr[rows, :] = m_new

    n_keys = n_in * PAGE_SIZE
    kpos = pg * n_keys + lax.broadcasted_iota(jnp.int32, (1, n_keys), 1)
    blk_row = lax.broadcasted_iota(jnp.int32, (n_sc, 1), 0)
    expand = jnp.where(blk_row == kpos // SEL_BLOCK, 1.0, 0.0).astype(BF16)
    selx = _dot(sel_scr[...].astype(BF16), expand)
    by_slot = [jnp.swapaxes(pages[i][...], 0, 1).astype(BF16) for i in range(n_in)]
    for k in range(N_KV_HEADS):
        kp = jnp.concatenate([by_slot[i][k] for i in range(n_in)], axis=0)
        vp = jnp.concatenate([by_slot[i][N_KV_HEADS + k] for i in range(n_in)], axis=0)
        mask = (selx[k * rk:(k + 1) * rk] > 0.5) & (kpos <= qpos)
        online_update(k, _dot_nt(q_scr[k], kp), mask, vp)

    @pl.when(pg == n_pages - 1)
    def _():
        y = y_ref[...]
        rpos = lax.broadcasted_iota(jnp.int32, (1, sr), 1)
        new_ok = (rpos < s_len) & (rpos <= tpos)
        gt = g_ref[...]
        win_by_slot = jnp.swapaxes(win_ref[...], 0, 1).astype(BF16)
        for k in range(N_KV_HEADS):
            rows = slice(k * rk, (k + 1) * rk)
            qk = q_scr[k]
            qf = qk.astype(F32)
            c_ks = kv0 + (2 * N_KV_HEADS + k) * HEAD_DIM
            c_vs = kv0 + (3 * N_KV_HEADS + k) * HEAD_DIM
            kn = y[:, c_ks:c_ks + HEAD_DIM]
            vn = y[:, c_vs:c_vs + HEAD_DIM]
            new_sel = sel_scr[rows, nbp:nbp + 1] > 0.5
            online_update(k, _dot_nt(qf, kn), new_sel & new_ok, vn)
            o_s = acc_scr[rows, :] / jnp.maximum(l_scr[rows, :], TINY)
            wb = win_ref.shape[0]
            c_kw = kv0 + (4 * N_KV_HEADS + k) * HEAD_DIM
            c_vw = kv0 + (5 * N_KV_HEADS + k) * HEAD_DIM
            kwb = win_by_slot[k]
            vwb = win_by_slot[N_KV_HEADS + k]
            kwn = y[:, c_kw:c_kw + HEAD_DIM]
            vwn = y[:, c_vw:c_vw + HEAD_DIM]
            dpos = qpos - (past - wb + lax.broadcasted_iota(jnp.int32, (1, wb), 1))
            mask_b = (dpos >= 0) & (dpos < WINDOW)
            s_b = jnp.where(mask_b, _dot_nt(qk, kwb), NEG_INF)
            s_n = jnp.where(new_ok, _dot_nt(qf, kwn), NEG_INF)
            mw = jnp.maximum(jnp.max(s_b, axis=-1, keepdims=True), jnp.max(s_n, axis=-1, keepdims=True))
            p_b = jnp.where(mask_b, jnp.exp(s_b - mw), 0.0)
            p_n = jnp.where(new_ok, jnp.exp(s_n - mw), 0.0)
            den = jnp.sum(p_b, axis=-1, keepdims=True) + jnp.sum(p_n, axis=-1, keepdims=True)
            o_w = (_dot(p_b.astype(BF16), vwb) + _dot(p_n, vwn)) / jnp.maximum(den, TINY)
            o_c = oc_scr[rows, :]
            for g in range(GROUP):
                r0, r1 = g * sr, (g + 1) * sr
                c = (k * GROUP + g) * 3
                o = gt[:, c:c + 1] * o_c[r0:r1] + gt[:, c + 1:c + 2] * o_s[r0:r1] + gt[:, c + 2:c + 3] * o_w[r0:r1]
                o_ref[:, (k * GROUP + g) * HEAD_DIM:(k * GROUP + g + 1) * HEAD_DIM] = o


SAMPLE_PAGES_PER_STEP = 8


def _nsa_sample(ys8, cmp_s, cache5, layer, win4, gates8, page_table, s_len):
    db, n_pages = page_table.shape
    past = n_pages * PAGE_SIZE
    n_in = math.gcd(n_pages, SAMPLE_PAGES_PER_STEP)
    assert s_len <= SAMPLE_ROWS and s_len <= SEL_BLOCK and past % SEL_BLOCK == 0
    n_cr = cmp_s.shape[1]
    n_c = (past + s_len - CMP_BLOCK) // CMP_STRIDE + 1
    assert n_c + 1 <= n_cr
    n_s = -(-(past + s_len) // SEL_BLOCK)
    n_sc = -(-n_s // 128) * 128
    ov = _overlap_shifted(n_cr, n_c, n_s, n_sc)
    wb = win4.shape[1]
    ncol = ys8.shape[2]
    rows = N_KV_HEADS * GROUP * SAMPLE_ROWS
    page_specs = [pl.BlockSpec((None, None, PAGE_SIZE, 2 * N_KV_HEADS, HEAD_DIM),
                               functools.partial(lambda b, p, pt, i: (layer, pt[b, p * n_in + i], 0, 1, 0), i=i))
                  for i in range(n_in)]
    grid_spec = pltpu.PrefetchScalarGridSpec(
        num_scalar_prefetch=1,
        grid=(db, n_pages // n_in),
        in_specs=[pl.BlockSpec((None, SAMPLE_ROWS, ncol), lambda b, p, pt: (b, 0, 0)),
                  pl.BlockSpec((None, n_cr, 2 * NKV_COLS), lambda b, p, pt: (b, 0, 0))]
                 + page_specs
                 + [pl.BlockSpec((None, wb, 2 * N_KV_HEADS, HEAD_DIM), lambda b, p, pt: (b, 0, 0, 0)),
                    pl.BlockSpec((None, SAMPLE_ROWS, 3 * N_HEADS), lambda b, p, pt: (b, 0, 0)),
                    pl.BlockSpec((n_cr, n_sc), lambda b, p, pt: (0, 0))],
        out_specs=pl.BlockSpec((None, SAMPLE_ROWS, NQ_COLS), lambda b, p, pt: (b, 0, 0)),
        scratch_shapes=[pltpu.VMEM((N_KV_HEADS, GROUP * SAMPLE_ROWS, HEAD_DIM), BF16),
                        pltpu.VMEM((rows, n_sc), F32),
                        pltpu.VMEM((rows, 1), F32),
                        pltpu.VMEM((rows, 1), F32),
                        pltpu.VMEM((rows, HEAD_DIM), F32),
                        pltpu.VMEM((rows, HEAD_DIM), F32)],
    )
    return pl.pallas_call(
        functools.partial(_nsa_sample_kernel, past=past, s_len=s_len, n_s=n_s, n_in=n_in),
        grid_spec=grid_spec,
        out_shape=jax.ShapeDtypeStruct((db, SAMPLE_ROWS, NQ_COLS), F32),
        compiler_params=_params("parallel", "arbitrary"),
    )(page_table, ys8, cmp_s, *([cache5] * n_in), win4, gates8, ov)


WKV_DIAG_BLOCK = 16
WKV_CHUNK = 64


def _bdot(a, b):
    return _dot(a.astype(BF16), b.astype(BF16))


def _bdot_nt(a, b):
    return _dot_nt(a.astype(BF16), b.astype(BF16))


def _bdot_tn(a, b):
    return _dot_tn(a.astype(BF16), b.astype(BF16))


def _unit_lower_inverses(mats, c):
    blk = min(WKV_DIAG_BLOCK, c)
    ri = lax.broadcasted_iota(jnp.int32, (c, c), 0)
    ci = lax.broadcasted_iota(jnp.int32, (c, c), 1)
    eye = jnp.where(ri == ci, 1.0, 0.0)
    same = (ri // blk) == (ci // blk)
    d = [jnp.where(same, a, 0.0) for a in mats]
    td = [eye + x for x in d]
    n_sq = int(math.log2(blk)) - 1
    pw = [_bdot(x, x) for x in d]
    stack = c >= WKV_DIAG_BLOCK
    for level in range(n_sq):
        if level + 1 < n_sq and stack:
            both = [_bdot(jnp.concatenate([p, t], axis=0), p) for p, t in zip(pw, td)]
            pw = [x[:c] for x in both]
            td = [t + x[c:] for t, x in zip(td, both)]
        else:
            td = [t + _bdot(t, p) for t, p in zip(td, pw)]
            if level + 1 < n_sq:
                pw = [_bdot(p, p) for p in pw]
    if blk == c:
        return td
    npow = [_bdot(t, jnp.where(same, 0.0, a)) for t, a in zip(td, mats)]
    tinv = [t + _bdot(x, t) for x, t in zip(npow, td)]
    for _ in range(int(math.log2(c // blk)) - 1):
        npow = [_bdot(x, x) for x in npow]
        tinv = [t + _bdot(x, t) for x, t in zip(npow, tinv)]
    return tinv


def _wkv_kernel(r_ref, k_ref, v_ref, lw_ref, a_ref, g_ref, kk_ref, ka_ref, rk_ref, lg_ref, lb_ref, s0_ref,
                z_ref, s_ref, *, c, nc, hb):
    n = RWKV_HEAD

    @pl.when(pl.program_id(2) == 0)
    def _():
        s_ref[...] = s0_ref[...]

    ri = lax.broadcasted_iota(jnp.int32, (c, c), 0)
    ci = lax.broadcasted_iota(jnp.int32, (c, c), 1)
    strict = ri > ci
    incl = ri >= ci
    tri = jnp.where(incl, 1.0, 0.0)
    lw_all = lw_ref[...]
    cum = jnp.concatenate([_dot(tri, lw_all[i * c:(i + 1) * c], HI) for i in range(nc)], axis=0)
    e_in_all = jnp.exp(cum)
    e_neg_all = jnp.exp(-cum)
    e_ex_all = jnp.exp(cum - lw_all)
    heads = range(hb)
    chunks = range(nc)
    sls = [slice(hh * n, (hh + 1) * n) for hh in heads]
    rws = [slice(i * c, (i + 1) * c) for i in chunks]
    r, v, k2, r_t, a_t, b_t, k_t, vb = ([] for _ in range(8))
    for sl in sls:
        k = k_ref[:, sl]
        a = a_ref[:, sl]
        kk = k * kk_ref[:, sl]
        kk = kk * (1.0 / jnp.maximum(jnp.sqrt(jnp.sum(kk * kk, axis=-1, keepdims=True)), 1e-12))
        r.append(r_ref[:, sl])
        v.append(v_ref[:, sl])
        k2.append(k * (1.0 + (a - 1.0) * ka_ref[:, sl]))
        r_t.append(r[-1] * e_in_all[:, sl])
        a_t.append(-kk * e_ex_all[:, sl])
        b_t.append(kk * a * e_neg_all[:, sl])
        k_t.append(k2[-1] * e_neg_all[:, sl])
        vb.append(v[-1].astype(BF16))

    items = [(hh, i) for hh in heads for i in chunks]
    aa = [_bdot_nt(jnp.concatenate([a_t[hh][rws[i]], r_t[hh][rws[i]]], axis=0),
                   jnp.concatenate([b_t[hh][rws[i]], k_t[hh][rws[i]]], axis=0)) for hh, i in items]
    r2 = lax.broadcasted_iota(jnp.int32, (2 * c, c), 0)
    c2 = lax.broadcasted_iota(jnp.int32, (2 * c, c), 1)
    tri2 = jnp.where(r2 < c, r2, r2 - c + 1) > c2
    a_ab = [jnp.where(strict, x[:c, :c], 0.0) for x in aa]
    a_rb = [jnp.where(incl, x[c:, :c], 0.0).astype(BF16) for x in aa]
    xkv = [_bdot(jnp.where(tri2, x[:, c:], 0.0), vb[hh][rws[i]]) for x, (hh, i) in zip(aa, items)]
    yv = [x[c:] for x in xkv]
    w_c = [e_in_all[i * c + c - 1:(i + 1) * c, sls[hh]] for hh, i in items]
    kv = [_bdot_tn(vb[hh][rws[i]], k_t[hh][rws[i]] * w) for w, (hh, i) in zip(w_c, items)]
    bh = [(b_t[hh][rws[i]] * w).astype(BF16) for w, (hh, i) in zip(w_c, items)]
    tinv = _unit_lower_inverses(a_ab, c)
    wt = [_bdot(t, a_t[hh][rws[i]]) for t, (hh, i) in zip(tinv, items)]
    u = [_bdot(t, x[:c]) for t, x in zip(tinv, xkv)]
    wr = [jnp.concatenate([w, r_t[hh][rws[i]]], axis=0).astype(BF16) for w, (hh, i) in zip(wt, items)]

    s = [s_ref[hh] for hh in heads]
    ys = [[] for _ in heads]
    for i in chunks:
        idx = [hh * nc + i for hh in heads]
        ls = [_dot_nt(wr[j], s[hh].astype(BF16)) for hh, j in zip(heads, idx)]
        p = [x[:c] + u[j] for x, j in zip(ls, idx)]
        for hh, j in zip(heads, idx):
            ys[hh].append(ls[hh][c:] + _bdot(a_rb[j], p[hh]) + yv[j])
        s = [s[hh] * w_c[j] + _bdot_tn(p[hh], bh[j]) + kv[j] for hh, j in zip(heads, idx)]
    zs = []
    for hh, sl in zip(heads, sls):
        s_ref[hh] = s[hh]
        y = ys[hh][0] if nc == 1 else jnp.concatenate(ys[hh], axis=0)
        mu = jnp.mean(y, axis=-1, keepdims=True)
        yc = y - mu
        yn = yc * lax.rsqrt(jnp.mean(yc * yc, axis=-1, keepdims=True) + LNX_EPS)
        yn = yn * lg_ref[:, sl] + lb_ref[:, sl]
        bonus = jnp.sum(r[hh] * k2[hh] * rk_ref[:, sl], axis=-1, keepdims=True) * v[hh]
        zs.append((yn + bonus) * g_ref[:, sl])
    z_ref[...] = jnp.concatenate(zs, axis=1).astype(z_ref.dtype)


def _wkv(rkv, lw, a, g, k_k, k_a, r_k, lnx_g, lnx_b, s0, n_seq, t, c, nc, hb):
    rows, d = lw.shape
    h = d // RWKV_HEAD
    hw = hb * RWKV_HEAD
    tb = nc * c
    nblk = t // tb
    tok = lambda s, hg, ch: (s * nblk + ch, hg)
    par = lambda s, hg, ch: (0, hg)
    st = lambda s, hg, ch: (s, hg, 0, 0)
    rkv_spec = lambda p: pl.BlockSpec((None, tb, hw), lambda s, hg, ch: (p, s * nblk + ch, hg))
    return pl.pallas_call(
        functools.partial(_wkv_kernel, c=c, nc=nc, hb=hb),
        grid=(n_seq, h // hb, nblk),
        in_specs=[rkv_spec(0), rkv_spec(1), rkv_spec(2),
                  pl.BlockSpec((tb, hw), tok), pl.BlockSpec((tb, hw), tok), pl.BlockSpec((tb, hw), tok),
                  pl.BlockSpec((1, hw), par), pl.BlockSpec((1, hw), par), pl.BlockSpec((1, hw), par),
                  pl.BlockSpec((1, hw), par), pl.BlockSpec((1, hw), par),
                  pl.BlockSpec((None, hb, RWKV_HEAD, RWKV_HEAD), st)],
        out_specs=[pl.BlockSpec((tb, hw), tok),
                   pl.BlockSpec((None, hb, RWKV_HEAD, RWKV_HEAD), st)],
        out_shape=[jax.ShapeDtypeStruct((rows, d), BF16),
                   jax.ShapeDtypeStruct((n_seq, h, RWKV_HEAD, RWKV_HEAD), F32)],
        compiler_params=_params("parallel", "parallel", "arbitrary"),
    )(rkv, rkv, rkv, lw, a, g, k_k, k_a, r_k, lnx_g, lnx_b, s0)


WKV_PROMPT_CHUNKS_PER_STEP = 2
WKV_PROMPT_HEADS_PER_STEP = 8


def _to_slots_kernel(x_ref, o_ref):
    x = x_ref[...]
    by_slot = jnp.stack([x[:, s * HEAD_DIM:(s + 1) * HEAD_DIM] for s in range(o_ref.shape[1])], axis=0)
    o_ref[...] = jnp.swapaxes(by_slot, 0, 1)


def _to_slots(y, rows, col0, n_slots, *, tm_pref=512):
    width = n_slots * HEAD_DIM
    assert col0 % width == 0
    tm = _tile(rows, tm_pref)
    return pl.pallas_call(
        _to_slots_kernel,
        grid=(rows // tm,),
        in_specs=[pl.BlockSpec((tm, width), lambda i: (i, col0 // width))],
        out_specs=pl.BlockSpec((tm, n_slots, HEAD_DIM), lambda i: (i, 0, 0)),
        out_shape=jax.ShapeDtypeStruct((rows, n_slots, HEAD_DIM), y.dtype),
        compiler_params=_params("parallel"),
    )(y)


def _nsa_layer(x, np_tok, bsz, t, db, s_len, cache5, layer, win, page_table, w_in, pe, w1, w2, w_out,
               ln_g, ln_b, alpha):
    n_main = NQ_COLS + 6 * NKV_COLS
    y_main, y_b = _mm(x, w_in, layer, n_main, out_dtypes=(F32, BF16), tm_pref=832, tn_pref=1024)
    w_gate = jnp.pad(w_in[layer:layer + 1, :, n_main:], ((0, 0), (0, 0), (0, 128 - 3 * N_HEADS)))
    gates = _mm(x, w_gate, 0, 128, act="sigmoid")[:, :3 * N_HEADS]

    wc, w2b = _cmp_weights(w1, w2)
    bias = _cmp_bias(pe, w1)
    cmp_p = _cmp_prompt(y_main, bsz, t, wc, bias, w2b)
    cmp_s = _cmp_sample(cache5, layer, page_table, wc, bias, w2b)

    o_p = _nsa_prompt(y_main, y_b, cmp_p, gates, bsz, t, x.shape[0])

    pad_rows = ((0, 0), (0, SAMPLE_ROWS - s_len), (0, 0))
    ys = y_main[np_tok:].reshape(db, s_len, n_main)
    ys8 = jnp.pad(ys, pad_rows)
    gates8 = jnp.pad(gates[np_tok:].reshape(db, s_len, 3 * N_HEADS), pad_rows)
    wb = win.shape[1]
    win4 = win.reshape(db, wb, 2 * N_KV_HEADS, HEAD_DIM)
    o_s = _nsa_sample(ys8, cmp_s, cache5, layer, win4, gates8, page_table, s_len)
    o_s = o_s[:, :s_len].reshape(db * s_len, NQ_COLS).astype(BF16)

    o = lax.dynamic_update_slice(o_p, o_s, (np_tok, 0))
    h = _mm_ln_cols(o, w_out, layer, x, ln_g, ln_b, alpha)

    kv0 = NQ_COLS
    kw0 = NQ_COLS + 4 * NKV_COLS
    kv_p = _to_slots(y_main, np_tok, kv0, 4 * N_KV_HEADS).reshape(bsz, t, 4, N_KV_HEADS, HEAD_DIM)
    wlen = min(WINDOW, t)
    win_p = jnp.stack([lax.slice(y_main, (b * t + t - wlen, kw0), ((b + 1) * t, n_main)) for b in range(bsz)])
    win_p = win_p.reshape(bsz, wlen, 2, N_KV_HEADS, HEAD_DIM)
    kv_s = ys[:, :, kv0:kw0].reshape(db, s_len, 4, N_KV_HEADS, HEAD_DIM)
    kvw_s = ys[:, :, kw0:].reshape(db, s_len, 2, N_KV_HEADS, HEAD_DIM)
    win_s = jnp.concatenate([win, kvw_s], axis=1)[:, s_len:]
    return h, kv_p, kv_s, win_p, win_s


def _rwkv_layer(x, np_tok, bsz, t, db, s_len, shift_s, wkv_s, mu, w_rkv, w0, w1, w2, a0, a1, a2, g1, g2,
                k_k, k_a, r_k, lnx_g, lnx_b, w_out, layer, ln_g, ln_b, alpha):
    d = x.shape[1]
    heads = d // RWKV_HEAD
    xprev = jnp.concatenate([jnp.zeros((1, d), x.dtype), x[:-1]], axis=0)
    xprev = xprev.at[np.arange(1, bsz) * t].set(0.0)
    xprev = xprev.at[np_tok + np.arange(db) * s_len].set(shift_s.astype(x.dtype))

    row = lambda v: v.reshape(1, -1)
    rkv = _mm_mix(x, xprev, mu[:3, None, :], w_rkv)
    lw, a, g = _loras(x, xprev, mu[3:6], w1, w2, row(w0), a1, a2, row(a0), g1, g2)

    pvec = (row(k_k), row(k_a), row(r_k), row(lnx_g), row(lnx_b))
    c_p = _tile(t, WKV_CHUNK, 8)
    nc_p = math.gcd(t // c_p, WKV_PROMPT_CHUNKS_PER_STEP)
    hb_p = math.gcd(heads, WKV_PROMPT_HEADS_PER_STEP)
    zero_state = jnp.zeros((bsz, heads, RWKV_HEAD, RWKV_HEAD), F32)
    z_p, st_p = _wkv(rkv, lw, a, g, *pvec, zero_state, bsz, t, c_p, nc_p, hb_p)

    c_s = -(-s_len // 8) * 8
    pad = lambda v: jnp.pad(v.reshape(v.shape[:-2] + (db, s_len, d)),
                            [(0, 0)] * (v.ndim - 1) + [(0, c_s - s_len), (0, 0)]
                            ).reshape(v.shape[:-2] + (db * c_s, d))
    z_s, st_s = _wkv(pad(rkv[:, np_tok:]), pad(lw[np_tok:]), pad(a[np_tok:]), pad(g[np_tok:]), *pvec,
                     wkv_s.astype(F32), db, c_s, c_s, 1, heads)
    z_s = z_s.reshape(db, c_s, d)[:, :s_len].reshape(db * s_len, d)

    z = lax.dynamic_update_slice(z_p, z_s, (np_tok, 0))
    h = _mm_ln_cols(z, w_out, layer, x, ln_g, ln_b, alpha)
    last_p = x[np.arange(1, bsz + 1) * t - 1]
    last_s = x[np_tok + np.arange(1, db + 1) * s_len - 1]
    return h, st_p, st_s, last_p, last_s


def _mlp(h, w_up, w_down, layer, ln_g, ln_b, alpha):
    d_ff = w_up.shape[2]
    u = _mm(h, w_up, layer, d_ff, act="relu2", out_dtypes=(BF16,))
    return _mm_ln(u, w_down, layer, h, ln_g, ln_b, alpha)


def kernel(x_prompt, x_sample, cache_nsa_kv, state_nsa_win, state_rwkv_wkv, state_rwkv_shift, page_table,
           nsa_w_in, nsa_cmp_pe, nsa_cmp_w1, nsa_cmp_w2, nsa_w_out,
           rwkv_mu, rwkv_w_rkv, rwkv_w0, rwkv_w1, rwkv_w2, rwkv_a0, rwkv_a1, rwkv_a2, rwkv_g1, rwkv_g2,
           rwkv_k_k, rwkv_k_a, rwkv_r_k, rwkv_lnx_g, rwkv_lnx_b, rwkv_w_out,
           ffn_w_up, ffn_w_down, ln_g, ln_b):
    bsz, t, d = x_prompt.shape
    db, s_len, _ = x_sample.shape
    depth = ffn_w_up.shape[0]
    alpha = (2 * depth) ** 0.25
    np_tok = bsz * t
    x = jnp.concatenate([x_prompt.reshape(np_tok, d), x_sample.reshape(db * s_len, d)], axis=0)
    n_l, n_pool = cache_nsa_kv.shape[:2]
    cache5 = cache_nsa_kv.reshape(n_l, n_pool, PAGE_SIZE, 4 * N_KV_HEADS, HEAD_DIM)
    page_table = page_table.astype(jnp.int32)

    kv_p, kv_s, win_p, win_s, wkv_p, wkv_s, sh_p, sh_s = ([] for _ in range(8))
    row = lambda v: v.reshape(1, -1)
    for i in range(depth):
        j = i // 2
        if i % 2 == 0:
            h, kvp_new, kvs_new, wp_new, ws_new = _nsa_layer(
                x, np_tok, bsz, t, db, s_len, cache5, j, state_nsa_win[j], page_table,
                nsa_w_in, nsa_cmp_pe[j], nsa_cmp_w1[j], nsa_cmp_w2[j], nsa_w_out,
                row(ln_g[i, 0]), row(ln_b[i, 0]), alpha)
            kv_p.append(kvp_new)
            kv_s.append(kvs_new)
            win_p.append(wp_new)
            win_s.append(ws_new)
        else:
            h, sp_new, ss_new, hp_new, hs_new = _rwkv_layer(
                x, np_tok, bsz, t, db, s_len, state_rwkv_shift[j], state_rwkv_wkv[j],
                rwkv_mu[j], rwkv_w_rkv[j], rwkv_w0[j], rwkv_w1[j], rwkv_w2[j], rwkv_a0[j], rwkv_a1[j],
                rwkv_a2[j], rwkv_g1[j], rwkv_g2[j], rwkv_k_k[j], rwkv_k_a[j], rwkv_r_k[j],
                rwkv_lnx_g[j], rwkv_lnx_b[j], rwkv_w_out, j, row(ln_g[i, 0]), row(ln_b[i, 0]), alpha)
            wkv_p.append(sp_new)
            wkv_s.append(ss_new)
            sh_p.append(hp_new)
            sh_s.append(hs_new)
        x = _mlp(h, ffn_w_up, ffn_w_down, i, row(ln_g[i, 1]), row(ln_b[i, 1]), alpha)
    return (x[:np_tok].reshape(bsz, t, d), x[np_tok:].reshape(db, s_len, d),
            jnp.stack(kv_p), jnp.stack(kv_s), jnp.stack(win_p), jnp.stack(win_s),
            jnp.stack(wkv_p), jnp.stack(wkv_s), jnp.stack(sh_p), jnp.stack(sh_s))
```

```python
import functools
import math

import jax
import jax.numpy as jnp
import numpy as np
from jax import lax
from jax.experimental import pallas as pl
from jax.experimental.pallas import tpu as pltpu

F32 = jnp.float32
BF16 = jnp.bfloat16

HEAD_DIM = 128
N_KV_HEADS = 4
GROUP = 4
N_HEADS = N_KV_HEADS * GROUP
QW = GROUP * HEAD_DIM
NQ_COLS = N_HEADS * HEAD_DIM
NKV_COLS = N_KV_HEADS * HEAD_DIM
CMP_BLOCK = 32
CMP_STRIDE = 16
SEL_BLOCK = 64
N_SEL = 16
WINDOW = 512
Q_BLOCK = 256
PAGE_SIZE = 128
RWKV_HEAD = 64
LN_EPS = 1e-5
LNX_EPS = 64e-5
FORCE = 1e4
NEG_INF = -1e30
TINY = 1e-30
CHUNKS_PER_PAGE = PAGE_SIZE // CMP_STRIDE

VMEM_LIMIT_BYTES = 56 * 1024 * 1024
HI = lax.Precision.HIGHEST


def _params(*sem):
    return pltpu.CompilerParams(dimension_semantics=sem, vmem_limit_bytes=VMEM_LIMIT_BYTES)


def _tile(n, pref, mult=16):
    if n <= pref:
        return n
    for t in range(pref - pref % mult, 0, -mult):
        if n % t == 0:
            return t
    return n


def _dot(a, b, precision=None):
    return jnp.dot(a, b, preferred_element_type=F32, precision=precision)


def _dot_nt(a, b, precision=None):
    return lax.dot_general(a, b, (((1,), (1,)), ((), ())), preferred_element_type=F32, precision=precision)


def _dot_tn(a, b, precision=None):
    return lax.dot_general(a, b, (((0,), (0,)), ((), ())), preferred_element_type=F32, precision=precision)


def _masked_softmax(s, mask):
    s = jnp.where(mask, s, NEG_INF)
    m = jnp.max(s, axis=-1, keepdims=True)
    p = jnp.where(mask, jnp.exp(s - m), 0.0)
    return p * (1.0 / jnp.maximum(jnp.sum(p, axis=-1, keepdims=True), TINY))


def _layer_norm_rows(s, g, b, eps):
    mu = jnp.mean(s, axis=-1, keepdims=True)
    c = s - mu
    var = jnp.mean(c * c, axis=-1, keepdims=True)
    return c * lax.rsqrt(var + eps) * g + b


def _mm_kernel(x_ref, w_ref, *refs, act):
    out_refs, xb_ref = refs[:-1], refs[-1]

    @pl.when(pl.program_id(1) == 0)
    def _():
        xb_ref[...] = x_ref[...].astype(BF16)

    y = _dot(xb_ref[...], w_ref[...].astype(BF16))
    if act == "relu2":
        y = jnp.maximum(y, 0.0)
        y = y * y
    elif act == "sigmoid":
        y = jax.nn.sigmoid(y)
    for o_ref in out_refs:
        o_ref[...] = y.astype(o_ref.dtype)


def _mm(x, w, layer, n_out, *, act=None, out_dtypes=(F32,), tm_pref=1040, tn_pref=1024):
    m, k = x.shape
    tm = _tile(m, tm_pref)
    tn = _tile(n_out, tn_pref, 128)
    outs = pl.pallas_call(
        functools.partial(_mm_kernel, act=act),
        grid=(m // tm, n_out // tn),
        in_specs=[pl.BlockSpec((tm, k), lambda i, j: (i, 0)),
                  pl.BlockSpec((None, k, tn), lambda i, j: (layer, 0, j))],
        out_specs=[pl.BlockSpec((tm, tn), lambda i, j: (i, j)) for _ in out_dtypes],
        out_shape=[jax.ShapeDtypeStruct((m, n_out), dt) for dt in out_dtypes],
        scratch_shapes=[pltpu.VMEM((tm, k), BF16)],
        compiler_params=_params("parallel", "arbitrary"),
    )(x, w)
    return outs[0] if len(outs) == 1 else outs


def _mm_mix_kernel(x_ref, xp_ref, mu_ref, w_ref, o_ref, xb_ref):
    @pl.when(pl.program_id(2) == 0)
    def _():
        x = x_ref[...]
        xb_ref[...] = (x + (xp_ref[...] - x) * mu_ref[...]).astype(BF16)

    o_ref[...] = _dot(xb_ref[...], w_ref[...].astype(BF16))


def _mm_mix(x, xp, mu, w, *, tm_pref=1040, tn_pref=512):
    m, k = x.shape
    npar, _, n = w.shape
    tm = _tile(m, tm_pref)
    tn = _tile(n, tn_pref, 128)
    return pl.pallas_call(
        _mm_mix_kernel,
        grid=(m // tm, npar, n // tn),
        in_specs=[pl.BlockSpec((tm, k), lambda i, p, j: (i, 0)),
                  pl.BlockSpec((tm, k), lambda i, p, j: (i, 0)),
                  pl.BlockSpec((None, 1, k), lambda i, p, j: (p, 0, 0)),
                  pl.BlockSpec((None, k, tn), lambda i, p, j: (p, 0, j))],
        out_specs=pl.BlockSpec((None, tm, tn), lambda i, p, j: (p, i, j)),
        out_shape=jax.ShapeDtypeStruct((npar, m, n), F32),
        scratch_shapes=[pltpu.VMEM((tm, k), BF16)],
        compiler_params=_params("parallel", "arbitrary", "arbitrary"),
    )(x, xp, mu, w)


def _loras_kernel(x_ref, xp_ref, mu_ref, w1_ref, w2_ref, w0_ref, a1_ref, a2_ref, a0_ref, g1_ref, g2_ref,
                  lw_ref, a_ref, g_ref):
    x = x_ref[...]
    xx = xp_ref[...] - x
    mix = lambda p: (x + xx * mu_ref[p:p + 1, :]).astype(BF16)
    low = lambda xm, w: _dot(xm, w[...].astype(BF16))
    up = lambda h, w: _dot(h.astype(BF16), w[...].astype(BF16))
    u = -(up(jnp.tanh(low(mix(0), w1_ref)), w2_ref) + w0_ref[...])
    softplus = jnp.maximum(u, 0.0) + jnp.log(1.0 + jnp.exp(-jnp.abs(u)))
    lw_ref[...] = -jnp.exp(-softplus - 0.5)
    a_ref[...] = jax.nn.sigmoid(up(low(mix(1), a1_ref), a2_ref) + a0_ref[...])
    g_ref[...] = up(jax.nn.sigmoid(low(mix(2), g1_ref)), g2_ref)


def _loras(x, xp, mu, w1, w2, w0, a1, a2, a0, g1, g2, *, tm_pref=320):
    m, k = x.shape
    n = w2.shape[1]
    tm = _tile(m, tm_pref)
    row = lambda i: (i, 0)
    fixed = lambda i: (0, 0)
    whole = lambda v: pl.BlockSpec(v.shape, fixed)
    out = jax.ShapeDtypeStruct((m, n), F32)
    return pl.pallas_call(
        _loras_kernel,
        grid=(m // tm,),
        in_specs=[pl.BlockSpec((tm, k), row), pl.BlockSpec((tm, k), row), whole(mu),
                  whole(w1), whole(w2), whole(w0), whole(a1), whole(a2), whole(a0), whole(g1), whole(g2)],
        out_specs=[pl.BlockSpec((tm, n), row)] * 3,
        out_shape=[out, out, out],
        compiler_params=_params("parallel"),
    )(x, xp, mu, w1, w2, w0, a1, a2, a0, g1, g2)


def _mm_ln_kernel(z_ref, w_ref, res_ref, g_ref, b_ref, o_ref, *, alpha):
    kk = pl.program_id(1)
    part = _dot(z_ref[...].astype(BF16), w_ref[...].astype(BF16))

    @pl.when(kk == 0)
    def _():
        o_ref[...] = part

    @pl.when(kk > 0)
    def _():
        o_ref[...] += part

    @pl.when(kk == pl.num_programs(1) - 1)
    def _():
        s = alpha * res_ref[...] + o_ref[...]
        o_ref[...] = _layer_norm_rows(s, g_ref[...], b_ref[...], LN_EPS)


def _mm_ln_cols_kernel(z_ref, w_ref, res_ref, g_ref, b_ref, o_ref, *, alpha, tn):
    j = pl.program_id(1)
    col = pl.multiple_of(j * tn, tn)
    o_ref[:, pl.ds(col, tn)] = _dot(z_ref[...], w_ref[...].astype(BF16))

    @pl.when(j == pl.num_programs(1) - 1)
    def _():
        s = alpha * res_ref[...] + o_ref[...]
        o_ref[...] = _layer_norm_rows(s, g_ref[...], b_ref[...], LN_EPS)


def _mm_ln_cols(z, w, layer, res, g, b, alpha, *, tm_pref=832, tn_pref=512):
    m, k = z.shape
    n = w.shape[2]
    tm = _tile(m, tm_pref)
    tn = _tile(n, tn_pref, 128)
    return pl.pallas_call(
        functools.partial(_mm_ln_cols_kernel, alpha=alpha, tn=tn),
        grid=(m // tm, n // tn),
        in_specs=[pl.BlockSpec((tm, k), lambda i, j: (i, 0)),
                  pl.BlockSpec((None, k, tn), lambda i, j: (layer, 0, j)),
                  pl.BlockSpec((tm, n), lambda i, j: (i, 0)),
                  pl.BlockSpec((1, n), lambda i, j: (0, 0)),
                  pl.BlockSpec((1, n), lambda i, j: (0, 0))],
        out_specs=pl.BlockSpec((tm, n), lambda i, j: (i, 0)),
        out_shape=jax.ShapeDtypeStruct((m, n), F32),
        compiler_params=_params("parallel", "arbitrary"),
    )(z, w, res, g, b)


def _mm_ln(z, w, layer, res, g, b, alpha, *, tm_pref=640, tk_pref=1024):
    m, k = z.shape
    n = w.shape[2]
    tm = _tile(m, tm_pref)
    tk = _tile(k, tk_pref, 128)
    return pl.pallas_call(
        functools.partial(_mm_ln_kernel, alpha=alpha),
        grid=(m // tm, k // tk),
        in_specs=[pl.BlockSpec((tm, tk), lambda i, kk: (i, kk)),
                  pl.BlockSpec((None, tk, n), lambda i, kk: (layer, kk, 0)),
                  pl.BlockSpec((tm, n), lambda i, kk: (i, 0)),
                  pl.BlockSpec((1, n), lambda i, kk: (0, 0)),
                  pl.BlockSpec((1, n), lambda i, kk: (0, 0))],
        out_specs=pl.BlockSpec((tm, n), lambda i, kk: (i, 0)),
        out_shape=jax.ShapeDtypeStruct((m, n), F32),
        compiler_params=_params("parallel", "arbitrary"),
    )(z, w, res, g, b)


def _cmp_bias_kernel(pe_ref, w1_ref, o_ref):
    acc = jnp.zeros((1, HEAD_DIM), F32)
    for rj in range(CMP_BLOCK):
        acc = acc + _dot(pe_ref[rj:rj + 1, :], w1_ref[rj], HI)
    o_ref[...] = acc


def _cmp_bias(pe, w1):
    return pl.pallas_call(
        _cmp_bias_kernel,
        grid=(2,),
        in_specs=[pl.BlockSpec((None, CMP_BLOCK, HEAD_DIM), lambda p: (p, 0, 0)),
                  pl.BlockSpec((None, CMP_BLOCK, HEAD_DIM, HEAD_DIM), lambda p: (p, 0, 0, 0))],
        out_specs=pl.BlockSpec((None, 1, HEAD_DIM), lambda p: (p, 0, 0)),
        out_shape=jax.ShapeDtypeStruct((2, 1, HEAD_DIM), F32),
        compiler_params=_params("parallel"),
    )(pe, w1)


def _cmp_kernel(*refs, n_in):
    pages = refs[-5 - n_in:-5]
    wc_ref, bias_ref, w2_ref, out_ref, carry_ref = refs[-5:]
    mk = CHUNKS_PER_PAGE * n_in
    rows = N_KV_HEADS * mk

    @pl.when(pl.program_id(1) == 0)
    def _():
        carry_ref[...] = jnp.zeros_like(carry_ref)

    first = (lax.broadcasted_iota(jnp.int32, (rows, 1), 0) % mk) == 0
    if len(pages[0].shape) == 3:
        split = [pages[i][...].reshape(CHUNKS_PER_PAGE, CMP_STRIDE, 2 * N_KV_HEADS, HEAD_DIM)
                 for i in range(n_in)]
        by_slot = [[jnp.swapaxes(split[i][:, j], 0, 1) for j in range(CMP_STRIDE)] for i in range(n_in)]
        piece = lambda i, j, slot: by_slot[i][j][slot]
    else:
        r_out = lax.broadcasted_iota(jnp.int32, (PAGE_SIZE, PAGE_SIZE), 0)
        r_in = lax.broadcasted_iota(jnp.int32, (PAGE_SIZE, PAGE_SIZE), 1)
        perm = jnp.where(r_in == (r_out % CHUNKS_PER_PAGE) * CMP_STRIDE + r_out // CHUNKS_PER_PAGE, 1.0, 0.0)
        grouped = [_dot(perm.astype(BF16), pages[i][...].astype(BF16)) for i in range(n_in)]
        piece = lambda i, j, slot: grouped[i][j * CHUNKS_PER_PAGE:(j + 1) * CHUNKS_PER_PAGE,
                                              slot * HEAD_DIM:(slot + 1) * HEAD_DIM]
    for p in range(2):
        lhs = jnp.concatenate(
            [jnp.concatenate([piece(i, j, p * N_KV_HEADS + k).astype(BF16) for j in range(CMP_STRIDE)], axis=1)
             for k in range(N_KV_HEADS) for i in range(n_in)], axis=0)
        acc = _dot(lhs, wc_ref[p])
        part0 = acc[:, :HEAD_DIM]
        part1 = acc[:, HEAD_DIM:]
        prev = jnp.where(first, carry_ref[p], pltpu.roll(part0, 1, 0))
        carry_ref[p] = pltpu.roll(part0, rows - (mk - 1), 0)
        h = prev + part1 + bias_ref[p]
        c = _dot(jax.nn.gelu(h).astype(BF16), w2_ref[p])
        for k in range(N_KV_HEADS):
            col = (p * N_KV_HEADS + k) * HEAD_DIM
            out_ref[:, col:col + HEAD_DIM] = c[k * mk:(k + 1) * mk]


def _cmp_weights(w1, w2):
    n_r = CMP_BLOCK // CMP_STRIDE
    w1r = w1.reshape(2, n_r, CMP_STRIDE, HEAD_DIM, HEAD_DIM)
    wc = jnp.concatenate([w1r[:, r] for r in range(n_r)], axis=-1).astype(BF16)
    return wc.reshape(2, CMP_STRIDE * HEAD_DIM, 2 * HEAD_DIM), w2.astype(BF16)


def _cmp_call(n_seq, n_steps, n_in, page_specs, operands, wc, bias, w2b, num_prefetch, prefetch):
    mk = CHUNKS_PER_PAGE * n_in
    if num_prefetch:
        fixed3 = lambda s, t, pt: (0, 0, 0)
        out_map = lambda s, t, pt: (s, t, 0)
    else:
        fixed3 = lambda s, t: (0, 0, 0)
        out_map = lambda s, t: (s, t, 0)
    grid_spec = pltpu.PrefetchScalarGridSpec(
        num_scalar_prefetch=num_prefetch,
        grid=(n_seq, n_steps),
        in_specs=page_specs + [
            pl.BlockSpec((2, CMP_STRIDE * HEAD_DIM, 2 * HEAD_DIM), fixed3),
            pl.BlockSpec((2, 1, HEAD_DIM), fixed3),
            pl.BlockSpec((2, HEAD_DIM, HEAD_DIM), fixed3)],
        out_specs=pl.BlockSpec((None, mk, 2 * NKV_COLS), out_map),
        scratch_shapes=[pltpu.VMEM((2, N_KV_HEADS * mk, HEAD_DIM), F32)],
    )
    return pl.pallas_call(
        functools.partial(_cmp_kernel, n_in=n_in),
        grid_spec=grid_spec,
        out_shape=jax.ShapeDtypeStruct((n_seq, n_steps * mk, 2 * NKV_COLS), F32),
        compiler_params=_params("parallel", "arbitrary"),
    )(*prefetch, *operands, wc, bias, w2b)


CMP_PAGES_PER_STEP = 8


def _cmp_prompt(y_main, bsz, t, wc, bias, w2b):
    pages_per_seq = t // PAGE_SIZE
    n_in = math.gcd(pages_per_seq, CMP_PAGES_PER_STEP)
    n_steps = pages_per_seq // n_in
    col_blk = NQ_COLS // (2 * NKV_COLS)
    specs = [pl.BlockSpec((PAGE_SIZE, 2 * NKV_COLS),
                          functools.partial(lambda s, st, i: (s * pages_per_seq + st * n_in + i, col_blk), i=i))
             for i in range(n_in)]
    return _cmp_call(bsz, n_steps, n_in, specs, [y_main] * n_in, wc, bias, w2b, 0, ())


def _cmp_sample(cache5, layer, page_table, wc, bias, w2b):
    n_in = math.gcd(page_table.shape[1], CMP_PAGES_PER_STEP)
    db, n_pages = page_table.shape
    n_steps = n_pages // n_in
    specs = [pl.BlockSpec((None, None, PAGE_SIZE, 2 * N_KV_HEADS, HEAD_DIM),
                          functools.partial(lambda s, st, pt, i: (layer, pt[s, st * n_in + i], 0, 0, 0), i=i))
             for i in range(n_in)]
    return _cmp_call(db, n_steps, n_in, specs, [cache5] * n_in, wc, bias, w2b, 1, (page_table,))


def _overlap_shifted(n_rows, n_c, n_s, n_cols):
    m = np.arange(n_rows)[:, None]
    j = np.arange(n_cols)[None, :]
    c0 = (m - 1) * CMP_STRIDE
    s0 = j * SEL_BLOCK
    ov = (c0 < s0 + SEL_BLOCK) & (c0 + CMP_BLOCK > s0) & (m >= 1) & (m <= n_c) & (j < n_s)
    return jnp.asarray(ov.astype(np.float32))


def _select_blocks(imp, qpos, n_s, n_cols):
    j = lax.broadcasted_iota(jnp.int32, (1, n_cols), 1)
    jf = j.astype(F32)
    cur = qpos // SEL_BLOCK
    forced = (j == 0) | (j == cur) | (j == cur - 1)
    valid = j * SEL_BLOCK <= qpos
    score = jnp.where(forced, FORCE, jnp.where(valid, imp, -FORCE))
    score = jnp.where(j < n_s, score, -jnp.inf)
    sel = jnp.zeros(score.shape, F32)
    for _ in range(min(N_SEL, n_s)):
        mx = jnp.max(score, axis=-1, keepdims=True)
        idx = jnp.min(jnp.where(score == mx, jf, float(n_cols)), axis=-1, keepdims=True)
        hit = jf == idx
        sel = jnp.where(hit, 1.0, sel)
        score = jnp.where(hit, -jnp.inf, score)
    return sel


SEL_KEY_TILE = 512


def _softmax_cols(s, mask):
    s = jnp.where(mask, s, NEG_INF)
    m = jnp.max(s, axis=0, keepdims=True)
    p = jnp.where(mask, jnp.exp2(s - m), 0.0)
    return p * (1.0 / jnp.maximum(jnp.sum(p, axis=0, keepdims=True), TINY))


def _select_blocks_cols(imp, qpos, n_s):
    j = lax.broadcasted_iota(jnp.int32, (n_s, 1), 0)
    cur = qpos // SEL_BLOCK
    forced = (j == 0) | (j == cur) | (j == cur - 1)
    valid = j * SEL_BLOCK <= qpos
    score = jnp.where(forced, FORCE, jnp.where(valid, imp, -FORCE))
    sub = 8
    blocks = [score[b * sub:(b + 1) * sub] for b in range(n_s // sub)]
    ranks = [jnp.zeros(blk.shape, F32) for blk in blocks]
    row = lax.broadcasted_iota(jnp.int32, (sub, 1), 0)
    for i in range(n_s):
        bi, ri = divmod(i, sub)
        si = blocks[bi][ri:ri + 1, :]
        for b, blk in enumerate(blocks):
            if b < bi:
                beats = si > blk
            elif b > bi:
                beats = si >= blk
            else:
                beats = jnp.where(row > ri, jnp.where(si >= blk, 1.0, 0.0), jnp.where(si > blk, 1.0, 0.0)) > 0.5
            ranks[b] = ranks[b] + jnp.where(beats, 1.0, 0.0)
    rank = jnp.concatenate(ranks, axis=0)
    return jnp.where(rank < float(min(N_SEL, n_s)), 1.0, 0.0)


def _nsa_prompt_kernel(q_ref, kc_ref, vct_ref, ks_ref, vst_ref, kw_ref, vwt_ref, g_ref, ovt_ref, rest_ref, o_ref,
                       selbias_ref, *, t):
    del rest_ref
    start = pl.program_id(2) * Q_BLOCK
    n_cols = GROUP * Q_BLOCK
    q = q_ref[...] * (HEAD_DIM ** -0.5 * math.log2(math.e))
    q4 = jnp.concatenate([q[:, g * HEAD_DIM:(g + 1) * HEAD_DIM] for g in range(GROUP)], axis=0).astype(BF16)
    qpos1 = start + lax.broadcasted_iota(jnp.int32, (1, Q_BLOCK), 1)
    qpos4 = start + lax.broadcasted_iota(jnp.int32, (1, n_cols), 1) % Q_BLOCK

    n_cr = kc_ref.shape[0]
    n_s = ovt_ref.shape[0]
    s = _dot_nt(kc_ref[...].astype(BF16), q4)
    m_idx = lax.broadcasted_iota(jnp.int32, (n_cr, 1), 0)
    cmask = (m_idx >= 1) & (m_idx * CMP_STRIDE + (CMP_BLOCK - CMP_STRIDE - 1) <= qpos4)
    p_c = _softmax_cols(s, cmask)
    o_c = _dot(vct_ref[...].astype(BF16), p_c.astype(BF16))
    psum = p_c[:, 0:Q_BLOCK]
    for g in range(1, GROUP):
        psum = psum + p_c[:, g * Q_BLOCK:(g + 1) * Q_BLOCK]
    imp = _dot(ovt_ref[...], psum, HI)
    sel = _select_blocks_cols(imp, qpos1, n_s)

    wlen = WINDOW + Q_BLOCK
    w0 = pl.multiple_of(jnp.clip(start - WINDOW, 0, t - wlen), PAGE_SIZE)
    sw = _dot_nt(kw_ref[pl.ds(w0, wlen), :], q4)
    dpos = qpos4 - (w0 + lax.broadcasted_iota(jnp.int32, (wlen, 1), 0))
    p_w = _softmax_cols(sw, (dpos >= 0) & (dpos < WINDOW))
    wblk = w0 // PAGE_SIZE
    vwt = jnp.concatenate([vwt_ref[wblk + i] for i in range(wlen // PAGE_SIZE)], axis=1)
    o_w = _dot(vwt, p_w.astype(BF16))

    tk = SEL_KEY_TILE
    blocks_per_tile = tk // SEL_BLOCK
    selbias_ref[...] = jnp.where(sel > 0.5, 0.0, NEG_INF)

    def scores(kt):
        return _dot_nt(ks_ref[pl.ds(pl.multiple_of(kt * tk, tk), tk), :], q4)

    def tile(kt, carry, last):
        m, l, acc, sc = carry
        sc_next = sc if last else scores(kt + 1)
        off = kt * tk
        rows = selbias_ref[pl.ds(pl.multiple_of(kt * blocks_per_tile, blocks_per_tile), blocks_per_tile), :]
        bias = jnp.concatenate([jnp.broadcast_to(rows[i:i + 1, :], (SEL_BLOCK, Q_BLOCK))
                                for i in range(blocks_per_tile)], axis=0)
        if last:
            kpos = off + lax.broadcasted_iota(jnp.int32, (tk, Q_BLOCK), 0)
            bias = jnp.where(kpos <= qpos1, bias, NEG_INF)
        sc = sc + jnp.concatenate([bias] * GROUP, axis=1)
        m_new = jnp.maximum(m, jnp.max(sc, axis=0, keepdims=True))
        alpha = jnp.exp2(m - m_new)
        p = jnp.exp2(sc - m_new)
        l = alpha * l + jnp.sum(p, axis=0, keepdims=True)
        vblk = kt * (tk // PAGE_SIZE)
        vst = jnp.concatenate([vst_ref[vblk + i] for i in range(tk // PAGE_SIZE)], axis=1)
        acc = alpha * acc + _dot(vst, p.astype(BF16))
        return m_new, l, acc, sc_next

    n_tiles = (start + Q_BLOCK + tk - 1) // tk
    m0 = jnp.full((1, n_cols), NEG_INF, F32)
    l0 = jnp.zeros((1, n_cols), F32)
    a0 = jnp.zeros((HEAD_DIM, n_cols), F32)
    carry = lax.fori_loop(0, n_tiles - 1, functools.partial(tile, last=False), (m0, l0, a0, scores(0)))
    _, l, acc, _ = tile(n_tiles - 1, carry, last=True)
    o_s = acc * (1.0 / jnp.maximum(l, TINY))

    gt = g_ref[...]
    o = gt[0:1, :] * o_c + gt[1:2, :] * o_s + gt[2:3, :] * o_w
    o_ref[...] = jnp.concatenate([o[:, g * Q_BLOCK:(g + 1) * Q_BLOCK].T for g in range(GROUP)],
                                 axis=1).astype(o_ref.dtype)


def _values_t(y_b, bsz, t, col0):
    v = lax.slice(y_b, (0, col0), (bsz * t, col0 + NKV_COLS))
    v = v.reshape(bsz, t // PAGE_SIZE, PAGE_SIZE, N_KV_HEADS, HEAD_DIM)
    return v.transpose(0, 3, 1, 4, 2)


def _nsa_prompt(y_main, y_b, cmp_p, gates, bsz, t, rest):
    assert t % SEL_KEY_TILE == 0 and t >= WINDOW + Q_BLOCK and SEL_KEY_TILE % PAGE_SIZE == 0
    np_tok = bsz * t
    nq = t // Q_BLOCK
    n_cr = t // CMP_STRIDE
    n_c = (t - CMP_BLOCK) // CMP_STRIDE + 1
    n_s = t // SEL_BLOCK
    assert n_s >= N_SEL and n_s % 8 == 0
    ovt = _overlap_shifted(n_cr, n_c, n_s, n_s).T
    n_pg = t // PAGE_SIZE
    v_sel_t = _values_t(y_b, bsz, t, NQ_COLS + 3 * NKV_COLS)
    v_win_t = _values_t(y_b, bsz, t, NQ_COLS + 5 * NKV_COLS)
    cmp_t = cmp_p.transpose(0, 2, 1)
    cb = NQ_COLS // HEAD_DIM
    g_t = gates[:np_tok].reshape(bsz, nq, Q_BLOCK, N_KV_HEADS, GROUP, 3).transpose(0, 3, 1, 5, 4, 2)
    g_t = g_t.reshape(bsz, N_KV_HEADS, nq, 3, GROUP * Q_BLOCK)
    seq_key = lambda c4: pl.BlockSpec((t, HEAD_DIM), lambda b, k, i: (b, cb + c4 * N_KV_HEADS + k))
    seq_val = pl.BlockSpec((None, None, n_pg, HEAD_DIM, PAGE_SIZE), lambda b, k, i: (b, k, 0, 0, 0))
    return pl.pallas_call(
        functools.partial(_nsa_prompt_kernel, t=t),
        grid=(bsz, N_KV_HEADS, nq),
        in_specs=[pl.BlockSpec((Q_BLOCK, QW), lambda b, k, i: (b * nq + i, k)),
                  pl.BlockSpec((None, n_cr, HEAD_DIM), lambda b, k, i: (b, 0, k)),
                  pl.BlockSpec((None, HEAD_DIM, n_cr), lambda b, k, i: (b, N_KV_HEADS + k, 0)),
                  seq_key(2), seq_val, seq_key(4), seq_val,
                  pl.BlockSpec((None, None, None, 3, GROUP * Q_BLOCK), lambda b, k, i: (b, k, i, 0, 0)),
                  pl.BlockSpec((n_s, n_cr), lambda b, k, i: (0, 0)),
                  pl.BlockSpec(memory_space=pl.ANY)],
        out_specs=pl.BlockSpec((Q_BLOCK, QW), lambda b, k, i: (b * nq + i, k)),
        out_shape=jax.ShapeDtypeStruct(rest.shape, rest.dtype),
        input_output_aliases={9: 0},
        scratch_shapes=[pltpu.VMEM((n_s, Q_BLOCK), F32)],
        compiler_params=_params("parallel", "parallel", "arbitrary"),
    )(y_main, cmp_p, cmp_t, y_b, v_sel_t, y_b, v_win_t, g_t, ovt, rest)


SAMPLE_ROWS = 8


def _nsa_sample_kernel(pt_ref, y_ref, cmp_ref, cache_ref, win_ref, g_ref, ov_ref, o_ref,
                       q_scr, sel_scr, m_scr, l_scr, acc_scr, oc_scr, kv_buf, kv_sem,
                       *, layer, past, s_len, n_s, n_in):
    pg = pl.program_id(1)
    n_pages = pl.num_programs(1)
    step = pl.program_id(0) * n_pages + pg
    cur = step % 2

    def page_copies(seq, grp, buf):
        out = []
        for i in range(n_in):
            page = pt_ref[seq, grp * n_in + i]
            for s in range(2 * N_KV_HEADS):
                out.append(pltpu.make_async_copy(
                    cache_ref.at[layer, page, :, 2 * N_KV_HEADS + s, :],
                    kv_buf.at[buf, s, pl.ds(i * PAGE_SIZE, PAGE_SIZE), :],
                    kv_sem.at[buf]))
        return out

    @pl.when(step == 0)
    def _():
        for cp in page_copies(0, 0, 0):
            cp.start()

    @pl.when(step + 1 < pl.num_programs(0) * n_pages)
    def _():
        nxt = step + 1
        for cp in page_copies(nxt // n_pages, nxt % n_pages, 1 - cur):
            cp.start()

    for cp in page_copies(pl.program_id(0), pg, cur):
        cp.wait()

    sr = SAMPLE_ROWS
    rk = GROUP * sr
    scale = HEAD_DIM ** -0.5
    tpos = lax.broadcasted_iota(jnp.int32, (rk, 1), 0) % sr
    qpos = past + tpos
    n_sc = sel_scr.shape[1]
    nbp = past // SEL_BLOCK
    kv0 = NQ_COLS

    @pl.when(pg == 0)
    def _():
        y = y_ref[...]
        n_cr = cmp_ref.shape[0]
        m_idx = lax.broadcasted_iota(jnp.int32, (1, n_cr), 1)
        cmask = (m_idx >= 1) & (m_idx * CMP_STRIDE + (CMP_BLOCK - CMP_STRIDE - 1) <= qpos)
        psums = []
        for k in range(N_KV_HEADS):
            qk = jnp.concatenate([y[:, (k * GROUP + g) * HEAD_DIM:(k * GROUP + g + 1) * HEAD_DIM]
                                  for g in range(GROUP)], axis=0) * scale
            qk = qk.astype(BF16)
            q_scr[k] = qk
            kc = cmp_ref[:, k * HEAD_DIM:(k + 1) * HEAD_DIM].astype(BF16)
            vc = cmp_ref[:, (N_KV_HEADS + k) * HEAD_DIM:(N_KV_HEADS + k + 1) * HEAD_DIM].astype(BF16)
            p_c = _masked_softmax(_dot_nt(qk, kc), cmask)
            oc_scr[k * rk:(k + 1) * rk, :] = _dot(p_c.astype(BF16), vc)
            psum = p_c[0:sr]
            for g in range(1, GROUP):
                psum = psum + p_c[g * sr:(g + 1) * sr]
            psums.append(psum)
        imp = _dot(jnp.concatenate(psums, axis=0), ov_ref[...], HI)
        qpos_sel = past + lax.broadcasted_iota(jnp.int32, (N_KV_HEADS * sr, 1), 0) % sr
        sel = _select_blocks(imp, qpos_sel, n_s, n_sc)
        for k in range(N_KV_HEADS):
            for g in range(GROUP):
                sel_scr[k * rk + g * sr:k * rk + (g + 1) * sr, :] = sel[k * sr:(k + 1) * sr]
        m_scr[...] = jnp.full(m_scr.shape, NEG_INF, F32)
        l_scr[...] = jnp.zeros(l_scr.shape, F32)
        acc_scr[...] = jnp.zeros(acc_scr.shape, F32)

    def online_update(k, sc, mask, v):
        rows = slice(k * rk, (k + 1) * rk)
        sc = jnp.where(mask, sc, NEG_INF)
        m_old = m_scr[rows, :]
        m_new = jnp.maximum(m_old, jnp.max(sc, axis=-1, keepdims=True))
        alpha = jnp.exp(m_old - m_new)
        p = jnp.where(mask, jnp.exp(sc - m_new), 0.0)
        l_scr[rows, :] = alpha * l_scr[rows, :] + jnp.sum(p, axis=-1, keepdims=True)
        acc_scr[rows, :] = alpha * acc_scr[rows, :] + _dot(p.astype(v.dtype), v)
        m_scr[rows, :] = m_new

    n_keys = n_in * PAGE_SIZE
    kpos = pg * n_keys + lax.broadcasted_iota(jnp.int32, (1, n_keys), 1)
    blk_row = lax.broadcasted_iota(jnp.int32, (n_sc, 1), 0)
    expand = jnp.where(blk_row == kpos // SEL_BLOCK, 1.0, 0.0).astype(BF16)
    selx = _dot(sel_scr[...].astype(BF16), expand)
    for k in range(N_KV_HEADS):
        kp = kv_buf[cur, k].astype(BF16)
        vp = kv_buf[cur, N_KV_HEADS + k].astype(BF16)
        mask = (selx[k * rk:(k + 1) * rk] > 0.5) & (kpos <= qpos)
        online_update(k, _dot_nt(q_scr[k], kp), mask, vp)

    @pl.when(pg == n_pages - 1)
    def _():
        y = y_ref[...]
        rpos = lax.broadcasted_iota(jnp.int32, (1, sr), 1)
        new_ok = (rpos < s_len) & (rpos <= tpos)
        gt = g_ref[...]
        win_by_slot = jnp.swapaxes(win_ref[...], 0, 1).astype(BF16)
        for k in range(N_KV_HEADS):
            rows = slice(k * rk, (k + 1) * rk)
            qk = q_scr[k]
            qf = qk.astype(F32)
            c_ks = kv0 + (2 * N_KV_HEADS + k) * HEAD_DIM
            c_vs = kv0 + (3 * N_KV_HEADS + k) * HEAD_DIM
            kn = y[:, c_ks:c_ks + HEAD_DIM]
            vn = y[:, c_vs:c_vs + HEAD_DIM]
            new_sel = sel_scr[rows, nbp:nbp + 1] > 0.5
            online_update(k, _dot_nt(qf, kn), new_sel & new_ok, vn)
            o_s = acc_scr[rows, :] / jnp.maximum(l_scr[rows, :], TINY)
            wb = win_ref.shape[0]
            c_kw = kv0 + (4 * N_KV_HEADS + k) * HEAD_DIM
            c_vw = kv0 + (5 * N_KV_HEADS + k) * HEAD_DIM
            kwb = win_by_slot[k]
            vwb = win_by_slot[N_KV_HEADS + k]
            kwn = y[:, c_kw:c_kw + HEAD_DIM]
            vwn = y[:, c_vw:c_vw + HEAD_DIM]
            dpos = qpos - (past - wb + lax.broadcasted_iota(jnp.int32, (1, wb), 1))
            mask_b = (dpos >= 0) & (dpos < WINDOW)
            s_b = jnp.where(mask_b, _dot_nt(qk, kwb), NEG_INF)
            s_n = jnp.where(new_ok, _dot_nt(qf, kwn), NEG_INF)
            mw = jnp.maximum(jnp.max(s_b, axis=-1, keepdims=True), jnp.max(s_n, axis=-1, keepdims=True))
            p_b = jnp.where(mask_b, jnp.exp(s_b - mw), 0.0)
            p_n = jnp.where(new_ok, jnp.exp(s_n - mw), 0.0)
            den = jnp.sum(p_b, axis=-1, keepdims=True) + jnp.sum(p_n, axis=-1, keepdims=True)
            o_w = (_dot(p_b.astype(BF16), vwb) + _dot(p_n, vwn)) / jnp.maximum(den, TINY)
            o_c = oc_scr[rows, :]
            for g in range(GROUP):
                r0, r1 = g * sr, (g + 1) * sr
                c = (k * GROUP + g) * 3
                o = gt[:, c:c + 1] * o_c[r0:r1] + gt[:, c + 1:c + 2] * o_s[r0:r1] + gt[:, c + 2:c + 3] * o_w[r0:r1]
                o_ref[:, (k * GROUP + g) * HEAD_DIM:(k * GROUP + g + 1) * HEAD_DIM] = o


SAMPLE_PAGES_PER_STEP = 8


def _nsa_sample(ys8, cmp_s, cache5, layer, win4, gates8, page_table, s_len):
    db, n_pages = page_table.shape
    past = n_pages * PAGE_SIZE
    n_in = math.gcd(n_pages, SAMPLE_PAGES_PER_STEP)
    assert s_len <= SAMPLE_ROWS and s_len <= SEL_BLOCK and past % SEL_BLOCK == 0
    n_cr = cmp_s.shape[1]
    n_c = (past + s_len - CMP_BLOCK) // CMP_STRIDE + 1
    assert n_c + 1 <= n_cr
    n_s = -(-(past + s_len) // SEL_BLOCK)
    n_sc = -(-n_s // 128) * 128
    ov = _overlap_shifted(n_cr, n_c, n_s, n_sc)
    wb = win4.shape[1]
    ncol = ys8.shape[2]
    rows = N_KV_HEADS * GROUP * SAMPLE_ROWS
    grid_spec = pltpu.PrefetchScalarGridSpec(
        num_scalar_prefetch=1,
        grid=(db, n_pages // n_in),
        in_specs=[pl.BlockSpec((None, SAMPLE_ROWS, ncol), lambda b, p, pt: (b, 0, 0)),
                  pl.BlockSpec((None, n_cr, 2 * NKV_COLS), lambda b, p, pt: (b, 0, 0)),
                  pl.BlockSpec(memory_space=pl.ANY),
                  pl.BlockSpec((None, wb, 2 * N_KV_HEADS, HEAD_DIM), lambda b, p, pt: (b, 0, 0, 0)),
                  pl.BlockSpec((None, SAMPLE_ROWS, 3 * N_HEADS), lambda b, p, pt: (b, 0, 0)),
                  pl.BlockSpec((n_cr, n_sc), lambda b, p, pt: (0, 0))],
        out_specs=pl.BlockSpec((None, SAMPLE_ROWS, NQ_COLS), lambda b, p, pt: (b, 0, 0)),
        scratch_shapes=[pltpu.VMEM((N_KV_HEADS, GROUP * SAMPLE_ROWS, HEAD_DIM), BF16),
                        pltpu.VMEM((rows, n_sc), F32),
                        pltpu.VMEM((rows, 1), F32),
                        pltpu.VMEM((rows, 1), F32),
                        pltpu.VMEM((rows, HEAD_DIM), F32),
                        pltpu.VMEM((rows, HEAD_DIM), F32),
                        pltpu.VMEM((2, 2 * N_KV_HEADS, n_in * PAGE_SIZE, HEAD_DIM), F32),
                        pltpu.SemaphoreType.DMA((2,))],
    )
    return pl.pallas_call(
        functools.partial(_nsa_sample_kernel, layer=layer, past=past, s_len=s_len, n_s=n_s, n_in=n_in),
        grid_spec=grid_spec,
        out_shape=jax.ShapeDtypeStruct((db, SAMPLE_ROWS, NQ_COLS), F32),
        compiler_params=_params("arbitrary", "arbitrary"),
    )(page_table, ys8, cmp_s, cache5, win4, gates8, ov)


WKV_DIAG_BLOCK = 16
WKV_CHUNK = 64


def _bdot(a, b):
    return _dot(a.astype(BF16), b.astype(BF16))


def _bdot_nt(a, b):
    return _dot_nt(a.astype(BF16), b.astype(BF16))


def _bdot_tn(a, b):
    return _dot_tn(a.astype(BF16), b.astype(BF16))


def _unit_lower_inverses(mats, c):
    blk = min(WKV_DIAG_BLOCK, c)
    ri = lax.broadcasted_iota(jnp.int32, (c, c), 0)
    ci = lax.broadcasted_iota(jnp.int32, (c, c), 1)
    eye = jnp.where(ri == ci, 1.0, 0.0)
    same = (ri // blk) == (ci // blk)
    d = [jnp.where(same, a, 0.0) for a in mats]
    td = [eye + x for x in d]
    n_sq = int(math.log2(blk)) - 1
    pw = [_bdot(x, x) for x in d]
    stack = c >= WKV_DIAG_BLOCK
    for level in range(n_sq):
        if level + 1 < n_sq and stack:
            both = [_bdot(jnp.concatenate([p, t], axis=0), p) for p, t in zip(pw, td)]
            pw = [x[:c] for x in both]
            td = [t + x[c:] for t, x in zip(td, both)]
        else:
            td = [t + _bdot(t, p) for t, p in zip(td, pw)]
            if level + 1 < n_sq:
                pw = [_bdot(p, p) for p in pw]
    if blk == c:
        return td
    npow = [_bdot(t, jnp.where(same, 0.0, a)) for t, a in zip(td, mats)]
    tinv = [t + _bdot(x, t) for x, t in zip(npow, td)]
    for _ in range(int(math.log2(c // blk)) - 1):
        npow = [_bdot(x, x) for x in npow]
        tinv = [t + _bdot(x, t) for x, t in zip(npow, tinv)]
    return tinv


def _wkv_kernel(r_ref, k_ref, v_ref, lw_ref, a_ref, g_ref, kk_ref, ka_ref, rk_ref, lg_ref, lb_ref, s0_ref,
                rest_ref, z_ref, s_ref, *, c, nc, hb):
    del rest_ref
    n = RWKV_HEAD

    @pl.when(pl.program_id(2) == 0)
    def _():
        s_ref[...] = s0_ref[...]

    ri = lax.broadcasted_iota(jnp.int32, (c, c), 0)
    ci = lax.broadcasted_iota(jnp.int32, (c, c), 1)
    strict = ri > ci
    incl = ri >= ci
    tri = jnp.where(incl, 1.0, 0.0)
    lw_all = lw_ref[...]
    cum = jnp.concatenate([_dot(tri, lw_all[i * c:(i + 1) * c], HI) for i in range(nc)], axis=0)
    e_in_all = jnp.exp(cum)
    e_neg_all = jnp.exp(-cum)
    e_ex_all = jnp.exp(cum - lw_all)
    heads = range(hb)
    chunks = range(nc)
    sls = [slice(hh * n, (hh + 1) * n) for hh in heads]
    rws = [slice(i * c, (i + 1) * c) for i in chunks]
    r, v, k2, r_t, a_t, b_t, k_t, vb = ([] for _ in range(8))
    for sl in sls:
        k = k_ref[:, sl]
        a = a_ref[:, sl]
        kk = k * kk_ref[:, sl]
        kk = kk * (1.0 / jnp.maximum(jnp.sqrt(jnp.sum(kk * kk, axis=-1, keepdims=True)), 1e-12))
        r.append(r_ref[:, sl])
        v.append(v_ref[:, sl])
        k2.append(k * (1.0 + (a - 1.0) * ka_ref[:, sl]))
        r_t.append(r[-1] * e_in_all[:, sl])
        a_t.append(-kk * e_ex_all[:, sl])
        b_t.append(kk * a * e_neg_all[:, sl])
        k_t.append(k2[-1] * e_neg_all[:, sl])
        vb.append(v[-1].astype(BF16))

    items = [(hh, i) for hh in heads for i in chunks]
    aa = [_bdot_nt(jnp.concatenate([a_t[hh][rws[i]], r_t[hh][rws[i]]], axis=0),
                   jnp.concatenate([b_t[hh][rws[i]], k_t[hh][rws[i]]], axis=0)) for hh, i in items]
    r2 = lax.broadcasted_iota(jnp.int32, (2 * c, c), 0)
    c2 = lax.broadcasted_iota(jnp.int32, (2 * c, c), 1)
    tri2 = jnp.where(r2 < c, r2, r2 - c + 1) > c2
    a_ab = [jnp.where(strict, x[:c, :c], 0.0) for x in aa]
    a_rb = [jnp.where(incl, x[c:, :c], 0.0).astype(BF16) for x in aa]
    xkv = [_bdot(jnp.where(tri2, x[:, c:], 0.0), vb[hh][rws[i]]) for x, (hh, i) in zip(aa, items)]
    yv = [x[c:] for x in xkv]
    w_c = [e_in_all[i * c + c - 1:(i + 1) * c, sls[hh]] for hh, i in items]
    kv = [_bdot_tn(vb[hh][rws[i]], k_t[hh][rws[i]] * w) for w, (hh, i) in zip(w_c, items)]
    bh = [(b_t[hh][rws[i]] * w).astype(BF16) for w, (hh, i) in zip(w_c, items)]
    tinv = _unit_lower_inverses(a_ab, c)
    wt = [_bdot(t, a_t[hh][rws[i]]) for t, (hh, i) in zip(tinv, items)]
    u = [_bdot(t, x[:c]) for t, x in zip(tinv, xkv)]
    wr = [jnp.concatenate([w, r_t[hh][rws[i]]], axis=0).astype(BF16) for w, (hh, i) in zip(wt, items)]

    s = [s_ref[hh] for hh in heads]
    ys = [[] for _ in heads]
    for i in chunks:
        idx = [hh * nc + i for hh in heads]
        ls = [_dot_nt(wr[j], s[hh].astype(BF16)) for hh, j in zip(heads, idx)]
        p = [x[:c] + u[j] for x, j in zip(ls, idx)]
        for hh, j in zip(heads, idx):
            ys[hh].append(ls[hh][c:] + _bdot(a_rb[j], p[hh]) + yv[j])
        s = [s[hh] * w_c[j] + _bdot_tn(p[hh], bh[j]) + kv[j] for hh, j in zip(heads, idx)]
    zs = []
    for hh, sl in zip(heads, sls):
        s_ref[hh] = s[hh]
        y = ys[hh][0] if nc == 1 else jnp.concatenate(ys[hh], axis=0)
        mu = jnp.mean(y, axis=-1, keepdims=True)
        yc = y - mu
        yn = yc * lax.rsqrt(jnp.mean(yc * yc, axis=-1, keepdims=True) + LNX_EPS)
        yn = yn * lg_ref[:, sl] + lb_ref[:, sl]
        bonus = jnp.sum(r[hh] * k2[hh] * rk_ref[:, sl], axis=-1, keepdims=True) * v[hh]
        zs.append((yn + bonus) * g_ref[:, sl])
    z_ref[...] = jnp.concatenate(zs, axis=1).astype(z_ref.dtype)


def _wkv(rkv, lw, a, g, k_k, k_a, r_k, lnx_g, lnx_b, s0, rest, n_seq, t, c, nc, hb):
    rows, d = lw.shape
    h = d // RWKV_HEAD
    hw = hb * RWKV_HEAD
    tb = nc * c
    nblk = t // tb
    tok = lambda s, hg, ch: (s * nblk + ch, hg)
    par = lambda s, hg, ch: (0, hg)
    st = lambda s, hg, ch: (s, hg, 0, 0)
    rkv_spec = lambda p: pl.BlockSpec((None, tb, hw), lambda s, hg, ch: (p, s * nblk + ch, hg))
    return pl.pallas_call(
        functools.partial(_wkv_kernel, c=c, nc=nc, hb=hb),
        grid=(n_seq, h // hb, nblk),
        in_specs=[rkv_spec(0), rkv_spec(1), rkv_spec(2),
                  pl.BlockSpec((tb, hw), tok), pl.BlockSpec((tb, hw), tok), pl.BlockSpec((tb, hw), tok),
                  pl.BlockSpec((1, hw), par), pl.BlockSpec((1, hw), par), pl.BlockSpec((1, hw), par),
                  pl.BlockSpec((1, hw), par), pl.BlockSpec((1, hw), par),
                  pl.BlockSpec((None, hb, RWKV_HEAD, RWKV_HEAD), st),
                  pl.BlockSpec(memory_space=pl.ANY)],
        out_specs=[pl.BlockSpec((tb, hw), tok),
                   pl.BlockSpec((None, hb, RWKV_HEAD, RWKV_HEAD), st)],
        out_shape=[jax.ShapeDtypeStruct((rows, d), BF16),
                   jax.ShapeDtypeStruct((n_seq, h, RWKV_HEAD, RWKV_HEAD), F32)],
        input_output_aliases={12: 0},
        compiler_params=_params("parallel", "parallel", "arbitrary"),
    )(rkv, rkv, rkv, lw, a, g, k_k, k_a, r_k, lnx_g, lnx_b, s0, rest)


WKV_PROMPT_CHUNKS_PER_STEP = 2
WKV_PROMPT_HEADS_PER_STEP = 8


def _to_slots_kernel(x_ref, o_ref):
    x = x_ref[...]
    by_slot = jnp.stack([x[:, s * HEAD_DIM:(s + 1) * HEAD_DIM] for s in range(o_ref.shape[1])], axis=0)
    o_ref[...] = jnp.swapaxes(by_slot, 0, 1)


def _to_slots(y, rows, col0, n_slots, *, tm_pref=512):
    width = n_slots * HEAD_DIM
    assert col0 % width == 0
    tm = _tile(rows, tm_pref)
    return pl.pallas_call(
        _to_slots_kernel,
        grid=(rows // tm,),
        in_specs=[pl.BlockSpec((tm, width), lambda i: (i, col0 // width))],
        out_specs=pl.BlockSpec((tm, n_slots, HEAD_DIM), lambda i: (i, 0, 0)),
        out_shape=jax.ShapeDtypeStruct((rows, n_slots, HEAD_DIM), y.dtype),
        compiler_params=_params("parallel"),
    )(y)


def _nsa_layer(x, np_tok, bsz, t, db, s_len, cache5, layer, win, page_table, w_in, pe, w1, w2, w_out,
               ln_g, ln_b, alpha):
    n_main = NQ_COLS + 6 * NKV_COLS
    y_main, y_b = _mm(x, w_in, layer, n_main, out_dtypes=(F32, BF16), tm_pref=832, tn_pref=1024)
    w_gate = jnp.pad(w_in[layer:layer + 1, :, n_main:], ((0, 0), (0, 0), (0, 128 - 3 * N_HEADS)))
    gates = _mm(x, w_gate, 0, 128, act="sigmoid")[:, :3 * N_HEADS]

    wc, w2b = _cmp_weights(w1, w2)
    bias = _cmp_bias(pe, w1)
    cmp_p = _cmp_prompt(y_main, bsz, t, wc, bias, w2b)
    cmp_s = _cmp_sample(cache5, layer, page_table, wc, bias, w2b)

    pad_rows = ((0, 0), (0, SAMPLE_ROWS - s_len), (0, 0))
    ys = y_main[np_tok:].reshape(db, s_len, n_main)
    ys8 = jnp.pad(ys, pad_rows)
    gates8 = jnp.pad(gates[np_tok:].reshape(db, s_len, 3 * N_HEADS), pad_rows)
    wb = win.shape[1]
    win4 = win.reshape(db, wb, 2 * N_KV_HEADS, HEAD_DIM)
    o_s = _nsa_sample(ys8, cmp_s, cache5, layer, win4, gates8, page_table, s_len)
    o_s = o_s[:, :s_len].reshape(db * s_len, NQ_COLS).astype(BF16)

    o = _nsa_prompt(y_main, y_b, cmp_p, gates, bsz, t, jnp.pad(o_s, ((np_tok, 0), (0, 0))))
    h = _mm_ln_cols(o, w_out, layer, x, ln_g, ln_b, alpha)

    kv0 = NQ_COLS
    kw0 = NQ_COLS + 4 * NKV_COLS
    kv_p = _to_slots(y_main, np_tok, kv0, 4 * N_KV_HEADS).reshape(bsz, t, 4, N_KV_HEADS, HEAD_DIM)
    wlen = min(WINDOW, t)
    win_p = jnp.stack([lax.slice(y_main, (b * t + t - wlen, kw0), ((b + 1) * t, n_main)) for b in range(bsz)])
    win_p = win_p.reshape(bsz, wlen, 2, N_KV_HEADS, HEAD_DIM)
    kv_s = ys[:, :, kv0:kw0].reshape(db, s_len, 4, N_KV_HEADS, HEAD_DIM)
    kvw_s = ys[:, :, kw0:].reshape(db, s_len, 2, N_KV_HEADS, HEAD_DIM)
    win_s = jnp.concatenate([win, kvw_s], axis=1)[:, s_len:]
    return h, kv_p, kv_s, win_p, win_s


def _rwkv_layer(x, np_tok, bsz, t, db, s_len, shift_s, wkv_s, mu, w_rkv, w0, w1, w2, a0, a1, a2, g1, g2,
                k_k, k_a, r_k, lnx_g, lnx_b, w_out, layer, ln_g, ln_b, alpha):
    d = x.shape[1]
    heads = d // RWKV_HEAD
    xprev = jnp.concatenate([jnp.zeros((1, d), x.dtype), x[:-1]], axis=0)
    xprev = xprev.at[np.arange(1, bsz) * t].set(0.0)
    xprev = xprev.at[np_tok + np.arange(db) * s_len].set(shift_s.astype(x.dtype))

    row = lambda v: v.reshape(1, -1)
    rkv = _mm_mix(x, xprev, mu[:3, None, :], w_rkv)
    lw, a, g = _loras(x, xprev, mu[3:6], w1, w2, row(w0), a1, a2, row(a0), g1, g2)

    pvec = (row(k_k), row(k_a), row(r_k), row(lnx_g), row(lnx_b))
    c_p = _tile(t, WKV_CHUNK, 8)
    nc_p = math.gcd(t // c_p, WKV_PROMPT_CHUNKS_PER_STEP)
    hb_p = math.gcd(heads, WKV_PROMPT_HEADS_PER_STEP)
    zero_state = jnp.zeros((bsz, heads, RWKV_HEAD, RWKV_HEAD), F32)
    c_s = -(-s_len // 8) * 8
    pad = lambda v: jnp.pad(v.reshape(v.shape[:-2] + (db, s_len, d)),
                            [(0, 0)] * (v.ndim - 1) + [(0, c_s - s_len), (0, 0)]
                            ).reshape(v.shape[:-2] + (db * c_s, d))
    z_s, st_s = _wkv(pad(rkv[:, np_tok:]), pad(lw[np_tok:]), pad(a[np_tok:]), pad(g[np_tok:]), *pvec,
                     wkv_s.astype(F32), jnp.zeros((db * c_s, d), BF16), db, c_s, c_s, 1, heads)
    z_s = z_s.reshape(db, c_s, d)[:, :s_len].reshape(db * s_len, d)

    z, st_p = _wkv(rkv, lw, a, g, *pvec, zero_state, jnp.pad(z_s, ((np_tok, 0), (0, 0))), bsz, t, c_p, nc_p, hb_p)
    h = _mm_ln_cols(z, w_out, layer, x, ln_g, ln_b, alpha)
    last_p = x[np.arange(1, bsz + 1) * t - 1]
    last_s = x[np_tok + np.arange(1, db + 1) * s_len - 1]
    return h, st_p, st_s, last_p, last_s


def _mlp(h, w_up, w_down, layer, ln_g, ln_b, alpha):
    d_ff = w_up.shape[2]
    u = _mm(h, w_up, layer, d_ff, act="relu2", out_dtypes=(BF16,))
    return _mm_ln(u, w_down, layer, h, ln_g, ln_b, alpha)


def kernel(x_prompt, x_sample, cache_nsa_kv, state_nsa_win, state_rwkv_wkv, state_rwkv_shift, page_table,
           nsa_w_in, nsa_cmp_pe, nsa_cmp_w1, nsa_cmp_w2, nsa_w_out,
           rwkv_mu, rwkv_w_rkv, rwkv_w0, rwkv_w1, rwkv_w2, rwkv_a0, rwkv_a1, rwkv_a2, rwkv_g1, rwkv_g2,
           rwkv_k_k, rwkv_k_a, rwkv_r_k, rwkv_lnx_g, rwkv_lnx_b, rwkv_w_out,
           ffn_w_up, ffn_w_down, ln_g, ln_b):
    bsz, t, d = x_prompt.shape
    db, s_len, _ = x_sample.shape
    depth = ffn_w_up.shape[0]
    alpha = (2 * depth) ** 0.25
    np_tok = bsz * t
    x = jnp.concatenate([x_prompt.reshape(np_tok, d), x_sample.reshape(db * s_len, d)], axis=0)
    n_l, n_pool = cache_nsa_kv.shape[:2]
    cache5 = cache_nsa_kv.reshape(n_l, n_pool, PAGE_SIZE, 4 * N_KV_HEADS, HEAD_DIM)
    page_table = page_table.astype(jnp.int32)

    kv_p, kv_s, win_p, win_s, wkv_p, wkv_s, sh_p, sh_s = ([] for _ in range(8))
    row = lambda v: v.reshape(1, -1)
    for i in range(depth):
        j = i // 2
        if i % 2 == 0:
            h, kvp_new, kvs_new, wp_new, ws_new = _nsa_layer(
                x, np_tok, bsz, t, db, s_len, cache5, j, state_nsa_win[j], page_table,
                nsa_w_in, nsa_cmp_pe[j], nsa_cmp_w1[j], nsa_cmp_w2[j], nsa_w_out,
                row(ln_g[i, 0]), row(ln_b[i, 0]), alpha)
            kv_p.append(kvp_new)
            kv_s.append(kvs_new)
            win_p.append(wp_new)
            win_s.append(ws_new)
        else:
            h, sp_new, ss_new, hp_new, hs_new = _rwkv_layer(
                x, np_tok, bsz, t, db, s_len, state_rwkv_shift[j], state_rwkv_wkv[j],
                rwkv_mu[j], rwkv_w_rkv[j], rwkv_w0[j], rwkv_w1[j], rwkv_w2[j], rwkv_a0[j], rwkv_a1[j],
                rwkv_a2[j], rwkv_g1[j], rwkv_g2[j], rwkv_k_k[j], rwkv_k_a[j], rwkv_r_k[j],
                rwkv_lnx_g[j], rwkv_lnx_b[j], rwkv_w_out, j, row(ln_g[i, 0]), row(ln_b[i, 0]), alpha)
            wkv_p.append(sp_new)
            wkv_s.append(ss_new)
            sh_p.append(hp_new)
            sh_s.append(hs_new)
        x = _mlp(h, ffn_w_up, ffn_w_down, i, row(ln_g[i, 1]), row(ln_b[i, 1]), alpha)
    return (x[:np_tok].reshape(bsz, t, d), x[np_tok:].reshape(db, s_len, d),
            jnp.stack(kv_p), jnp.stack(kv_s), jnp.stack(win_p), jnp.stack(win_s),
            jnp.stack(wkv_p), jnp.stack(wkv_s), jnp.stack(sh_p), jnp.stack(sh_s))
```

```python
import functools
import math

import jax
import jax.numpy as jnp
import numpy as np
from jax import lax
from jax.experimental import pallas as pl
from jax.experimental.pallas import tpu as pltpu

F32 = jnp.float32
BF16 = jnp.bfloat16

HEAD_DIM = 128
N_KV_HEADS = 4
GROUP = 4
N_HEADS = N_KV_HEADS * GROUP
QW = GROUP * HEAD_DIM
NQ_COLS = N_HEADS * HEAD_DIM
NKV_COLS = N_KV_HEADS * HEAD_DIM
CMP_BLOCK = 32
CMP_STRIDE = 16
SEL_BLOCK = 64
N_SEL = 16
WINDOW = 512
Q_BLOCK = 256
PAGE_SIZE = 128
RWKV_HEAD = 64
LN_EPS = 1e-5
LNX_EPS = 64e-5
FORCE = 1e4
NEG_INF = -1e30
TINY = 1e-30
CHUNKS_PER_PAGE = PAGE_SIZE // CMP_STRIDE

VMEM_LIMIT_BYTES = 56 * 1024 * 1024
HI = lax.Precision.HIGHEST


def _params(*sem):
    return pltpu.CompilerParams(dimension_semantics=sem, vmem_limit_bytes=VMEM_LIMIT_BYTES)


def _tile(n, pref, mult=16):
    if n <= pref:
        return n
    for t in range(pref - pref % mult, 0, -mult):
        if n % t == 0:
            return t
    return n


def _dot(a, b, precision=None):
    return jnp.dot(a, b, preferred_element_type=F32, precision=precision)


def _dot_nt(a, b, precision=None):
    return lax.dot_general(a, b, (((1,), (1,)), ((), ())), preferred_element_type=F32, precision=precision)


def _dot_tn(a, b, precision=None):
    return lax.dot_general(a, b, (((0,), (0,)), ((), ())), preferred_element_type=F32, precision=precision)


def _masked_softmax(s, mask):
    s = jnp.where(mask, s, NEG_INF)
    m = jnp.max(s, axis=-1, keepdims=True)
    p = jnp.where(mask, jnp.exp(s - m), 0.0)
    return p * (1.0 / jnp.maximum(jnp.sum(p, axis=-1, keepdims=True), TINY))


def _layer_norm_rows(s, g, b, eps):
    mu = jnp.mean(s, axis=-1, keepdims=True)
    c = s - mu
    var = jnp.mean(c * c, axis=-1, keepdims=True)
    return c * lax.rsqrt(var + eps) * g + b


def _mm_kernel(x_ref, w_ref, *refs, act):
    out_refs, xb_ref = refs[:-1], refs[-1]

    @pl.when(pl.program_id(1) == 0)
    def _():
        xb_ref[...] = x_ref[...].astype(BF16)

    y = _dot(xb_ref[...], w_ref[...].astype(BF16))
    if act == "relu2":
        y = jnp.maximum(y, 0.0)
        y = y * y
    elif act == "sigmoid":
        y = jax.nn.sigmoid(y)
    for o_ref in out_refs:
        o_ref[...] = y.astype(o_ref.dtype)


def _mm(x, w, layer, n_out, *, act=None, out_dtypes=(F32,), tm_pref=1040, tn_pref=1024):
    m, k = x.shape
    tm = _tile(m, tm_pref)
    tn = _tile(n_out, tn_pref, 128)
    outs = pl.pallas_call(
        functools.partial(_mm_kernel, act=act),
        grid=(m // tm, n_out // tn),
        in_specs=[pl.BlockSpec((tm, k), lambda i, j: (i, 0)),
                  pl.BlockSpec((None, k, tn), lambda i, j: (layer, 0, j))],
        out_specs=[pl.BlockSpec((tm, tn), lambda i, j: (i, j)) for _ in out_dtypes],
        out_shape=[jax.ShapeDtypeStruct((m, n_out), dt) for dt in out_dtypes],
        scratch_shapes=[pltpu.VMEM((tm, k), BF16)],
        compiler_params=_params("parallel", "arbitrary"),
    )(x, w)
    return outs[0] if len(outs) == 1 else outs


def _mm_mix_kernel(x_ref, xp_ref, mu_ref, w_ref, o_ref, xb_ref):
    @pl.when(pl.program_id(2) == 0)
    def _():
        x = x_ref[...]
        xb_ref[...] = (x + (xp_ref[...] - x) * mu_ref[...]).astype(BF16)

    o_ref[...] = _dot(xb_ref[...], w_ref[...].astype(BF16))


def _mm_mix(x, xp, mu, w, *, tm_pref=1040, tn_pref=512):
    m, k = x.shape
    npar, _, n = w.shape
    tm = _tile(m, tm_pref)
    tn = _tile(n, tn_pref, 128)
    return pl.pallas_call(
        _mm_mix_kernel,
        grid=(m // tm, npar, n // tn),
        in_specs=[pl.BlockSpec((tm, k), lambda i, p, j: (i, 0)),
                  pl.BlockSpec((tm, k), lambda i, p, j: (i, 0)),
                  pl.BlockSpec((None, 1, k), lambda i, p, j: (p, 0, 0)),
                  pl.BlockSpec((None, k, tn), lambda i, p, j: (p, 0, j))],
        out_specs=pl.BlockSpec((None, tm, tn), lambda i, p, j: (p, i, j)),
        out_shape=jax.ShapeDtypeStruct((npar, m, n), F32),
        scratch_shapes=[pltpu.VMEM((tm, k), BF16)],
        compiler_params=_params("parallel", "arbitrary", "arbitrary"),
    )(x, xp, mu, w)


def _loras_kernel(x_ref, xp_ref, mu_ref, w1_ref, w2_ref, w0_ref, a1_ref, a2_ref, a0_ref, g1_ref, g2_ref,
                  lw_ref, a_ref, g_ref):
    x = x_ref[...]
    xx = xp_ref[...] - x
    mix = lambda p: (x + xx * mu_ref[p:p + 1, :]).astype(BF16)
    low = lambda xm, w: _dot(xm, w[...].astype(BF16))
    up = lambda h, w: _dot(h.astype(BF16), w[...].astype(BF16))
    u = -(up(jnp.tanh(low(mix(0), w1_ref)), w2_ref) + w0_ref[...])
    softplus = jnp.maximum(u, 0.0) + jnp.log(1.0 + jnp.exp(-jnp.abs(u)))
    lw_ref[...] = -jnp.exp(-softplus - 0.5)
    a_ref[...] = jax.nn.sigmoid(up(low(mix(1), a1_ref), a2_ref) + a0_ref[...])
    g_ref[...] = up(jax.nn.sigmoid(low(mix(2), g1_ref)), g2_ref)


def _loras(x, xp, mu, w1, w2, w0, a1, a2, a0, g1, g2, *, tm_pref=320):
    m, k = x.shape
    n = w2.shape[1]
    tm = _tile(m, tm_pref)
    row = lambda i: (i, 0)
    fixed = lambda i: (0, 0)
    whole = lambda v: pl.BlockSpec(v.shape, fixed)
    out = jax.ShapeDtypeStruct((m, n), F32)
    return pl.pallas_call(
        _loras_kernel,
        grid=(m // tm,),
        in_specs=[pl.BlockSpec((tm, k), row), pl.BlockSpec((tm, k), row), whole(mu),
                  whole(w1), whole(w2), whole(w0), whole(a1), whole(a2), whole(a0), whole(g1), whole(g2)],
        out_specs=[pl.BlockSpec((tm, n), row)] * 3,
        out_shape=[out, out, out],
        compiler_params=_params("parallel"),
    )(x, xp, mu, w1, w2, w0, a1, a2, a0, g1, g2)


def _mm_ln_kernel(z_ref, w_ref, res_ref, g_ref, b_ref, o_ref, *, alpha):
    kk = pl.program_id(1)

    @pl.when(kk == 0)
    def _():
        o_ref[...] = _dot(z_ref[...].astype(BF16), w_ref[...].astype(BF16))

    @pl.when(kk > 0)
    def _():
        o_ref[...] += _dot(z_ref[...].astype(BF16), w_ref[...].astype(BF16))

    @pl.when(kk == pl.num_programs(1) - 1)
    def _():
        s = alpha * res_ref[...] + o_ref[...]
        o_ref[...] = _layer_norm_rows(s, g_ref[...], b_ref[...], LN_EPS)


def _mm_ln_cols_kernel(z_ref, w_ref, res_ref, g_ref, b_ref, o_ref, *, alpha, tn):
    j = pl.program_id(1)
    col = pl.multiple_of(j * tn, tn)
    o_ref[:, pl.ds(col, tn)] = _dot(z_ref[...], w_ref[...].astype(BF16))

    @pl.when(j == pl.num_programs(1) - 1)
    def _():
        s = alpha * res_ref[...] + o_ref[...]
        o_ref[...] = _layer_norm_rows(s, g_ref[...], b_ref[...], LN_EPS)


def _mm_ln_cols(z, w, layer, res, g, b, alpha, *, tm_pref=832, tn_pref=512):
    m, k = z.shape
    n = w.shape[2]
    tm = _tile(m, tm_pref)
    tn = _tile(n, tn_pref, 128)
    return pl.pallas_call(
        functools.partial(_mm_ln_cols_kernel, alpha=alpha, tn=tn),
        grid=(m // tm, n // tn),
        in_specs=[pl.BlockSpec((tm, k), lambda i, j: (i, 0)),
                  pl.BlockSpec((None, k, tn), lambda i, j: (layer, 0, j)),
                  pl.BlockSpec((tm, n), lambda i, j: (i, 0)),
                  pl.BlockSpec((1, n), lambda i, j: (0, 0)),
                  pl.BlockSpec((1, n), lambda i, j: (0, 0))],
        out_specs=pl.BlockSpec((tm, n), lambda i, j: (i, 0)),
        out_shape=jax.ShapeDtypeStruct((m, n), F32),
        compiler_params=_params("parallel", "arbitrary"),
    )(z, w, res, g, b)


def _mm_ln(z, w, layer, res, g, b, alpha, *, tm_pref=640, tk_pref=1024):
    m, k = z.shape
    n = w.shape[2]
    tm = _tile(m, tm_pref)
    tk = _tile(k, tk_pref, 128)
    return pl.pallas_call(
        functools.partial(_mm_ln_kernel, alpha=alpha),
        grid=(m // tm, k // tk),
        in_specs=[pl.BlockSpec((tm, tk), lambda i, kk: (i, kk)),
                  pl.BlockSpec((None, tk, n), lambda i, kk: (layer, kk, 0)),
                  pl.BlockSpec((tm, n), lambda i, kk: (i, 0)),
                  pl.BlockSpec((1, n), lambda i, kk: (0, 0)),
                  pl.BlockSpec((1, n), lambda i, kk: (0, 0))],
        out_specs=pl.BlockSpec((tm, n), lambda i, kk: (i, 0)),
        out_shape=jax.ShapeDtypeStruct((m, n), F32),
        compiler_params=_params("parallel", "arbitrary"),
    )(z, w, res, g, b)


def _cmp_bias_kernel(pe_ref, w1_ref, o_ref):
    acc = jnp.zeros((1, HEAD_DIM), F32)
    for rj in range(CMP_BLOCK):
        acc = acc + _dot(pe_ref[rj:rj + 1, :], w1_ref[rj], HI)
    o_ref[...] = acc


def _cmp_bias(pe, w1):
    return pl.pallas_call(
        _cmp_bias_kernel,
        grid=(2,),
        in_specs=[pl.BlockSpec((None, CMP_BLOCK, HEAD_DIM), lambda p: (p, 0, 0)),
                  pl.BlockSpec((None, CMP_BLOCK, HEAD_DIM, HEAD_DIM), lambda p: (p, 0, 0, 0))],
        out_specs=pl.BlockSpec((None, 1, HEAD_DIM), lambda p: (p, 0, 0)),
        out_shape=jax.ShapeDtypeStruct((2, 1, HEAD_DIM), F32),
        compiler_params=_params("parallel"),
    )(pe, w1)


def _cmp_kernel(*refs, n_in):
    pages = refs[-5 - n_in:-5]
    wc_ref, bias_ref, w2_ref, out_ref, carry_ref = refs[-5:]
    mk = CHUNKS_PER_PAGE * n_in
    rows = N_KV_HEADS * mk

    @pl.when(pl.program_id(1) == 0)
    def _():
        carry_ref[...] = jnp.zeros_like(carry_ref)

    first = (lax.broadcasted_iota(jnp.int32, (rows, 1), 0) % mk) == 0
    if len(pages[0].shape) == 3:
        split = [pages[i][...].reshape(CHUNKS_PER_PAGE, CMP_STRIDE, 2 * N_KV_HEADS, HEAD_DIM)
                 for i in range(n_in)]
        by_slot = [[jnp.swapaxes(split[i][:, j], 0, 1) for j in range(CMP_STRIDE)] for i in range(n_in)]
        piece = lambda i, j, slot: by_slot[i][j][slot]
    else:
        r_out = lax.broadcasted_iota(jnp.int32, (PAGE_SIZE, PAGE_SIZE), 0)
        r_in = lax.broadcasted_iota(jnp.int32, (PAGE_SIZE, PAGE_SIZE), 1)
        perm = jnp.where(r_in == (r_out % CHUNKS_PER_PAGE) * CMP_STRIDE + r_out // CHUNKS_PER_PAGE, 1.0, 0.0)
        grouped = [_dot(perm.astype(BF16), pages[i][...].astype(BF16)) for i in range(n_in)]
        piece = lambda i, j, slot: grouped[i][j * CHUNKS_PER_PAGE:(j + 1) * CHUNKS_PER_PAGE,
                                              slot * HEAD_DIM:(slot + 1) * HEAD_DIM]
    for p in range(2):
        lhs = jnp.concatenate(
            [jnp.concatenate([piece(i, j, p * N_KV_HEADS + k).astype(BF16) for j in range(CMP_STRIDE)], axis=1)
             for k in range(N_KV_HEADS) for i in range(n_in)], axis=0)
        acc = _dot(lhs, wc_ref[p])
        part0 = acc[:, :HEAD_DIM]
        part1 = acc[:, HEAD_DIM:]
        prev = jnp.where(first, carry_ref[p], pltpu.roll(part0, 1, 0))
        carry_ref[p] = pltpu.roll(part0, rows - (mk - 1), 0)
        h = prev + part1 + bias_ref[p]
        c = _dot(jax.nn.gelu(h).astype(BF16), w2_ref[p])
        for k in range(N_KV_HEADS):
            col = (p * N_KV_HEADS + k) * HEAD_DIM
            out_ref[:, col:col + HEAD_DIM] = c[k * mk:(k + 1) * mk]


def _cmp_weights(w1, w2):
    n_r = CMP_BLOCK // CMP_STRIDE
    w1r = w1.reshape(2, n_r, CMP_STRIDE, HEAD_DIM, HEAD_DIM)
    wc = jnp.concatenate([w1r[:, r] for r in range(n_r)], axis=-1).astype(BF16)
    return wc.reshape(2, CMP_STRIDE * HEAD_DIM, 2 * HEAD_DIM), w2.astype(BF16)


def _cmp_call(n_seq, n_steps, n_in, page_specs, operands, wc, bias, w2b, num_prefetch, prefetch):
    mk = CHUNKS_PER_PAGE * n_in
    if num_prefetch:
        fixed3 = lambda s, t, pt: (0, 0, 0)
        out_map = lambda s, t, pt: (s, t, 0)
    else:
        fixed3 = lambda s, t: (0, 0, 0)
        out_map = lambda s, t: (s, t, 0)
    grid_spec = pltpu.PrefetchScalarGridSpec(
        num_scalar_prefetch=num_prefetch,
        grid=(n_seq, n_steps),
        in_specs=page_specs + [
            pl.BlockSpec((2, CMP_STRIDE * HEAD_DIM, 2 * HEAD_DIM), fixed3),
            pl.BlockSpec((2, 1, HEAD_DIM), fixed3),
            pl.BlockSpec((2, HEAD_DIM, HEAD_DIM), fixed3)],
        out_specs=pl.BlockSpec((None, mk, 2 * NKV_COLS), out_map),
        scratch_shapes=[pltpu.VMEM((2, N_KV_HEADS * mk, HEAD_DIM), F32)],
    )
    return pl.pallas_call(
        functools.partial(_cmp_kernel, n_in=n_in),
        grid_spec=grid_spec,
        out_shape=jax.ShapeDtypeStruct((n_seq, n_steps * mk, 2 * NKV_COLS), F32),
        compiler_params=_params("parallel", "arbitrary"),
    )(*prefetch, *operands, wc, bias, w2b)


CMP_PAGES_PER_STEP = 8


def _cmp_prompt(y_main, bsz, t, wc, bias, w2b):
    pages_per_seq = t // PAGE_SIZE
    n_in = math.gcd(pages_per_seq, CMP_PAGES_PER_STEP)
    n_steps = pages_per_seq // n_in
    col_blk = NQ_COLS // (2 * NKV_COLS)
    specs = [pl.BlockSpec((PAGE_SIZE, 2 * NKV_COLS),
                          functools.partial(lambda s, st, i: (s * pages_per_seq + st * n_in + i, col_blk), i=i))
             for i in range(n_in)]
    return _cmp_call(bsz, n_steps, n_in, specs, [y_main] * n_in, wc, bias, w2b, 0, ())


def _cmp_sample(cache5, layer, page_table, wc, bias, w2b):
    n_in = math.gcd(page_table.shape[1], CMP_PAGES_PER_STEP)
    db, n_pages = page_table.shape
    n_steps = n_pages // n_in
    specs = [pl.BlockSpec((None, None, PAGE_SIZE, 2 * N_KV_HEADS, HEAD_DIM),
                          functools.partial(lambda s, st, pt, i: (layer, pt[s, st * n_in + i], 0, 0, 0), i=i))
             for i in range(n_in)]
    return _cmp_call(db, n_steps, n_in, specs, [cache5] * n_in, wc, bias, w2b, 1, (page_table,))


def _overlap_shifted(n_rows, n_c, n_s, n_cols):
    m = np.arange(n_rows)[:, None]
    j = np.arange(n_cols)[None, :]
    c0 = (m - 1) * CMP_STRIDE
    s0 = j * SEL_BLOCK
    ov = (c0 < s0 + SEL_BLOCK) & (c0 + CMP_BLOCK > s0) & (m >= 1) & (m <= n_c) & (j < n_s)
    return jnp.asarray(ov.astype(np.float32))


def _select_blocks(imp, qpos, n_s, n_cols):
    j = lax.broadcasted_iota(jnp.int32, (1, n_cols), 1)
    jf = j.astype(F32)
    cur = qpos // SEL_BLOCK
    forced = (j == 0) | (j == cur) | (j == cur - 1)
    valid = j * SEL_BLOCK <= qpos
    score = jnp.where(forced, FORCE, jnp.where(valid, imp, -FORCE))
    score = jnp.where(j < n_s, score, -jnp.inf)
    sel = jnp.zeros(score.shape, F32)
    for _ in range(min(N_SEL, n_s)):
        mx = jnp.max(score, axis=-1, keepdims=True)
        idx = jnp.min(jnp.where(score == mx, jf, float(n_cols)), axis=-1, keepdims=True)
        hit = jf == idx
        sel = jnp.where(hit, 1.0, sel)
        score = jnp.where(hit, -jnp.inf, score)
    return sel


SEL_KEY_TILE = 512


def _softmax_cols(s, mask):
    s = jnp.where(mask, s, NEG_INF)
    m = jnp.max(s, axis=0, keepdims=True)
    p = jnp.where(mask, jnp.exp2(s - m), 0.0)
    return p * (1.0 / jnp.maximum(jnp.sum(p, axis=0, keepdims=True), TINY))


def _select_blocks_cols(imp, qpos, n_s):
    n_rows = imp.shape[0]
    j = lax.broadcasted_iota(jnp.int32, (n_rows, 1), 0)
    cur = qpos // SEL_BLOCK
    forced = (j == 0) | (j == cur) | (j == cur - 1)
    valid = j * SEL_BLOCK <= qpos
    score = jnp.where(forced, FORCE, jnp.where(valid, imp, -FORCE))
    score = jnp.where(j < n_s, score, -jnp.inf)
    sub = 8
    blocks = [score[b * sub:(b + 1) * sub] for b in range(-(-n_s // sub))]
    ranks = [jnp.zeros(blk.shape, F32) for blk in blocks]
    row = lax.broadcasted_iota(jnp.int32, (sub, 1), 0)
    for i in range(n_s):
        bi, ri = divmod(i, sub)
        si = blocks[bi][ri:ri + 1, :]
        for b, blk in enumerate(blocks):
            if b < bi:
                beats = si > blk
            elif b > bi:
                beats = si >= blk
            else:
                beats = jnp.where(row > ri, jnp.where(si >= blk, 1.0, 0.0), jnp.where(si > blk, 1.0, 0.0)) > 0.5
            ranks[b] = ranks[b] + jnp.where(beats, 1.0, 0.0)
    rank = jnp.concatenate(ranks, axis=0)
    sel = jnp.where(rank < float(min(N_SEL, n_s)), 1.0, 0.0)
    if sel.shape[0] < n_rows:
        sel = jnp.concatenate([sel, jnp.zeros((n_rows - sel.shape[0], sel.shape[1]), F32)], axis=0)
    return sel


def _nsa_prompt_kernel(q_ref, kc_ref, vct_ref, ks_ref, vst_ref, kw_ref, vwt_ref, g_ref, ovt_ref, rest_ref, o_ref,
                       selbias_ref, *, t):
    del rest_ref
    start = pl.program_id(2) * Q_BLOCK
    n_cols = GROUP * Q_BLOCK
    q = q_ref[...] * (HEAD_DIM ** -0.5 * math.log2(math.e))
    q4 = jnp.concatenate([q[:, g * HEAD_DIM:(g + 1) * HEAD_DIM] for g in range(GROUP)], axis=0).astype(BF16)
    qpos1 = start + lax.broadcasted_iota(jnp.int32, (1, Q_BLOCK), 1)
    qpos4 = start + lax.broadcasted_iota(jnp.int32, (1, n_cols), 1) % Q_BLOCK

    n_cr = kc_ref.shape[0]
    n_s = ovt_ref.shape[0]
    s = _dot_nt(kc_ref[...].astype(BF16), q4)
    m_idx = lax.broadcasted_iota(jnp.int32, (n_cr, 1), 0)
    cmask = (m_idx >= 1) & (m_idx * CMP_STRIDE + (CMP_BLOCK - CMP_STRIDE - 1) <= qpos4)
    p_c = _softmax_cols(s, cmask)
    o_c = _dot(vct_ref[...].astype(BF16), p_c.astype(BF16))
    psum = p_c[:, 0:Q_BLOCK]
    for g in range(1, GROUP):
        psum = psum + p_c[:, g * Q_BLOCK:(g + 1) * Q_BLOCK]
    imp = _dot(ovt_ref[...], psum, HI)
    sel = _select_blocks_cols(imp, qpos1, n_s)

    wlen = WINDOW + Q_BLOCK
    w0 = pl.multiple_of(jnp.clip(start - WINDOW, 0, t - wlen), PAGE_SIZE)
    sw = _dot_nt(kw_ref[pl.ds(w0, wlen), :], q4)
    dpos = qpos4 - (w0 + lax.broadcasted_iota(jnp.int32, (wlen, 1), 0))
    p_w = _softmax_cols(sw, (dpos >= 0) & (dpos < WINDOW))
    wblk = w0 // PAGE_SIZE
    vwt = jnp.concatenate([vwt_ref[wblk + i] for i in range(wlen // PAGE_SIZE)], axis=1)
    o_w = _dot(vwt, p_w.astype(BF16))

    tk = SEL_KEY_TILE
    blocks_per_tile = tk // SEL_BLOCK
    selbias_ref[...] = jnp.where(sel > 0.5, 0.0, NEG_INF)

    def scores(kt):
        return _dot_nt(ks_ref[pl.ds(pl.multiple_of(kt * tk, tk), tk), :], q4)

    def tile(kt, carry, last):
        m, l, acc, sc = carry
        sc_next = sc if last else scores(kt + 1)
        off = kt * tk
        rows = selbias_ref[pl.ds(pl.multiple_of(kt * blocks_per_tile, blocks_per_tile), blocks_per_tile), :]
        bias = jnp.concatenate([jnp.broadcast_to(rows[i:i + 1, :], (SEL_BLOCK, Q_BLOCK))
                                for i in range(blocks_per_tile)], axis=0)
        if last:
            kpos = off + lax.broadcasted_iota(jnp.int32, (tk, Q_BLOCK), 0)
            bias = jnp.where(kpos <= qpos1, bias, NEG_INF)
        sc = sc + jnp.concatenate([bias] * GROUP, axis=1)
        m_new = jnp.maximum(m, jnp.max(sc, axis=0, keepdims=True))
        alpha = jnp.exp2(m - m_new)
        p = jnp.exp2(sc - m_new)
        l = alpha * l + jnp.sum(p, axis=0, keepdims=True)
        vblk = kt * (tk // PAGE_SIZE)
        vst = jnp.concatenate([vst_ref[vblk + i] for i in range(tk // PAGE_SIZE)], axis=1)
        acc = alpha * acc + _dot(vst, p.astype(BF16))
        return m_new, l, acc, sc_next

    n_tiles = (start + Q_BLOCK + tk - 1) // tk
    m0 = jnp.full((1, n_cols), NEG_INF, F32)
    l0 = jnp.zeros((1, n_cols), F32)
    a0 = jnp.zeros((HEAD_DIM, n_cols), F32)
    carry = lax.fori_loop(0, n_tiles - 1, functools.partial(tile, last=False), (m0, l0, a0, scores(0)))
    _, l, acc, _ = tile(n_tiles - 1, carry, last=True)
    o_s = acc * (1.0 / jnp.maximum(l, TINY))

    gt = g_ref[...]
    o = gt[0:1, :] * o_c + gt[1:2, :] * o_s + gt[2:3, :] * o_w
    o_ref[...] = jnp.concatenate([o[:, g * Q_BLOCK:(g + 1) * Q_BLOCK].T for g in range(GROUP)],
                                 axis=1).astype(o_ref.dtype)


def _values_t(y_b, bsz, t, col0):
    v = lax.slice(y_b, (0, col0), (bsz * t, col0 + NKV_COLS))
    v = v.reshape(bsz, t // PAGE_SIZE, PAGE_SIZE, N_KV_HEADS, HEAD_DIM)
    return v.transpose(0, 3, 1, 4, 2)


def _nsa_prompt(y_main, y_b, cmp_p, gates, bsz, t, rest):
    assert t % SEL_KEY_TILE == 0 and t >= WINDOW + Q_BLOCK and SEL_KEY_TILE % PAGE_SIZE == 0
    np_tok = bsz * t
    nq = t // Q_BLOCK
    n_cr = t // CMP_STRIDE
    n_c = (t - CMP_BLOCK) // CMP_STRIDE + 1
    n_s = t // SEL_BLOCK
    assert n_s >= N_SEL and n_s % 8 == 0
    ovt = _overlap_shifted(n_cr, n_c, n_s, n_s).T
    n_pg = t // PAGE_SIZE
    v_sel_t = _values_t(y_b, bsz, t, NQ_COLS + 3 * NKV_COLS)
    v_win_t = _values_t(y_b, bsz, t, NQ_COLS + 5 * NKV_COLS)
    cmp_t = cmp_p.transpose(0, 2, 1)
    cb = NQ_COLS // HEAD_DIM
    g_t = gates[:np_tok].reshape(bsz, nq, Q_BLOCK, N_KV_HEADS, GROUP, 3).transpose(0, 3, 1, 5, 4, 2)
    g_t = g_t.reshape(bsz, N_KV_HEADS, nq, 3, GROUP * Q_BLOCK)
    seq_key = lambda c4: pl.BlockSpec((t, HEAD_DIM), lambda b, k, i: (b, cb + c4 * N_KV_HEADS + k))
    seq_val = pl.BlockSpec((None, None, n_pg, HEAD_DIM, PAGE_SIZE), lambda b, k, i: (b, k, 0, 0, 0))
    return pl.pallas_call(
        functools.partial(_nsa_prompt_kernel, t=t),
        grid=(bsz, N_KV_HEADS, nq),
        in_specs=[pl.BlockSpec((Q_BLOCK, QW), lambda b, k, i: (b * nq + i, k)),
                  pl.BlockSpec((None, n_cr, HEAD_DIM), lambda b, k, i: (b, 0, k)),
                  pl.BlockSpec((None, HEAD_DIM, n_cr), lambda b, k, i: (b, N_KV_HEADS + k, 0)),
                  seq_key(2), seq_val, seq_key(4), seq_val,
                  pl.BlockSpec((None, None, None, 3, GROUP * Q_BLOCK), lambda b, k, i: (b, k, i, 0, 0)),
                  pl.BlockSpec((n_s, n_cr), lambda b, k, i: (0, 0)),
                  pl.BlockSpec(memory_space=pl.ANY)],
        out_specs=pl.BlockSpec((Q_BLOCK, QW), lambda b, k, i: (b * nq + i, k)),
        out_shape=jax.ShapeDtypeStruct(rest.shape, rest.dtype),
        input_output_aliases={9: 0},
        scratch_shapes=[pltpu.VMEM((n_s, Q_BLOCK), F32)],
        compiler_params=_params("parallel", "parallel", "arbitrary"),
    )(y_main, cmp_p, cmp_t, y_b, v_sel_t, y_b, v_win_t, g_t, ovt, rest)


SAMPLE_ROWS = 8


def _nsa_sample_kernel(pt_ref, y_ref, cmp_ref, cache_ref, win_ref, g_ref, ov_ref, o_ref,
                       q_scr, sel_scr, m_scr, l_scr, acc_scr, oc_scr, kv_buf, kv_sem,
                       *, layer, past, s_len, n_s, n_in):
    pg = pl.program_id(1)
    n_pages = pl.num_programs(1)
    step = pl.program_id(0) * n_pages + pg
    cur = step % 2

    def page_copies(seq, grp, buf):
        out = []
        for i in range(n_in):
            page = pt_ref[seq, grp * n_in + i]
            for s in range(2 * N_KV_HEADS):
                out.append(pltpu.make_async_copy(
                    cache_ref.at[layer, page, :, 2 * N_KV_HEADS + s, :],
                    kv_buf.at[buf, s, pl.ds(i * PAGE_SIZE, PAGE_SIZE), :],
                    kv_sem.at[buf]))
        return out

    @pl.when(step == 0)
    def _():
        for cp in page_copies(0, 0, 0):
            cp.start()

    @pl.when(step + 1 < pl.num_programs(0) * n_pages)
    def _():
        nxt = step + 1
        for cp in page_copies(nxt // n_pages, nxt % n_pages, 1 - cur):
            cp.start()

    for cp in page_copies(pl.program_id(0), pg, cur):
        cp.wait()

    sr = SAMPLE_ROWS
    rk = GROUP * sr
    scale = HEAD_DIM ** -0.5
    tpos = lax.broadcasted_iota(jnp.int32, (rk, 1), 0) % sr
    qpos = past + tpos
    n_sc = sel_scr.shape[1]
    nbp = past // SEL_BLOCK
    kv0 = NQ_COLS

    @pl.when(pg == 0)
    def _():
        y = y_ref[...]
        n_cr = cmp_ref.shape[0]
        m_idx = lax.broadcasted_iota(jnp.int32, (1, n_cr), 1)
        cmask = (m_idx >= 1) & (m_idx * CMP_STRIDE + (CMP_BLOCK - CMP_STRIDE - 1) <= qpos)
        psums = []
        for k in range(N_KV_HEADS):
            qk = jnp.concatenate([y[:, (k * GROUP + g) * HEAD_DIM:(k * GROUP + g + 1) * HEAD_DIM]
                                  for g in range(GROUP)], axis=0) * scale
            qk = qk.astype(BF16)
            q_scr[k] = qk
            kc = cmp_ref[:, k * HEAD_DIM:(k + 1) * HEAD_DIM].astype(BF16)
            vc = cmp_ref[:, (N_KV_HEADS + k) * HEAD_DIM:(N_KV_HEADS + k + 1) * HEAD_DIM].astype(BF16)
            p_c = _masked_softmax(_dot_nt(qk, kc), cmask)
            oc_scr[k * rk:(k + 1) * rk, :] = _dot(p_c.astype(BF16), vc)
            psum = p_c[0:sr]
            for g in range(1, GROUP):
                psum = psum + p_c[g * sr:(g + 1) * sr]
            psums.append(psum)
        lanes = 128
        psum_all = jnp.concatenate(psums + [jnp.zeros((lanes - N_KV_HEADS * sr, psums[0].shape[1]), F32)], axis=0)
        imp_t = _dot_nt(ov_ref[...], psum_all, HI)
        qpos_sel = past + lax.broadcasted_iota(jnp.int32, (1, lanes), 1) % sr
        sel = _select_blocks_cols(imp_t, qpos_sel, n_s).T
        for k in range(N_KV_HEADS):
            for g in range(GROUP):
                sel_scr[k * rk + g * sr:k * rk + (g + 1) * sr, :] = sel[k * sr:(k + 1) * sr]
        m_scr[...] = jnp.full(m_scr.shape, NEG_INF, F32)
        l_scr[...] = jnp.zeros(l_scr.shape, F32)
        acc_scr[...] = jnp.zeros(acc_scr.shape, F32)

    def online_update(k, sc, mask, v):
        rows = slice(k * rk, (k + 1) * rk)
        sc = jnp.where(mask, sc, NEG_INF)
        m_old = m_scr[rows, :]
        m_new = jnp.maximum(m_old, jnp.max(sc, axis=-1, keepdims=True))
        alpha = jnp.exp(m_old - m_new)
        p = jnp.where(mask, jnp.exp(sc - m_new), 0.0)
        l_scr[rows, :] = alpha * l_scr[rows, :] + jnp.sum(p, axis=-1, keepdims=True)
        acc_scr[rows, :] = alpha * acc_scr[rows, :] + _dot(p.astype(v.dtype), v)
        m_scr[rows, :] = m_new

    n_keys = n_in * PAGE_SIZE
    kpos = pg * n_keys + lax.broadcasted_iota(jnp.int32, (1, n_keys), 1)
    blk_row = lax.broadcasted_iota(jnp.int32, (n_sc, 1), 0)
    expand = jnp.where(blk_row == kpos // SEL_BLOCK, 1.0, 0.0).astype(BF16)
    selx = _dot(sel_scr[...].astype(BF16), expand)
    for k in range(N_KV_HEADS):
        kp = kv_buf[cur, k].astype(BF16)
        vp = kv_buf[cur, N_KV_HEADS + k].astype(BF16)
        mask = (selx[k * rk:(k + 1) * rk] > 0.5) & (kpos <= qpos)
        online_update(k, _dot_nt(q_scr[k], kp), mask, vp)

    @pl.when(pg == n_pages - 1)
    def _():
        y = y_ref[...]
        rpos = lax.broadcasted_iota(jnp.int32, (1, sr), 1)
        new_ok = (rpos < s_len) & (rpos <= tpos)
        gt = g_ref[...]
        win_by_slot = jnp.swapaxes(win_ref[...], 0, 1).astype(BF16)
        for k in range(N_KV_HEADS):
            rows = slice(k * rk, (k + 1) * rk)
            qk = q_scr[k]
            qf = qk.astype(F32)
            c_ks = kv0 + (2 * N_KV_HEADS + k) * HEAD_DIM
            c_vs = kv0 + (3 * N_KV_HEADS + k) * HEAD_DIM
            kn = y[:, c_ks:c_ks + HEAD_DIM]
            vn = y[:, c_vs:c_vs + HEAD_DIM]
            new_sel = sel_scr[rows, nbp:nbp + 1] > 0.5
            online_update(k, _dot_nt(qf, kn), new_sel & new_ok, vn)
            o_s = acc_scr[rows, :] / jnp.maximum(l_scr[rows, :], TINY)
            wb = win_ref.shape[0]
            c_kw = kv0 + (4 * N_KV_HEADS + k) * HEAD_DIM
            c_vw = kv0 + (5 * N_KV_HEADS + k) * HEAD_DIM
            kwb = win_by_slot[k]
            vwb = win_by_slot[N_KV_HEADS + k]
            kwn = y[:, c_kw:c_kw + HEAD_DIM]
            vwn = y[:, c_vw:c_vw + HEAD_DIM]
            dpos = qpos - (past - wb + lax.broadcasted_iota(jnp.int32, (1, wb), 1))
            mask_b = (dpos >= 0) & (dpos < WINDOW)
            s_b = jnp.where(mask_b, _dot_nt(qk, kwb), NEG_INF)
            s_n = jnp.where(new_ok, _dot_nt(qf, kwn), NEG_INF)
            mw = jnp.maximum(jnp.max(s_b, axis=-1, keepdims=True), jnp.max(s_n, axis=-1, keepdims=True))
            p_b = jnp.where(mask_b, jnp.exp(s_b - mw), 0.0)
            p_n = jnp.where(new_ok, jnp.exp(s_n - mw), 0.0)
            den = jnp.sum(p_b, axis=-1, keepdims=True) + jnp.sum(p_n, axis=-1, keepdims=True)
            o_w = (_dot(p_b.astype(BF16), vwb) + _dot(p_n, vwn)) / jnp.maximum(den, TINY)
            o_c = oc_scr[rows, :]
            for g in range(GROUP):
                r0, r1 = g * sr, (g + 1) * sr
                c = (k * GROUP + g) * 3
                o = gt[:, c:c + 1] * o_c[r0:r1] + gt[:, c + 1:c + 2] * o_s[r0:r1] + gt[:, c + 2:c + 3] * o_w[r0:r1]
                o_ref[:, (k * GROUP + g) * HEAD_DIM:(k * GROUP + g + 1) * HEAD_DIM] = o


SAMPLE_PAGES_PER_STEP = 8


def _nsa_sample(ys8, cmp_s, cache5, layer, win4, gates8, page_table, s_len):
    db, n_pages = page_table.shape
    past = n_pages * PAGE_SIZE
    n_in = math.gcd(n_pages, SAMPLE_PAGES_PER_STEP)
    assert s_len <= SAMPLE_ROWS and s_len <= SEL_BLOCK and past % SEL_BLOCK == 0
    n_cr = cmp_s.shape[1]
    n_c = (past + s_len - CMP_BLOCK) // CMP_STRIDE + 1
    assert n_c + 1 <= n_cr
    n_s = -(-(past + s_len) // SEL_BLOCK)
    n_sc = -(-n_s // 128) * 128
    ov = _overlap_shifted(n_cr, n_c, n_s, n_sc).T
    wb = win4.shape[1]
    ncol = ys8.shape[2]
    rows = N_KV_HEADS * GROUP * SAMPLE_ROWS
    grid_spec = pltpu.PrefetchScalarGridSpec(
        num_scalar_prefetch=1,
        grid=(db, n_pages // n_in),
        in_specs=[pl.BlockSpec((None, SAMPLE_ROWS, ncol), lambda b, p, pt: (b, 0, 0)),
                  pl.BlockSpec((None, n_cr, 2 * NKV_COLS), lambda b, p, pt: (b, 0, 0)),
                  pl.BlockSpec(memory_space=pl.ANY),
                  pl.BlockSpec((None, wb, 2 * N_KV_HEADS, HEAD_DIM), lambda b, p, pt: (b, 0, 0, 0)),
                  pl.BlockSpec((None, SAMPLE_ROWS, 3 * N_HEADS), lambda b, p, pt: (b, 0, 0)),
                  pl.BlockSpec((n_sc, n_cr), lambda b, p, pt: (0, 0))],
        out_specs=pl.BlockSpec((None, SAMPLE_ROWS, NQ_COLS), lambda b, p, pt: (b, 0, 0)),
        scratch_shapes=[pltpu.VMEM((N_KV_HEADS, GROUP * SAMPLE_ROWS, HEAD_DIM), BF16),
                        pltpu.VMEM((rows, n_sc), F32),
                        pltpu.VMEM((rows, 1), F32),
                        pltpu.VMEM((rows, 1), F32),
                        pltpu.VMEM((rows, HEAD_DIM), F32),
                        pltpu.VMEM((rows, HEAD_DIM), F32),
                        pltpu.VMEM((2, 2 * N_KV_HEADS, n_in * PAGE_SIZE, HEAD_DIM), F32),
                        pltpu.SemaphoreType.DMA((2,))],
    )
    return pl.pallas_call(
        functools.partial(_nsa_sample_kernel, layer=layer, past=past, s_len=s_len, n_s=n_s, n_in=n_in),
        grid_spec=grid_spec,
        out_shape=jax.ShapeDtypeStruct((db, SAMPLE_ROWS, NQ_COLS), F32),
        compiler_params=_params("arbitrary", "arbitrary"),
    )(page_table, ys8, cmp_s, cache5, win4, gates8, ov)


WKV_DIAG_BLOCK = 16
WKV_CHUNK = 64


def _bdot(a, b):
    return _dot(a.astype(BF16), b.astype(BF16))


def _bdot_nt(a, b):
    return _dot_nt(a.astype(BF16), b.astype(BF16))


def _bdot_tn(a, b):
    return _dot_tn(a.astype(BF16), b.astype(BF16))


def _unit_lower_inverses(mats, c):
    blk = min(WKV_DIAG_BLOCK, c)
    ri = lax.broadcasted_iota(jnp.int32, (c, c), 0)
    ci = lax.broadcasted_iota(jnp.int32, (c, c), 1)
    eye = jnp.where(ri == ci, 1.0, 0.0)
    same = (ri // blk) == (ci // blk)
    d = [jnp.where(same, a, 0.0) for a in mats]
    td = [eye + x for x in d]
    n_sq = int(math.log2(blk)) - 1
    pw = [_bdot(x, x) for x in d]
    stack = c >= WKV_DIAG_BLOCK
    for level in range(n_sq):
        if level + 1 < n_sq and stack:
            both = [_bdot(jnp.concatenate([p, t], axis=0), p) for p, t in zip(pw, td)]
            pw = [x[:c] for x in both]
            td = [t + x[c:] for t, x in zip(td, both)]
        else:
            td = [t + _bdot(t, p) for t, p in zip(td, pw)]
            if level + 1 < n_sq:
                pw = [_bdot(p, p) for p in pw]
    if blk == c:
        return td
    npow = [_bdot(t, jnp.where(same, 0.0, a)) for t, a in zip(td, mats)]
    tinv = [t + _bdot(x, t) for x, t in zip(npow, td)]
    for _ in range(int(math.log2(c // blk)) - 1):
        npow = [_bdot(x, x) for x in npow]
        tinv = [t + _bdot(x, t) for x, t in zip(npow, tinv)]
    return tinv


def _wkv_kernel(r_ref, k_ref, v_ref, lw_ref, a_ref, g_ref, kk_ref, ka_ref, rk_ref, lg_ref, lb_ref, s0_ref,
                rest_ref, z_ref, s_ref, *, c, nc, hb):
    del rest_ref
    n = RWKV_HEAD

    @pl.when(pl.program_id(2) == 0)
    def _():
        s_ref[...] = s0_ref[...]

    ri = lax.broadcasted_iota(jnp.int32, (c, c), 0)
    ci = lax.broadcasted_iota(jnp.int32, (c, c), 1)
    strict = ri > ci
    incl = ri >= ci
    tri = jnp.where(incl, 1.0, 0.0)
    lw_all = lw_ref[...]
    cum = jnp.concatenate([_dot(tri, lw_all[i * c:(i + 1) * c], HI) for i in range(nc)], axis=0)
    e_in_all = jnp.exp(cum)
    e_neg_all = jnp.exp(-cum)
    e_ex_all = jnp.exp(cum - lw_all)
    heads = range(hb)
    chunks = range(nc)
    sls = [slice(hh * n, (hh + 1) * n) for hh in heads]
    rws = [slice(i * c, (i + 1) * c) for i in chunks]
    r, v, k2, r_t, a_t, b_t, k_t, vb = ([] for _ in range(8))
    for sl in sls:
        k = k_ref[:, sl]
        a = a_ref[:, sl]
        kk = k * kk_ref[:, sl]
        kk = kk * (1.0 / jnp.maximum(jnp.sqrt(jnp.sum(kk * kk, axis=-1, keepdims=True)), 1e-12))
        r.append(r_ref[:, sl])
        v.append(v_ref[:, sl])
        k2.append(k * (1.0 + (a - 1.0) * ka_ref[:, sl]))
        r_t.append(r[-1] * e_in_all[:, sl])
        a_t.append(-kk * e_ex_all[:, sl])
        b_t.append(kk * a * e_neg_all[:, sl])
        k_t.append(k2[-1] * e_neg_all[:, sl])
        vb.append(v[-1].astype(BF16))

    items = [(hh, i) for hh in heads for i in chunks]
    aa = [_bdot_nt(jnp.concatenate([a_t[hh][rws[i]], r_t[hh][rws[i]]], axis=0),
                   jnp.concatenate([b_t[hh][rws[i]], k_t[hh][rws[i]]], axis=0)) for hh, i in items]
    r2 = lax.broadcasted_iota(jnp.int32, (2 * c, c), 0)
    c2 = lax.broadcasted_iota(jnp.int32, (2 * c, c), 1)
    tri2 = jnp.where(r2 < c, r2, r2 - c + 1) > c2
    a_ab = [jnp.where(strict, x[:c, :c], 0.0) for x in aa]
    a_rb = [jnp.where(incl, x[c:, :c], 0.0).astype(BF16) for x in aa]
    xkv = [_bdot(jnp.where(tri2, x[:, c:], 0.0), vb[hh][rws[i]]) for x, (hh, i) in zip(aa, items)]
    yv = [x[c:] for x in xkv]
    w_c = [e_in_all[i * c + c - 1:(i + 1) * c, sls[hh]] for hh, i in items]
    kv = [_bdot_tn(vb[hh][rws[i]], k_t[hh][rws[i]] * w) for w, (hh, i) in zip(w_c, items)]
    bh = [(b_t[hh][rws[i]] * w).astype(BF16) for w, (hh, i) in zip(w_c, items)]
    tinv = _unit_lower_inverses(a_ab, c)
    wt = [_bdot(t, a_t[hh][rws[i]]) for t, (hh, i) in zip(tinv, items)]
    u = [_bdot(t, x[:c]) for t, x in zip(tinv, xkv)]
    wr = [jnp.concatenate([w, r_t[hh][rws[i]]], axis=0).astype(BF16) for w, (hh, i) in zip(wt, items)]

    s = [s_ref[hh] for hh in heads]
    ys = [[] for _ in heads]
    for i in chunks:
        idx = [hh * nc + i for hh in heads]
        ls = [_dot_nt(wr[j], s[hh].astype(BF16)) for hh, j in zip(heads, idx)]
        p = [x[:c] + u[j] for x, j in zip(ls, idx)]
        for hh, j in zip(heads, idx):
            ys[hh].append(ls[hh][c:] + _bdot(a_rb[j], p[hh]) + yv[j])
        s = [s[hh] * w_c[j] + _bdot_tn(p[hh], bh[j]) + kv[j] for hh, j in zip(heads, idx)]
    zs = []
    for hh, sl in zip(heads, sls):
        s_ref[hh] = s[hh]
        y = ys[hh][0] if nc == 1 else jnp.concatenate(ys[hh], axis=0)
        mu = jnp.mean(y, axis=-1, keepdims=True)
        yc = y - mu
        yn = yc * lax.rsqrt(jnp.mean(yc * yc, axis=-1, keepdims=True) + LNX_EPS)
        yn = yn * lg_ref[:, sl] + lb_ref[:, sl]
        bonus = jnp.sum(r[hh] * k2[hh] * rk_ref[:, sl], axis=-1, keepdims=True) * v[hh]
        zs.append((yn + bonus) * g_ref[:, sl])
    z_ref[...] = jnp.concatenate(zs, axis=1).astype(z_ref.dtype)


def _wkv(rkv, lw, a, g, k_k, k_a, r_k, lnx_g, lnx_b, s0, rest, n_seq, t, c, nc, hb):
    rows, d = lw.shape
    h = d // RWKV_HEAD
    hw = hb * RWKV_HEAD
    tb = nc * c
    nblk = t // tb
    tok = lambda s, hg, ch: (s * nblk + ch, hg)
    par = lambda s, hg, ch: (0, hg)
    st = lambda s, hg, ch: (s, hg, 0, 0)
    rkv_spec = lambda p: pl.BlockSpec((None, tb, hw), lambda s, hg, ch: (p, s * nblk + ch, hg))
    return pl.pallas_call(
        functools.partial(_wkv_kernel, c=c, nc=nc, hb=hb),
        grid=(n_seq, h // hb, nblk),
        in_specs=[rkv_spec(0), rkv_spec(1), rkv_spec(2),
                  pl.BlockSpec((tb, hw), tok), pl.BlockSpec((tb, hw), tok), pl.BlockSpec((tb, hw), tok),
                  pl.BlockSpec((1, hw), par), pl.BlockSpec((1, hw), par), pl.BlockSpec((1, hw), par),
                  pl.BlockSpec((1, hw), par), pl.BlockSpec((1, hw), par),
                  pl.BlockSpec((None, hb, RWKV_HEAD, RWKV_HEAD), st),
                  pl.BlockSpec(memory_space=pl.ANY)],
        out_specs=[pl.BlockSpec((tb, hw), tok),
                   pl.BlockSpec((None, hb, RWKV_HEAD, RWKV_HEAD), st)],
        out_shape=[jax.ShapeDtypeStruct((rows, d), BF16),
                   jax.ShapeDtypeStruct((n_seq, h, RWKV_HEAD, RWKV_HEAD), F32)],
        input_output_aliases={12: 0},
        compiler_params=_params("parallel", "parallel", "arbitrary"),
    )(rkv, rkv, rkv, lw, a, g, k_k, k_a, r_k, lnx_g, lnx_b, s0, rest)


WKV_PROMPT_CHUNKS_PER_STEP = 2
WKV_PROMPT_HEADS_PER_STEP = 8


def _to_slots_kernel(x_ref, o_ref):
    x = x_ref[...]
    by_slot = jnp.stack([x[:, s * HEAD_DIM:(s + 1) * HEAD_DIM] for s in range(o_ref.shape[1])], axis=0)
    o_ref[...] = jnp.swapaxes(by_slot, 0, 1)


def _to_slots(y, rows, col0, n_slots, *, tm_pref=512):
    width = n_slots * HEAD_DIM
    assert col0 % width == 0
    tm = _tile(rows, tm_pref)
    return pl.pallas_call(
        _to_slots_kernel,
        grid=(rows // tm,),
        in_specs=[pl.BlockSpec((tm, width), lambda i: (i, col0 // width))],
        out_specs=pl.BlockSpec((tm, n_slots, HEAD_DIM), lambda i: (i, 0, 0)),
        out_shape=jax.ShapeDtypeStruct((rows, n_slots, HEAD_DIM), y.dtype),
        compiler_params=_params("parallel"),
    )(y)


def _nsa_layer(x, np_tok, bsz, t, db, s_len, cache5, layer, win, page_table, w_in, pe, w1, w2, w_out,
               ln_g, ln_b, alpha):
    n_main = NQ_COLS + 6 * NKV_COLS
    y_main, y_b = _mm(x, w_in, layer, n_main, out_dtypes=(F32, BF16), tm_pref=832, tn_pref=1024)
    w_gate = jnp.pad(w_in[layer:layer + 1, :, n_main:], ((0, 0), (0, 0), (0, 128 - 3 * N_HEADS)))
    gates = _mm(x, w_gate, 0, 128, act="sigmoid")[:, :3 * N_HEADS]

    wc, w2b = _cmp_weights(w1, w2)
    bias = _cmp_bias(pe, w1)
    cmp_p = _cmp_prompt(y_main, bsz, t, wc, bias, w2b)
    cmp_s = _cmp_sample(cache5, layer, page_table, wc, bias, w2b)

    pad_rows = ((0, 0), (0, SAMPLE_ROWS - s_len), (0, 0))
    ys = y_main[np_tok:].reshape(db, s_len, n_main)
    ys8 = jnp.pad(ys, pad_rows)
    gates8 = jnp.pad(gates[np_tok:].reshape(db, s_len, 3 * N_HEADS), pad_rows)
    wb = win.shape[1]
    win4 = win.reshape(db, wb, 2 * N_KV_HEADS, HEAD_DIM)
    o_s = _nsa_sample(ys8, cmp_s, cache5, layer, win4, gates8, page_table, s_len)
    o_s = o_s[:, :s_len].reshape(db * s_len, NQ_COLS).astype(BF16)

    o = _nsa_prompt(y_main, y_b, cmp_p, gates, bsz, t, jnp.pad(o_s, ((np_tok, 0), (0, 0))))
    h = _mm_ln_cols(o, w_out, layer, x, ln_g, ln_b, alpha)

    kv0 = NQ_COLS
    kw0 = NQ_COLS + 4 * NKV_COLS
    kv_p = _to_slots(y_main, np_tok, kv0, 4 * N_KV_HEADS).reshape(bsz, t, 4, N_KV_HEADS, HEAD_DIM)
    wlen = min(WINDOW, t)
    win_p = jnp.stack([lax.slice(y_main, (b * t + t - wlen, kw0), ((b + 1) * t, n_main)) for b in range(bsz)])
    win_p = win_p.reshape(bsz, wlen, 2, N_KV_HEADS, HEAD_DIM)
    kv_s = ys[:, :, kv0:kw0].reshape(db, s_len, 4, N_KV_HEADS, HEAD_DIM)
    kvw_s = ys[:, :, kw0:].reshape(db, s_len, 2, N_KV_HEADS, HEAD_DIM)
    win_s = jnp.concatenate([win, kvw_s], axis=1)[:, s_len:]
    return h, kv_p, kv_s, win_p, win_s


def _rwkv_layer(x, np_tok, bsz, t, db, s_len, shift_s, wkv_s, mu, w_rkv, w0, w1, w2, a0, a1, a2, g1, g2,
                k_k, k_a, r_k, lnx_g, lnx_b, w_out, layer, ln_g, ln_b, alpha):
    d = x.shape[1]
    heads = d // RWKV_HEAD
    xprev = jnp.concatenate([jnp.zeros((1, d), x.dtype), x[:-1]], axis=0)
    xprev = xprev.at[np.arange(1, bsz) * t].set(0.0)
    xprev = xprev.at[np_tok + np.arange(db) * s_len].set(shift_s.astype(x.dtype))

    row = lambda v: v.reshape(1, -1)
    rkv = _mm_mix(x, xprev, mu[:3, None, :], w_rkv)
    lw, a, g = _loras(x, xprev, mu[3:6], w1, w2, row(w0), a1, a2, row(a0), g1, g2)

    pvec = (row(k_k), row(k_a), row(r_k), row(lnx_g), row(lnx_b))
    c_p = _tile(t, WKV_CHUNK, 8)
    nc_p = math.gcd(t // c_p, WKV_PROMPT_CHUNKS_PER_STEP)
    hb_p = math.gcd(heads, WKV_PROMPT_HEADS_PER_STEP)
    zero_state = jnp.zeros((bsz, heads, RWKV_HEAD, RWKV_HEAD), F32)
    c_s = -(-s_len // 8) * 8
    pad = lambda v: jnp.pad(v.reshape(v.shape[:-2] + (db, s_len, d)),
                            [(0, 0)] * (v.ndim - 1) + [(0, c_s - s_len), (0, 0)]
                            ).reshape(v.shape[:-2] + (db * c_s, d))
    z_s, st_s = _wkv(pad(rkv[:, np_tok:]), pad(lw[np_tok:]), pad(a[np_tok:]), pad(g[np_tok:]), *pvec,
                     wkv_s.astype(F32), jnp.zeros((db * c_s, d), BF16), db, c_s, c_s, 1, heads)
    z_s = z_s.reshape(db, c_s, d)[:, :s_len].reshape(db * s_len, d)

    z, st_p = _wkv(rkv, lw, a, g, *pvec, zero_state, jnp.pad(z_s, ((np_tok, 0), (0, 0))), bsz, t, c_p, nc_p, hb_p)
    h = _mm_ln_cols(z, w_out, layer, x, ln_g, ln_b, alpha)
    last_p = x[np.arange(1, bsz + 1) * t - 1]
    last_s = x[np_tok + np.arange(1, db + 1) * s_len - 1]
    return h, st_p, st_s, last_p, last_s


def _mlp(h, w_up, w_down, layer, ln_g, ln_b, alpha):
    d_ff = w_up.shape[2]
    u = _mm(h, w_up, layer, d_ff, act="relu2", out_dtypes=(BF16,))
    return _mm_ln(u, w_down, layer, h, ln_g, ln_b, alpha)


def kernel(x_prompt, x_sample, cache_nsa_kv, state_nsa_win, state_rwkv_wkv, state_rwkv_shift, page_table,
           nsa_w_in, nsa_cmp_pe, nsa_cmp_w1, nsa_cmp_w2, nsa_w_out,
           rwkv_mu, rwkv_w_rkv, rwkv_w0, rwkv_w1, rwkv_w2, rwkv_a0, rwkv_a1, rwkv_a2, rwkv_g1, rwkv_g2,
           rwkv_k_k, rwkv_k_a, rwkv_r_k, rwkv_lnx_g, rwkv_lnx_b, rwkv_w_out,
           ffn_w_up, ffn_w_down, ln_g, ln_b):
    bsz, t, d = x_prompt.shape
    db, s_len, _ = x_sample.shape
    depth = ffn_w_up.shape[0]
    alpha = (2 * depth) ** 0.25
    np_tok = bsz * t
    x = jnp.concatenate([x_prompt.reshape(np_tok, d), x_sample.reshape(db * s_len, d)], axis=0)
    n_l, n_pool = cache_nsa_kv.shape[:2]
    cache5 = cache_nsa_kv.reshape(n_l, n_pool, PAGE_SIZE, 4 * N_KV_HEADS, HEAD_DIM)
    page_table = page_table.astype(jnp.int32)

    kv_p, kv_s, win_p, win_s, wkv_p, wkv_s, sh_p, sh_s = ([] for _ in range(8))
    row = lambda v: v.reshape(1, -1)
    for i in range(depth):
        j = i // 2
        if i % 2 == 0:
            h, kvp_new, kvs_new, wp_new, ws_new = _nsa_layer(
                x, np_tok, bsz, t, db, s_len, cache5, j, state_nsa_win[j], page_table,
                nsa_w_in, nsa_cmp_pe[j], nsa_cmp_w1[j], nsa_cmp_w2[j], nsa_w_out,
                row(ln_g[i, 0]), row(ln_b[i, 0]), alpha)
            kv_p.append(kvp_new)
            kv_s.append(kvs_new)
            win_p.append(wp_new)
            win_s.append(ws_new)
        else:
            h, sp_new, ss_new, hp_new, hs_new = _rwkv_layer(
                x, np_tok, bsz, t, db, s_len, state_rwkv_shift[j], state_rwkv_wkv[j],
                rwkv_mu[j], rwkv_w_rkv[j], rwkv_w0[j], rwkv_w1[j], rwkv_w2[j], rwkv_a0[j], rwkv_a1[j],
                rwkv_a2[j], rwkv_g1[j], rwkv_g2[j], rwkv_k_k[j], rwkv_k_a[j], rwkv_r_k[j],
                rwkv_lnx_g[j], rwkv_lnx_b[j], rwkv_w_out, j, row(ln_g[i, 0]), row(ln_b[i, 0]), alpha)
            wkv_p.append(sp_new)
            wkv_s.append(ss_new)
            sh_p.append(hp_new)
            sh_s.append(hs_new)
        x = _mlp(h, ffn_w_up, ffn_w_down, i, row(ln_g[i, 1]), row(ln_b[i, 1]), alpha)
    return (x[:np_tok].reshape(bsz, t, d), x[np_tok:].reshape(db, s_len, d),
            jnp.stack(kv_p), jnp.stack(kv_s), jnp.stack(win_p), jnp.stack(win_s),
            jnp.stack(wkv_p), jnp.stack(wkv_s), jnp.stack(sh_p), jnp.stack(sh_s))
```

```python
import functools
import math

import jax
import jax.numpy as jnp
import numpy as np
from jax import lax
from jax.experimental import pallas as pl
from jax.experimental.pallas import tpu as pltpu

F32 = jnp.float32
BF16 = jnp.bfloat16

HEAD_DIM = 128
N_KV_HEADS = 4
GROUP = 4
N_HEADS = N_KV_HEADS * GROUP
QW = GROUP * HEAD_DIM
NQ_COLS = N_HEADS * HEAD_DIM
NKV_COLS = N_KV_HEADS * HEAD_DIM
CMP_BLOCK = 32
CMP_STRIDE = 16
SEL_BLOCK = 64
N_SEL = 16
WINDOW = 512
Q_BLOCK = 256
PAGE_SIZE = 128
RWKV_HEAD = 64
LN_EPS = 1e-5
LNX_EPS = 64e-5
FORCE = 1e4
NEG_INF = -1e30
TINY = 1e-30
CHUNKS_PER_PAGE = PAGE_SIZE // CMP_STRIDE

VMEM_LIMIT_BYTES = 56 * 1024 * 1024
VREG_LANES = 128
HI = lax.Precision.HIGHEST


def _params(*sem):
    return pltpu.CompilerParams(dimension_semantics=sem, vmem_limit_bytes=VMEM_LIMIT_BYTES)


def _tile(n, pref, mult=16):
    if n <= pref:
        return n
    for t in range(pref - pref % mult, 0, -mult):
        if n % t == 0:
            return t
    return n


def _dot(a, b, precision=None):
    return jnp.dot(a, b, preferred_element_type=F32, precision=precision)


def _dot_nt(a, b, precision=None):
    return lax.dot_general(a, b, (((1,), (1,)), ((), ())), preferred_element_type=F32, precision=precision)


def _dot_tn(a, b, precision=None):
    return lax.dot_general(a, b, (((0,), (0,)), ((), ())), preferred_element_type=F32, precision=precision)


def _masked_softmax(s, mask):
    s = jnp.where(mask, s, NEG_INF)
    m = jnp.max(s, axis=-1, keepdims=True)
    p = jnp.where(mask, jnp.exp(s - m), 0.0)
    return p * (1.0 / jnp.maximum(jnp.sum(p, axis=-1, keepdims=True), TINY))


def _layer_norm_rows(s, g, b, eps):
    mu = jnp.mean(s, axis=-1, keepdims=True)
    c = s - mu
    var = jnp.mean(c * c, axis=-1, keepdims=True)
    return c * lax.rsqrt(var + eps) * g + b


def _mm_kernel(x_ref, w_ref, *refs, act):
    out_refs, xb_ref = refs[:-1], refs[-1]

    @pl.when(pl.program_id(1) == 0)
    def _():
        xb_ref[...] = x_ref[...].astype(BF16)

    y = _dot(xb_ref[...], w_ref[...].astype(BF16))
    if act == "relu2":
        y = jnp.maximum(y, 0.0)
        y = y * y
    elif act == "sigmoid":
        y = jax.nn.sigmoid(y)
    for o_ref in out_refs:
        o_ref[...] = y.astype(o_ref.dtype)


def _mm(x, w, layer, n_out, *, act=None, out_dtypes=(F32,), tm_pref=1040, tn_pref=1024):
    m, k = x.shape
    tm = _tile(m, tm_pref)
    tn = _tile(n_out, tn_pref, 128)
    outs = pl.pallas_call(
        functools.partial(_mm_kernel, act=act),
        grid=(m // tm, n_out // tn),
        in_specs=[pl.BlockSpec((tm, k), lambda i, j: (i, 0)),
                  pl.BlockSpec((None, k, tn), lambda i, j: (layer, 0, j))],
        out_specs=[pl.BlockSpec((tm, tn), lambda i, j: (i, j)) for _ in out_dtypes],
        out_shape=[jax.ShapeDtypeStruct((m, n_out), dt) for dt in out_dtypes],
        scratch_shapes=[pltpu.VMEM((tm, k), BF16)],
        compiler_params=_params("parallel", "arbitrary"),
    )(x, w)
    return outs[0] if len(outs) == 1 else outs


def _mm_mix_kernel(x_ref, xp_ref, mu_ref, w_ref, o_ref, xb_ref):
    @pl.when(pl.program_id(2) == 0)
    def _():
        x = x_ref[...]
        xb_ref[...] = (x + (xp_ref[...] - x) * mu_ref[...]).astype(BF16)

    o_ref[...] = _dot(xb_ref[...], w_ref[...].astype(BF16))


def _mm_mix(x, xp, mu, w, *, tm_pref=1040, tn_pref=512):
    m, k = x.shape
    npar, _, n = w.shape
    tm = _tile(m, tm_pref)
    tn = _tile(n, tn_pref, 128)
    return pl.pallas_call(
        _mm_mix_kernel,
        grid=(m // tm, npar, n // tn),
        in_specs=[pl.BlockSpec((tm, k), lambda i, p, j: (i, 0)),
                  pl.BlockSpec((tm, k), lambda i, p, j: (i, 0)),
                  pl.BlockSpec((None, 1, k), lambda i, p, j: (p, 0, 0)),
                  pl.BlockSpec((None, k, tn), lambda i, p, j: (p, 0, j))],
        out_specs=pl.BlockSpec((None, tm, tn), lambda i, p, j: (p, i, j)),
        out_shape=jax.ShapeDtypeStruct((npar, m, n), F32),
        scratch_shapes=[pltpu.VMEM((tm, k), BF16)],
        compiler_params=_params("parallel", "arbitrary", "arbitrary"),
    )(x, xp, mu, w)


def _loras_kernel(x_ref, xp_ref, mu_ref, w1_ref, w2_ref, w0_ref, a1_ref, a2_ref, a0_ref, g1_ref, g2_ref,
                  lw_ref, a_ref, g_ref):
    x = x_ref[...]
    xx = xp_ref[...] - x
    mix = lambda p: (x + xx * mu_ref[p:p + 1, :]).astype(BF16)
    low = lambda xm, w: _dot(xm, w[...].astype(BF16))
    up = lambda h, w: _dot(h.astype(BF16), w[...].astype(BF16))
    u = -(up(jnp.tanh(low(mix(0), w1_ref)), w2_ref) + w0_ref[...])
    softplus = jnp.maximum(u, 0.0) + jnp.log(1.0 + jnp.exp(-jnp.abs(u)))
    lw_ref[...] = -jnp.exp(-softplus - 0.5)
    a_ref[...] = jax.nn.sigmoid(up(low(mix(1), a1_ref), a2_ref) + a0_ref[...])
    g_ref[...] = up(jax.nn.sigmoid(low(mix(2), g1_ref)), g2_ref)


def _loras(x, xp, mu, w1, w2, w0, a1, a2, a0, g1, g2, *, tm_pref=320):
    m, k = x.shape
    n = w2.shape[1]
    tm = _tile(m, tm_pref)
    row = lambda i: (i, 0)
    fixed = lambda i: (0, 0)
    whole = lambda v: pl.BlockSpec(v.shape, fixed)
    out = jax.ShapeDtypeStruct((m, n), F32)
    return pl.pallas_call(
        _loras_kernel,
        grid=(m // tm,),
        in_specs=[pl.BlockSpec((tm, k), row), pl.BlockSpec((tm, k), row), whole(mu),
                  whole(w1), whole(w2), whole(w0), whole(a1), whole(a2), whole(a0), whole(g1), whole(g2)],
        out_specs=[pl.BlockSpec((tm, n), row)] * 3,
        out_shape=[out, out, out],
        compiler_params=_params("parallel"),
    )(x, xp, mu, w1, w2, w0, a1, a2, a0, g1, g2)


def _mm_ln_kernel(z_ref, w_ref, res_ref, g_ref, b_ref, o_ref, *, alpha):
    kk = pl.program_id(1)

    @pl.when(kk == 0)
    def _():
        o_ref[...] = _dot(z_ref[...].astype(BF16), w_ref[...].astype(BF16))

    @pl.when(kk > 0)
    def _():
        o_ref[...] += _dot(z_ref[...].astype(BF16), w_ref[...].astype(BF16))

    @pl.when(kk == pl.num_programs(1) - 1)
    def _():
        s = alpha * res_ref[...] + o_ref[...]
        o_ref[...] = _layer_norm_rows(s, g_ref[...], b_ref[...], LN_EPS)


def _mm_ln_cols_kernel(z_ref, w_ref, res_ref, g_ref, b_ref, o_ref, *, alpha, tn):
    j = pl.program_id(1)
    col = pl.multiple_of(j * tn, tn)
    o_ref[:, pl.ds(col, tn)] = _dot(z_ref[...], w_ref[...].astype(BF16))

    @pl.when(j == pl.num_programs(1) - 1)
    def _():
        s = alpha * res_ref[...] + o_ref[...]
        o_ref[...] = _layer_norm_rows(s, g_ref[...], b_ref[...], LN_EPS)


def _mm_ln_cols(z, w, layer, res, g, b, alpha, *, tm_pref=832, tn_pref=512):
    m, k = z.shape
    n = w.shape[2]
    tm = _tile(m, tm_pref)
    tn = _tile(n, tn_pref, 128)
    return pl.pallas_call(
        functools.partial(_mm_ln_cols_kernel, alpha=alpha, tn=tn),
        grid=(m // tm, n // tn),
        in_specs=[pl.BlockSpec((tm, k), lambda i, j: (i, 0)),
                  pl.BlockSpec((None, k, tn), lambda i, j: (layer, 0, j)),
                  pl.BlockSpec((tm, n), lambda i, j: (i, 0)),
                  pl.BlockSpec((1, n), lambda i, j: (0, 0)),
                  pl.BlockSpec((1, n), lambda i, j: (0, 0))],
        out_specs=pl.BlockSpec((tm, n), lambda i, j: (i, 0)),
        out_shape=jax.ShapeDtypeStruct((m, n), F32),
        compiler_params=_params("parallel", "arbitrary"),
    )(z, w, res, g, b)


def _mm_ln(z, w, layer, res, g, b, alpha, *, tm_pref=640, tk_pref=1024):
    m, k = z.shape
    n = w.shape[2]
    tm = _tile(m, tm_pref)
    tk = _tile(k, tk_pref, 128)
    return pl.pallas_call(
        functools.partial(_mm_ln_kernel, alpha=alpha),
        grid=(m // tm, k // tk),
        in_specs=[pl.BlockSpec((tm, tk), lambda i, kk: (i, kk)),
                  pl.BlockSpec((None, tk, n), lambda i, kk: (layer, kk, 0)),
                  pl.BlockSpec((tm, n), lambda i, kk: (i, 0)),
                  pl.BlockSpec((1, n), lambda i, kk: (0, 0)),
                  pl.BlockSpec((1, n), lambda i, kk: (0, 0))],
        out_specs=pl.BlockSpec((tm, n), lambda i, kk: (i, 0)),
        out_shape=jax.ShapeDtypeStruct((m, n), F32),
        compiler_params=_params("parallel", "arbitrary"),
    )(z, w, res, g, b)


def _cmp_bias_kernel(pe_ref, w1_ref, o_ref):
    acc = jnp.zeros((1, HEAD_DIM), F32)
    for rj in range(CMP_BLOCK):
        acc = acc + _dot(pe_ref[rj:rj + 1, :], w1_ref[rj], HI)
    o_ref[...] = acc


def _cmp_bias(pe, w1):
    return pl.pallas_call(
        _cmp_bias_kernel,
        grid=(2,),
        in_specs=[pl.BlockSpec((None, CMP_BLOCK, HEAD_DIM), lambda p: (p, 0, 0)),
                  pl.BlockSpec((None, CMP_BLOCK, HEAD_DIM, HEAD_DIM), lambda p: (p, 0, 0, 0))],
        out_specs=pl.BlockSpec((None, 1, HEAD_DIM), lambda p: (p, 0, 0)),
        out_shape=jax.ShapeDtypeStruct((2, 1, HEAD_DIM), F32),
        compiler_params=_params("parallel"),
    )(pe, w1)


def _cmp_kernel(*refs, n_in):
    pages = refs[-5 - n_in:-5]
    wc_ref, bias_ref, w2_ref, out_ref, carry_ref = refs[-5:]
    mk = CHUNKS_PER_PAGE * n_in
    rows = N_KV_HEADS * mk

    @pl.when(pl.program_id(1) == 0)
    def _():
        carry_ref[...] = jnp.zeros_like(carry_ref)

    first = (lax.broadcasted_iota(jnp.int32, (rows, 1), 0) % mk) == 0
    if len(pages[0].shape) == 3:
        split = [pages[i][...].reshape(CHUNKS_PER_PAGE, CMP_STRIDE, 2 * N_KV_HEADS, HEAD_DIM)
                 for i in range(n_in)]
        by_slot = [[jnp.swapaxes(split[i][:, j], 0, 1) for j in range(CMP_STRIDE)] for i in range(n_in)]
        piece = lambda i, j, slot: by_slot[i][j][slot]
    else:
        r_out = lax.broadcasted_iota(jnp.int32, (PAGE_SIZE, PAGE_SIZE), 0)
        r_in = lax.broadcasted_iota(jnp.int32, (PAGE_SIZE, PAGE_SIZE), 1)
        perm = jnp.where(r_in == (r_out % CHUNKS_PER_PAGE) * CMP_STRIDE + r_out // CHUNKS_PER_PAGE, 1.0, 0.0)
        grouped = [_dot(perm.astype(BF16), pages[i][...].astype(BF16)) for i in range(n_in)]
        piece = lambda i, j, slot: grouped[i][j * CHUNKS_PER_PAGE:(j + 1) * CHUNKS_PER_PAGE,
                                              slot * HEAD_DIM:(slot + 1) * HEAD_DIM]
    for p in range(2):
        lhs = jnp.concatenate(
            [jnp.concatenate([piece(i, j, p * N_KV_HEADS + k).astype(BF16) for j in range(CMP_STRIDE)], axis=1)
             for k in range(N_KV_HEADS) for i in range(n_in)], axis=0)
        acc = _dot(lhs, wc_ref[p])
        part0 = acc[:, :HEAD_DIM]
        part1 = acc[:, HEAD_DIM:]
        prev = jnp.where(first, carry_ref[p], pltpu.roll(part0, 1, 0))
        carry_ref[p] = pltpu.roll(part0, rows - (mk - 1), 0)
        h = prev + part1 + bias_ref[p]
        c = _dot(jax.nn.gelu(h).astype(BF16), w2_ref[p])
        for k in range(N_KV_HEADS):
            col = (p * N_KV_HEADS + k) * HEAD_DIM
            out_ref[:, col:col + HEAD_DIM] = c[k * mk:(k + 1) * mk]


def _cmp_weights(w1, w2):
    n_r = CMP_BLOCK // CMP_STRIDE
    w1r = w1.reshape(2, n_r, CMP_STRIDE, HEAD_DIM, HEAD_DIM)
    wc = jnp.concatenate([w1r[:, r] for r in range(n_r)], axis=-1).astype(BF16)
    return wc.reshape(2, CMP_STRIDE * HEAD_DIM, 2 * HEAD_DIM), w2.astype(BF16)


def _cmp_call(n_seq, n_steps, n_in, page_specs, operands, wc, bias, w2b, num_prefetch, prefetch):
    mk = CHUNKS_PER_PAGE * n_in
    if num_prefetch:
        fixed3 = lambda s, t, pt: (0, 0, 0)
        out_map = lambda s, t, pt: (s, t, 0)
    else:
        fixed3 = lambda s, t: (0, 0, 0)
        out_map = lambda s, t: (s, t, 0)
    grid_spec = pltpu.PrefetchScalarGridSpec(
        num_scalar_prefetch=num_prefetch,
        grid=(n_seq, n_steps),
        in_specs=page_specs + [
            pl.BlockSpec((2, CMP_STRIDE * HEAD_DIM, 2 * HEAD_DIM), fixed3),
            pl.BlockSpec((2, 1, HEAD_DIM), fixed3),
            pl.BlockSpec((2, HEAD_DIM, HEAD_DIM), fixed3)],
        out_specs=pl.BlockSpec((None, mk, 2 * NKV_COLS), out_map),
        scratch_shapes=[pltpu.VMEM((2, N_KV_HEADS * mk, HEAD_DIM), F32)],
    )
    return pl.pallas_call(
        functools.partial(_cmp_kernel, n_in=n_in),
        grid_spec=grid_spec,
        out_shape=jax.ShapeDtypeStruct((n_seq, n_steps * mk, 2 * NKV_COLS), F32),
        compiler_params=_params("parallel", "arbitrary"),
    )(*prefetch, *operands, wc, bias, w2b)


CMP_PAGES_PER_STEP = 8


def _cmp_prompt(y_main, bsz, t, wc, bias, w2b):
    pages_per_seq = t // PAGE_SIZE
    n_in = math.gcd(pages_per_seq, CMP_PAGES_PER_STEP)
    n_steps = pages_per_seq // n_in
    col_blk = NQ_COLS // (2 * NKV_COLS)
    specs = [pl.BlockSpec((PAGE_SIZE, 2 * NKV_COLS),
                          functools.partial(lambda s, st, i: (s * pages_per_seq + st * n_in + i, col_blk), i=i))
             for i in range(n_in)]
    return _cmp_call(bsz, n_steps, n_in, specs, [y_main] * n_in, wc, bias, w2b, 0, ())


def _cmp_sample(cache5, layer, page_table, wc, bias, w2b):
    n_in = math.gcd(page_table.shape[1], CMP_PAGES_PER_STEP)
    db, n_pages = page_table.shape
    n_steps = n_pages // n_in
    specs = [pl.BlockSpec((None, None, PAGE_SIZE, 2 * N_KV_HEADS, HEAD_DIM),
                          functools.partial(lambda s, st, pt, i: (layer, pt[s, st * n_in + i], 0, 0, 0), i=i))
             for i in range(n_in)]
    return _cmp_call(db, n_steps, n_in, specs, [cache5] * n_in, wc, bias, w2b, 1, (page_table,))


def _overlap_shifted(n_rows, n_c, n_s, n_cols):
    m = np.arange(n_rows)[:, None]
    j = np.arange(n_cols)[None, :]
    c0 = (m - 1) * CMP_STRIDE
    s0 = j * SEL_BLOCK
    ov = (c0 < s0 + SEL_BLOCK) & (c0 + CMP_BLOCK > s0) & (m >= 1) & (m <= n_c) & (j < n_s)
    return jnp.asarray(ov.astype(np.float32))


SEL_KEY_TILE = 512


def _softmax_cols(s, mask):
    s = jnp.where(mask, s, NEG_INF)
    m = jnp.max(s, axis=0, keepdims=True)
    p = jnp.where(mask, jnp.exp2(s - m), 0.0)
    return p * (1.0 / jnp.maximum(jnp.sum(p, axis=0, keepdims=True), TINY))


def _select_blocks_cols(imp, qpos, n_s):
    n_rows = imp.shape[0]
    j = lax.broadcasted_iota(jnp.int32, (n_rows, 1), 0)
    cur = qpos // SEL_BLOCK
    forced = (j == 0) | (j == cur) | (j == cur - 1)
    valid = j * SEL_BLOCK <= qpos
    score = jnp.where(forced, FORCE, jnp.where(valid, imp, -FORCE))
    score = jnp.where(j < n_s, score, -jnp.inf)
    sub = 8
    blocks = [score[b * sub:(b + 1) * sub] for b in range(-(-n_s // sub))]
    ranks = [jnp.zeros(blk.shape, F32) for blk in blocks]
    row = lax.broadcasted_iota(jnp.int32, (sub, 1), 0)
    for i in range(n_s):
        bi, ri = divmod(i, sub)
        si = blocks[bi][ri:ri + 1, :]
        for b, blk in enumerate(blocks):
            if b < bi:
                beats = si > blk
            elif b > bi:
                beats = si >= blk
            else:
                beats = jnp.where(row > ri, jnp.where(si >= blk, 1.0, 0.0), jnp.where(si > blk, 1.0, 0.0)) > 0.5
            ranks[b] = ranks[b] + jnp.where(beats, 1.0, 0.0)
    rank = jnp.concatenate(ranks, axis=0)
    sel = jnp.where(rank < float(min(N_SEL, n_s)), 1.0, 0.0)
    if sel.shape[0] < n_rows:
        sel = jnp.concatenate([sel, jnp.zeros((n_rows - sel.shape[0], sel.shape[1]), F32)], axis=0)
    return sel


def _nsa_prompt_kernel(q_ref, kc_ref, vct_ref, ks_ref, vst_ref, kw_ref, vwt_ref, g_ref, ovt_ref, rest_ref, o_ref,
                       selbias_ref, *, t):
    del rest_ref
    start = pl.program_id(2) * Q_BLOCK
    n_cols = GROUP * Q_BLOCK
    q = q_ref[...] * (HEAD_DIM ** -0.5 * math.log2(math.e))
    q4 = jnp.concatenate([q[:, g * HEAD_DIM:(g + 1) * HEAD_DIM] for g in range(GROUP)], axis=0).astype(BF16)
    qpos1 = start + lax.broadcasted_iota(jnp.int32, (1, Q_BLOCK), 1)
    qpos4 = start + lax.broadcasted_iota(jnp.int32, (1, n_cols), 1) % Q_BLOCK

    n_cr = kc_ref.shape[0]
    n_s = ovt_ref.shape[0]
    s = _dot_nt(kc_ref[...].astype(BF16), q4)
    m_idx = lax.broadcasted_iota(jnp.int32, (n_cr, 1), 0)
    cmask = (m_idx >= 1) & (m_idx * CMP_STRIDE + (CMP_BLOCK - CMP_STRIDE - 1) <= qpos4)
    p_c = _softmax_cols(s, cmask)
    o_c = _dot(vct_ref[...].astype(BF16), p_c.astype(BF16))
    psum = p_c[:, 0:Q_BLOCK]
    for g in range(1, GROUP):
        psum = psum + p_c[:, g * Q_BLOCK:(g + 1) * Q_BLOCK]
    imp = _dot(ovt_ref[...], psum, HI)
    sel = _select_blocks_cols(imp, qpos1, n_s)

    wlen = WINDOW + Q_BLOCK
    w0 = pl.multiple_of(jnp.clip(start - WINDOW, 0, t - wlen), PAGE_SIZE)
    sw = _dot_nt(kw_ref[pl.ds(w0, wlen), :], q4)
    dpos = qpos4 - (w0 + lax.broadcasted_iota(jnp.int32, (wlen, 1), 0))
    p_w = _softmax_cols(sw, (dpos >= 0) & (dpos < WINDOW))
    wblk = w0 // PAGE_SIZE
    vwt = jnp.concatenate([vwt_ref[wblk + i] for i in range(wlen // PAGE_SIZE)], axis=1)
    o_w = _dot(vwt, p_w.astype(BF16))

    tk = SEL_KEY_TILE
    blocks_per_tile = tk // SEL_BLOCK
    selbias_ref[...] = jnp.where(sel > 0.5, 0.0, NEG_INF)

    def scores(kt):
        return _dot_nt(ks_ref[pl.ds(pl.multiple_of(kt * tk, tk), tk), :], q4)

    def tile(kt, carry, last):
        m, l, acc, sc = carry
        sc_next = sc if last else scores(kt + 1)
        off = kt * tk
        rows = selbias_ref[pl.ds(pl.multiple_of(kt * blocks_per_tile, blocks_per_tile), blocks_per_tile), :]
        bias = jnp.concatenate([jnp.broadcast_to(rows[i:i + 1, :], (SEL_BLOCK, Q_BLOCK))
                                for i in range(blocks_per_tile)], axis=0)
        if last:
            kpos = off + lax.broadcasted_iota(jnp.int32, (tk, Q_BLOCK), 0)
            bias = jnp.where(kpos <= qpos1, bias, NEG_INF)
        sc = sc + jnp.concatenate([bias] * GROUP, axis=1)
        m_new = jnp.maximum(m, jnp.max(sc, axis=0, keepdims=True))
        alpha = jnp.exp2(m - m_new)
        p = jnp.exp2(sc - m_new)
        l = alpha * l + jnp.sum(p, axis=0, keepdims=True)
        vblk = kt * (tk // PAGE_SIZE)
        vst = jnp.concatenate([vst_ref[vblk + i] for i in range(tk // PAGE_SIZE)], axis=1)
        acc = alpha * acc + _dot(vst, p.astype(BF16))
        return m_new, l, acc, sc_next

    n_tiles = (start + Q_BLOCK + tk - 1) // tk
    m0 = jnp.full((1, n_cols), NEG_INF, F32)
    l0 = jnp.zeros((1, n_cols), F32)
    a0 = jnp.zeros((HEAD_DIM, n_cols), F32)
    carry = lax.fori_loop(0, n_tiles - 1, functools.partial(tile, last=False), (m0, l0, a0, scores(0)))
    _, l, acc, _ = tile(n_tiles - 1, carry, last=True)
    o_s = acc * (1.0 / jnp.maximum(l, TINY))

    gt = g_ref[...]
    o = gt[0:1, :] * o_c + gt[1:2, :] * o_s + gt[2:3, :] * o_w
    o_ref[...] = jnp.concatenate([o[:, g * Q_BLOCK:(g + 1) * Q_BLOCK].T for g in range(GROUP)],
                                 axis=1).astype(o_ref.dtype)


def _values_t(y_b, bsz, t, col0):
    v = lax.slice(y_b, (0, col0), (bsz * t, col0 + NKV_COLS))
    v = v.reshape(bsz, t // PAGE_SIZE, PAGE_SIZE, N_KV_HEADS, HEAD_DIM)
    return v.transpose(0, 3, 1, 4, 2)


def _nsa_prompt(y_main, y_b, cmp_p, gates, bsz, t, rest):
    assert t % SEL_KEY_TILE == 0 and t >= WINDOW + Q_BLOCK and SEL_KEY_TILE % PAGE_SIZE == 0
    np_tok = bsz * t
    nq = t // Q_BLOCK
    n_cr = t // CMP_STRIDE
    n_c = (t - CMP_BLOCK) // CMP_STRIDE + 1
    n_s = t // SEL_BLOCK
    assert n_s >= N_SEL and n_s % 8 == 0
    ovt = _overlap_shifted(n_cr, n_c, n_s, n_s).T
    n_pg = t // PAGE_SIZE
    v_sel_t = _values_t(y_b, bsz, t, NQ_COLS + 3 * NKV_COLS)
    v_win_t = _values_t(y_b, bsz, t, NQ_COLS + 5 * NKV_COLS)
    cmp_t = cmp_p.transpose(0, 2, 1)
    cb = NQ_COLS // HEAD_DIM
    g_t = gates[:np_tok].reshape(bsz, nq, Q_BLOCK, N_KV_HEADS, GROUP, 3).transpose(0, 3, 1, 5, 4, 2)
    g_t = g_t.reshape(bsz, N_KV_HEADS, nq, 3, GROUP * Q_BLOCK)
    seq_key = lambda c4: pl.BlockSpec((t, HEAD_DIM), lambda b, k, i: (b, cb + c4 * N_KV_HEADS + k))
    seq_val = pl.BlockSpec((None, None, n_pg, HEAD_DIM, PAGE_SIZE), lambda b, k, i: (b, k, 0, 0, 0))
    return pl.pallas_call(
        functools.partial(_nsa_prompt_kernel, t=t),
        grid=(bsz, N_KV_HEADS, nq),
        in_specs=[pl.BlockSpec((Q_BLOCK, QW), lambda b, k, i: (b * nq + i, k)),
                  pl.BlockSpec((None, n_cr, HEAD_DIM), lambda b, k, i: (b, 0, k)),
                  pl.BlockSpec((None, HEAD_DIM, n_cr), lambda b, k, i: (b, N_KV_HEADS + k, 0)),
                  seq_key(2), seq_val, seq_key(4), seq_val,
                  pl.BlockSpec((None, None, None, 3, GROUP * Q_BLOCK), lambda b, k, i: (b, k, i, 0, 0)),
                  pl.BlockSpec((n_s, n_cr), lambda b, k, i: (0, 0)),
                  pl.BlockSpec(memory_space=pl.ANY)],
        out_specs=pl.BlockSpec((Q_BLOCK, QW), lambda b, k, i: (b * nq + i, k)),
        out_shape=jax.ShapeDtypeStruct(rest.shape, rest.dtype),
        input_output_aliases={9: 0},
        scratch_shapes=[pltpu.VMEM((n_s, Q_BLOCK), F32)],
        compiler_params=_params("parallel", "parallel", "arbitrary"),
    )(y_main, cmp_p, cmp_t, y_b, v_sel_t, y_b, v_win_t, g_t, ovt, rest)


SAMPLE_ROWS = 8


def _nsa_sample_kernel(pt_ref, y_ref, cmp_ref, cache_ref, win_ref, g_ref, ov_ref, o_ref,
                       q_scr, sel_scr, m_scr, l_scr, acc_scr, oc_scr, kv_buf, kv_sem,
                       *, layer, past, s_len, n_s, n_in):
    pg = pl.program_id(1)
    n_pages = pl.num_programs(1)
    step = pl.program_id(0) * n_pages + pg
    cur = step % 2

    def page_copies(seq, grp, buf):
        out = []
        for i in range(n_in):
            page = pt_ref[seq, grp * n_in + i]
            for s in range(2 * N_KV_HEADS):
                out.append(pltpu.make_async_copy(
                    cache_ref.at[layer, page, :, 2 * N_KV_HEADS + s, :],
                    kv_buf.at[buf, s, pl.ds(i * PAGE_SIZE, PAGE_SIZE), :],
                    kv_sem.at[buf]))
        return out

    @pl.when(step == 0)
    def _():
        for cp in page_copies(0, 0, 0):
            cp.start()

    @pl.when(step + 1 < pl.num_programs(0) * n_pages)
    def _():
        nxt = step + 1
        for cp in page_copies(nxt // n_pages, nxt % n_pages, 1 - cur):
            cp.start()

    for cp in page_copies(pl.program_id(0), pg, cur):
        cp.wait()

    sr = SAMPLE_ROWS
    rk = GROUP * sr
    scale = HEAD_DIM ** -0.5
    tpos = lax.broadcasted_iota(jnp.int32, (rk, 1), 0) % sr
    qpos = past + tpos
    n_sc = sel_scr.shape[1]
    nbp = past // SEL_BLOCK
    kv0 = NQ_COLS

    @pl.when(pg == 0)
    def _():
        y = y_ref[...]
        n_cr = cmp_ref.shape[0]
        m_idx = lax.broadcasted_iota(jnp.int32, (1, n_cr), 1)
        cmask = (m_idx >= 1) & (m_idx * CMP_STRIDE + (CMP_BLOCK - CMP_STRIDE - 1) <= qpos)
        psums = []
        for k in range(N_KV_HEADS):
            qk = jnp.concatenate([y[:, (k * GROUP + g) * HEAD_DIM:(k * GROUP + g + 1) * HEAD_DIM]
                                  for g in range(GROUP)], axis=0) * scale
            qk = qk.astype(BF16)
            q_scr[k] = qk
            kc = cmp_ref[:, k * HEAD_DIM:(k + 1) * HEAD_DIM].astype(BF16)
            vc = cmp_ref[:, (N_KV_HEADS + k) * HEAD_DIM:(N_KV_HEADS + k + 1) * HEAD_DIM].astype(BF16)
            p_c = _masked_softmax(_dot_nt(qk, kc), cmask)
            oc_scr[k * rk:(k + 1) * rk, :] = _dot(p_c.astype(BF16), vc)
            psum = p_c[0:sr]
            for g in range(1, GROUP):
                psum = psum + p_c[g * sr:(g + 1) * sr]
            psums.append(psum)
        lanes = VREG_LANES
        psum_all = jnp.concatenate(psums + [jnp.zeros((lanes - N_KV_HEADS * sr, psums[0].shape[1]), F32)], axis=0)
        imp_t = _dot_nt(ov_ref[...], psum_all, HI)
        qpos_sel = past + lax.broadcasted_iota(jnp.int32, (1, lanes), 1) % sr
        sel = _select_blocks_cols(imp_t, qpos_sel, n_s).T
        for k in range(N_KV_HEADS):
            for g in range(GROUP):
                sel_scr[k * rk + g * sr:k * rk + (g + 1) * sr, :] = sel[k * sr:(k + 1) * sr]
        m_scr[...] = jnp.full(m_scr.shape, NEG_INF, F32)
        l_scr[...] = jnp.zeros(l_scr.shape, F32)
        acc_scr[...] = jnp.zeros(acc_scr.shape, F32)

    def online_update(k, sc, mask, v):
        rows = slice(k * rk, (k + 1) * rk)
        sc = jnp.where(mask, sc, NEG_INF)
        m_old = m_scr[rows, :]
        m_new = jnp.maximum(m_old, jnp.max(sc, axis=-1, keepdims=True))
        alpha = jnp.exp(m_old - m_new)
        p = jnp.where(mask, jnp.exp(sc - m_new), 0.0)
        l_scr[rows, :] = alpha * l_scr[rows, :] + jnp.sum(p, axis=-1, keepdims=True)
        acc_scr[rows, :] = alpha * acc_scr[rows, :] + _dot(p.astype(v.dtype), v)
        m_scr[rows, :] = m_new

    n_keys = n_in * PAGE_SIZE
    kpos = pg * n_keys + lax.broadcasted_iota(jnp.int32, (1, n_keys), 1)
    blk_row = lax.broadcasted_iota(jnp.int32, (n_sc, 1), 0)
    expand = jnp.where(blk_row == kpos // SEL_BLOCK, 1.0, 0.0).astype(BF16)
    selx = _dot(sel_scr[...].astype(BF16), expand)
    for k in range(N_KV_HEADS):
        kp = kv_buf[cur, k].astype(BF16)
        vp = kv_buf[cur, N_KV_HEADS + k].astype(BF16)
        mask = (selx[k * rk:(k + 1) * rk] > 0.5) & (kpos <= qpos)
        online_update(k, _dot_nt(q_scr[k], kp), mask, vp)

    @pl.when(pg == n_pages - 1)
    def _():
        y = y_ref[...]
        rpos = lax.broadcasted_iota(jnp.int32, (1, sr), 1)
        new_ok = (rpos < s_len) & (rpos <= tpos)
        gt = g_ref[...]
        win_by_slot = jnp.swapaxes(win_ref[...], 0, 1).astype(BF16)
        for k in range(N_KV_HEADS):
            rows = slice(k * rk, (k + 1) * rk)
            qk = q_scr[k]
            qf = qk.astype(F32)
            c_ks = kv0 + (2 * N_KV_HEADS + k) * HEAD_DIM
            c_vs = kv0 + (3 * N_KV_HEADS + k) * HEAD_DIM
            kn = y[:, c_ks:c_ks + HEAD_DIM]
            vn = y[:, c_vs:c_vs + HEAD_DIM]
            new_sel = sel_scr[rows, nbp:nbp + 1] > 0.5
            online_update(k, _dot_nt(qf, kn), new_sel & new_ok, vn)
            o_s = acc_scr[rows, :] / jnp.maximum(l_scr[rows, :], TINY)
            wb = win_ref.shape[0]
            c_kw = kv0 + (4 * N_KV_HEADS + k) * HEAD_DIM
            c_vw = kv0 + (5 * N_KV_HEADS + k) * HEAD_DIM
            kwb = win_by_slot[k]
            vwb = win_by_slot[N_KV_HEADS + k]
            kwn = y[:, c_kw:c_kw + HEAD_DIM]
            vwn = y[:, c_vw:c_vw + HEAD_DIM]
            dpos = qpos - (past - wb + lax.broadcasted_iota(jnp.int32, (1, wb), 1))
            mask_b = (dpos >= 0) & (dpos < WINDOW)
            s_b = jnp.where(mask_b, _dot_nt(qk, kwb), NEG_INF)
            s_n = jnp.where(new_ok, _dot_nt(qf, kwn), NEG_INF)
            mw = jnp.maximum(jnp.max(s_b, axis=-1, keepdims=True), jnp.max(s_n, axis=-1, keepdims=True))
            p_b = jnp.where(mask_b, jnp.exp(s_b - mw), 0.0)
            p_n = jnp.where(new_ok, jnp.exp(s_n - mw), 0.0)
            den = jnp.sum(p_b, axis=-1, keepdims=True) + jnp.sum(p_n, axis=-1, keepdims=True)
            o_w = (_dot(p_b.astype(BF16), vwb) + _dot(p_n, vwn)) / jnp.maximum(den, TINY)
            o_c = oc_scr[rows, :]
            for g in range(GROUP):
                r0, r1 = g * sr, (g + 1) * sr
                c = (k * GROUP + g) * 3
                o = gt[:, c:c + 1] * o_c[r0:r1] + gt[:, c + 1:c + 2] * o_s[r0:r1] + gt[:, c + 2:c + 3] * o_w[r0:r1]
                o_ref[:, (k * GROUP + g) * HEAD_DIM:(k * GROUP + g + 1) * HEAD_DIM] = o


SAMPLE_PAGES_PER_STEP = 8


def _nsa_sample(ys8, cmp_s, cache5, layer, win4, gates8, page_table, s_len):
    db, n_pages = page_table.shape
    past = n_pages * PAGE_SIZE
    n_in = math.gcd(n_pages, SAMPLE_PAGES_PER_STEP)
    assert s_len <= SAMPLE_ROWS and s_len <= SEL_BLOCK and past % SEL_BLOCK == 0
    n_cr = cmp_s.shape[1]
    n_c = (past + s_len - CMP_BLOCK) // CMP_STRIDE + 1
    assert n_c + 1 <= n_cr
    n_s = -(-(past + s_len) // SEL_BLOCK)
    n_sc = -(-n_s // VREG_LANES) * VREG_LANES
    ov = _overlap_shifted(n_cr, n_c, n_s, n_sc).T
    wb = win4.shape[1]
    ncol = ys8.shape[2]
    rows = N_KV_HEADS * GROUP * SAMPLE_ROWS
    grid_spec = pltpu.PrefetchScalarGridSpec(
        num_scalar_prefetch=1,
        grid=(db, n_pages // n_in),
        in_specs=[pl.BlockSpec((None, SAMPLE_ROWS, ncol), lambda b, p, pt: (b, 0, 0)),
                  pl.BlockSpec((None, n_cr, 2 * NKV_COLS), lambda b, p, pt: (b, 0, 0)),
                  pl.BlockSpec(memory_space=pl.ANY),
                  pl.BlockSpec((None, wb, 2 * N_KV_HEADS, HEAD_DIM), lambda b, p, pt: (b, 0, 0, 0)),
                  pl.BlockSpec((None, SAMPLE_ROWS, 3 * N_HEADS), lambda b, p, pt: (b, 0, 0)),
                  pl.BlockSpec((n_sc, n_cr), lambda b, p, pt: (0, 0))],
        out_specs=pl.BlockSpec((None, SAMPLE_ROWS, NQ_COLS), lambda b, p, pt: (b, 0, 0)),
        scratch_shapes=[pltpu.VMEM((N_KV_HEADS, GROUP * SAMPLE_ROWS, HEAD_DIM), BF16),
                        pltpu.VMEM((rows, n_sc), F32),
                        pltpu.VMEM((rows, 1), F32),
                        pltpu.VMEM((rows, 1), F32),
                        pltpu.VMEM((rows, HEAD_DIM), F32),
                        pltpu.VMEM((rows, HEAD_DIM), F32),
                        pltpu.VMEM((2, 2 * N_KV_HEADS, n_in * PAGE_SIZE, HEAD_DIM), F32),
                        pltpu.SemaphoreType.DMA((2,))],
    )
    return pl.pallas_call(
        functools.partial(_nsa_sample_kernel, layer=layer, past=past, s_len=s_len, n_s=n_s, n_in=n_in),
        grid_spec=grid_spec,
        out_shape=jax.ShapeDtypeStruct((db, SAMPLE_ROWS, NQ_COLS), F32),
        compiler_params=_params("arbitrary", "arbitrary"),
    )(page_table, ys8, cmp_s, cache5, win4, gates8, ov)


WKV_DIAG_BLOCK = 16
WKV_CHUNK = 64


def _bdot(a, b):
    return _dot(a.astype(BF16), b.astype(BF16))


def _bdot_nt(a, b):
    return _dot_nt(a.astype(BF16), b.astype(BF16))


def _bdot_tn(a, b):
    return _dot_tn(a.astype(BF16), b.astype(BF16))


def _unit_lower_inverses(mats, c):
    blk = min(WKV_DIAG_BLOCK, c)
    ri = lax.broadcasted_iota(jnp.int32, (c, c), 0)
    ci = lax.broadcasted_iota(jnp.int32, (c, c), 1)
    eye = jnp.where(ri == ci, 1.0, 0.0)
    same = (ri // blk) == (ci // blk)
    d = [jnp.where(same, a, 0.0) for a in mats]
    td = [eye + x for x in d]
    n_sq = int(math.log2(blk)) - 1
    pw = [_bdot(x, x) for x in d]
    stack = c >= WKV_DIAG_BLOCK
    for level in range(n_sq):
        if level + 1 < n_sq and stack:
            both = [_bdot(jnp.concatenate([p, t], axis=0), p) for p, t in zip(pw, td)]
            pw = [x[:c] for x in both]
            td = [t + x[c:] for t, x in zip(td, both)]
        else:
            td = [t + _bdot(t, p) for t, p in zip(td, pw)]
            if level + 1 < n_sq:
                pw = [_bdot(p, p) for p in pw]
    if blk == c:
        return td
    npow = [_bdot(t, jnp.where(same, 0.0, a)) for t, a in zip(td, mats)]
    tinv = [t + _bdot(x, t) for x, t in zip(npow, td)]
    for _ in range(int(math.log2(c // blk)) - 1):
        npow = [_bdot(x, x) for x in npow]
        tinv = [t + _bdot(x, t) for x, t in zip(npow, tinv)]
    return tinv


def _wkv_kernel(r_ref, k_ref, v_ref, lw_ref, a_ref, g_ref, kk_ref, ka_ref, rk_ref, lg_ref, lb_ref, s0_ref,
                rest_ref, z_ref, s_ref, *, c, nc, hb):
    del rest_ref
    n = RWKV_HEAD

    @pl.when(pl.program_id(2) == 0)
    def _():
        s_ref[...] = s0_ref[...]

    ri = lax.broadcasted_iota(jnp.int32, (c, c), 0)
    ci = lax.broadcasted_iota(jnp.int32, (c, c), 1)
    strict = ri > ci
    incl = ri >= ci
    tri = jnp.where(incl, 1.0, 0.0)
    lw_all = lw_ref[...]
    cum = jnp.concatenate([_dot(tri, lw_all[i * c:(i + 1) * c], HI) for i in range(nc)], axis=0)
    e_in_all = jnp.exp(cum)
    e_neg_all = jnp.exp(-cum)
    e_ex_all = jnp.exp(cum - lw_all)
    heads = range(hb)
    chunks = range(nc)
    sls = [slice(hh * n, (hh + 1) * n) for hh in heads]
    rws = [slice(i * c, (i + 1) * c) for i in chunks]
    r, v, k2, r_t, a_t, b_t, k_t, vb = ([] for _ in range(8))
    for sl in sls:
        k = k_ref[:, sl]
        a = a_ref[:, sl]
        kk = k * kk_ref[:, sl]
        kk = kk * (1.0 / jnp.maximum(jnp.sqrt(jnp.sum(kk * kk, axis=-1, keepdims=True)), 1e-12))
        r.append(r_ref[:, sl])
        v.append(v_ref[:, sl])
        k2.append(k * (1.0 + (a - 1.0) * ka_ref[:, sl]))
        r_t.append(r[-1] * e_in_all[:, sl])
        a_t.append(-kk * e_ex_all[:, sl])
        b_t.append(kk * a * e_neg_all[:, sl])
        k_t.append(k2[-1] * e_neg_all[:, sl])
        vb.append(v[-1].astype(BF16))

    items = [(hh, i) for hh in heads for i in chunks]
    aa = [_bdot_nt(jnp.concatenate([a_t[hh][rws[i]], r_t[hh][rws[i]]], axis=0),
                   jnp.concatenate([b_t[hh][rws[i]], k_t[hh][rws[i]]], axis=0)) for hh, i in items]
    r2 = lax.broadcasted_iota(jnp.int32, (2 * c, c), 0)
    c2 = lax.broadcasted_iota(jnp.int32, (2 * c, c), 1)
    tri2 = jnp.where(r2 < c, r2, r2 - c + 1) > c2
    a_ab = [jnp.where(strict, x[:c, :c], 0.0) for x in aa]
    a_rb = [jnp.where(incl, x[c:, :c], 0.0).astype(BF16) for x in aa]
    xkv = [_bdot(jnp.where(tri2, x[:, c:], 0.0), vb[hh][rws[i]]) for x, (hh, i) in zip(aa, items)]
    yv = [x[c:] for x in xkv]
    w_c = [e_in_all[i * c + c - 1:(i + 1) * c, sls[hh]] for hh, i in items]
    kv = [_bdot_tn(vb[hh][rws[i]], k_t[hh][rws[i]] * w) for w, (hh, i) in zip(w_c, items)]
    bh = [(b_t[hh][rws[i]] * w).astype(BF16) for w, (hh, i) in zip(w_c, items)]
    tinv = _unit_lower_inverses(a_ab, c)
    wt = [_bdot(t, a_t[hh][rws[i]]) for t, (hh, i) in zip(tinv, items)]
    u = [_bdot(t, x[:c]) for t, x in zip(tinv, xkv)]
    wr = [jnp.concatenate([w, r_t[hh][rws[i]]], axis=0).astype(BF16) for w, (hh, i) in zip(wt, items)]

    s = [s_ref[hh] for hh in heads]
    ys = [[] for _ in heads]
    for i in chunks:
        idx = [hh * nc + i for hh in heads]
        ls = [_dot_nt(wr[j], s[hh].astype(BF16)) for hh, j in zip(heads, idx)]
        p = [x[:c] + u[j] for x, j in zip(ls, idx)]
        for hh, j in zip(heads, idx):
            ys[hh].append(ls[hh][c:] + _bdot(a_rb[j], p[hh]) + yv[j])
        s = [s[hh] * w_c[j] + _bdot_tn(p[hh], bh[j]) + kv[j] for hh, j in zip(heads, idx)]
    zs = []
    for hh, sl in zip(heads, sls):
        s_ref[hh] = s[hh]
        y = ys[hh][0] if nc == 1 else jnp.concatenate(ys[hh], axis=0)
        mu = jnp.mean(y, axis=-1, keepdims=True)
        yc = y - mu
        yn = yc * lax.rsqrt(jnp.mean(yc * yc, axis=-1, keepdims=True) + LNX_EPS)
        yn = yn * lg_ref[:, sl] + lb_ref[:, sl]
        bonus = jnp.sum(r[hh] * k2[hh] * rk_ref[:, sl], axis=-1, keepdims=True) * v[hh]
        zs.append((yn + bonus) * g_ref[:, sl])
    z_ref[...] = jnp.concatenate(zs, axis=1).astype(z_ref.dtype)


def _wkv(rkv, lw, a, g, k_k, k_a, r_k, lnx_g, lnx_b, s0, rest, n_seq, t, c, nc, hb):
    rows, d = lw.shape
    h = d // RWKV_HEAD
    hw = hb * RWKV_HEAD
    tb = nc * c
    nblk = t // tb
    tok = lambda s, hg, ch: (s * nblk + ch, hg)
    par = lambda s, hg, ch: (0, hg)
    st = lambda s, hg, ch: (s, hg, 0, 0)
    rkv_spec = lambda p: pl.BlockSpec((None, tb, hw), lambda s, hg, ch: (p, s * nblk + ch, hg))
    return pl.pallas_call(
        functools.partial(_wkv_kernel, c=c, nc=nc, hb=hb),
        grid=(n_seq, h // hb, nblk),
        in_specs=[rkv_spec(0), rkv_spec(1), rkv_spec(2),
                  pl.BlockSpec((tb, hw), tok), pl.BlockSpec((tb, hw), tok), pl.BlockSpec((tb, hw), tok),
                  pl.BlockSpec((1, hw), par), pl.BlockSpec((1, hw), par), pl.BlockSpec((1, hw), par),
                  pl.BlockSpec((1, hw), par), pl.BlockSpec((1, hw), par),
                  pl.BlockSpec((None, hb, RWKV_HEAD, RWKV_HEAD), st),
                  pl.BlockSpec(memory_space=pl.ANY)],
        out_specs=[pl.BlockSpec((tb, hw), tok),
                   pl.BlockSpec((None, hb, RWKV_HEAD, RWKV_HEAD), st)],
        out_shape=[jax.ShapeDtypeStruct((rows, d), BF16),
                   jax.ShapeDtypeStruct((n_seq, h, RWKV_HEAD, RWKV_HEAD), F32)],
        input_output_aliases={12: 0},
        compiler_params=_params("parallel", "parallel", "arbitrary"),
    )(rkv, rkv, rkv, lw, a, g, k_k, k_a, r_k, lnx_g, lnx_b, s0, rest)


WKV_PROMPT_CHUNKS_PER_STEP = 2
WKV_PROMPT_HEADS_PER_STEP = 8


def _to_slots_kernel(x_ref, o_ref):
    x = x_ref[...]
    by_slot = jnp.stack([x[:, s * HEAD_DIM:(s + 1) * HEAD_DIM] for s in range(o_ref.shape[1])], axis=0)
    o_ref[...] = jnp.swapaxes(by_slot, 0, 1)


def _to_slots(y, rows, col0, n_slots, *, tm_pref=512):
    width = n_slots * HEAD_DIM
    assert col0 % width == 0
    tm = _tile(rows, tm_pref)
    return pl.pallas_call(
        _to_slots_kernel,
        grid=(rows // tm,),
        in_specs=[pl.BlockSpec((tm, width), lambda i: (i, col0 // width))],
        out_specs=pl.BlockSpec((tm, n_slots, HEAD_DIM), lambda i: (i, 0, 0)),
        out_shape=jax.ShapeDtypeStruct((rows, n_slots, HEAD_DIM), y.dtype),
        compiler_params=_params("parallel"),
    )(y)


def _nsa_layer(x, np_tok, bsz, t, db, s_len, cache5, layer, win, page_table, w_in, pe, w1, w2, w_out,
               ln_g, ln_b, alpha):
    n_main = NQ_COLS + 6 * NKV_COLS
    y_main, y_b = _mm(x, w_in, layer, n_main, out_dtypes=(F32, BF16), tm_pref=832, tn_pref=1024)
    w_gate = jnp.pad(w_in[layer:layer + 1, :, n_main:], ((0, 0), (0, 0), (0, VREG_LANES - 3 * N_HEADS)))
    gates = _mm(x, w_gate, 0, VREG_LANES, act="sigmoid")[:, :3 * N_HEADS]

    wc, w2b = _cmp_weights(w1, w2)
    bias = _cmp_bias(pe, w1)
    cmp_p = _cmp_prompt(y_main, bsz, t, wc, bias, w2b)
    cmp_s = _cmp_sample(cache5, layer, page_table, wc, bias, w2b)

    pad_rows = ((0, 0), (0, SAMPLE_ROWS - s_len), (0, 0))
    ys = y_main[np_tok:].reshape(db, s_len, n_main)
    ys8 = jnp.pad(ys, pad_rows)
    gates8 = jnp.pad(gates[np_tok:].reshape(db, s_len, 3 * N_HEADS), pad_rows)
    wb = win.shape[1]
    win4 = win.reshape(db, wb, 2 * N_KV_HEADS, HEAD_DIM)
    o_s = _nsa_sample(ys8, cmp_s, cache5, layer, win4, gates8, page_table, s_len)
    o_s = o_s[:, :s_len].reshape(db * s_len, NQ_COLS).astype(BF16)

    o = _nsa_prompt(y_main, y_b, cmp_p, gates, bsz, t, jnp.pad(o_s, ((np_tok, 0), (0, 0))))
    h = _mm_ln_cols(o, w_out, layer, x, ln_g, ln_b, alpha)

    kv0 = NQ_COLS
    kw0 = NQ_COLS + 4 * NKV_COLS
    kv_p = _to_slots(y_main, np_tok, kv0, 4 * N_KV_HEADS).reshape(bsz, t, 4, N_KV_HEADS, HEAD_DIM)
    wlen = min(WINDOW, t)
    win_p = jnp.stack([lax.slice(y_main, (b * t + t - wlen, kw0), ((b + 1) * t, n_main)) for b in range(bsz)])
    win_p = win_p.reshape(bsz, wlen, 2, N_KV_HEADS, HEAD_DIM)
    kv_s = ys[:, :, kv0:kw0].reshape(db, s_len, 4, N_KV_HEADS, HEAD_DIM)
    kvw_s = ys[:, :, kw0:].reshape(db, s_len, 2, N_KV_HEADS, HEAD_DIM)
    win_s = jnp.concatenate([win, kvw_s], axis=1)[:, s_len:]
    return h, kv_p, kv_s, win_p, win_s


def _rwkv_layer(x, np_tok, bsz, t, db, s_len, shift_s, wkv_s, mu, w_rkv, w0, w1, w2, a0, a1, a2, g1, g2,
                k_k, k_a, r_k, lnx_g, lnx_b, w_out, layer, ln_g, ln_b, alpha):
    d = x.shape[1]
    heads = d // RWKV_HEAD
    xprev = jnp.concatenate([jnp.zeros((1, d), x.dtype), x[:-1]], axis=0)
    xprev = xprev.at[np.arange(1, bsz) * t].set(0.0)
    xprev = xprev.at[np_tok + np.arange(db) * s_len].set(shift_s.astype(x.dtype))

    row = lambda v: v.reshape(1, -1)
    rkv = _mm_mix(x, xprev, mu[:3, None, :], w_rkv)
    lw, a, g = _loras(x, xprev, mu[3:6], w1, w2, row(w0), a1, a2, row(a0), g1, g2)

    pvec = (row(k_k), row(k_a), row(r_k), row(lnx_g), row(lnx_b))
    c_p = _tile(t, WKV_CHUNK, 8)
    nc_p = math.gcd(t // c_p, WKV_PROMPT_CHUNKS_PER_STEP)
    hb_p = math.gcd(heads, WKV_PROMPT_HEADS_PER_STEP)
    zero_state = jnp.zeros((bsz, heads, RWKV_HEAD, RWKV_HEAD), F32)
    c_s = -(-s_len // 8) * 8
    pad = lambda v: jnp.pad(v.reshape(v.shape[:-2] + (db, s_len, d)),
                            [(0, 0)] * (v.ndim - 1) + [(0, c_s - s_len), (0, 0)]
                            ).reshape(v.shape[:-2] + (db * c_s, d))
    z_s, st_s = _wkv(pad(rkv[:, np_tok:]), pad(lw[np_tok:]), pad(a[np_tok:]), pad(g[np_tok:]), *pvec,
                     wkv_s.astype(F32), jnp.zeros((db * c_s, d), BF16), db, c_s, c_s, 1, heads)
    z_s = z_s.reshape(db, c_s, d)[:, :s_len].reshape(db * s_len, d)

    z, st_p = _wkv(rkv, lw, a, g, *pvec, zero_state, jnp.pad(z_s, ((np_tok, 0), (0, 0))), bsz, t, c_p, nc_p, hb_p)
    h = _mm_ln_cols(z, w_out, layer, x, ln_g, ln_b, alpha)
    last_p = x[np.arange(1, bsz + 1) * t - 1]
    last_s = x[np_tok + np.arange(1, db + 1) * s_len - 1]
    return h, st_p, st_s, last_p, last_s


def _mlp(h, w_up, w_down, layer, ln_g, ln_b, alpha):
    d_ff = w_up.shape[2]
    u = _mm(h, w_up, layer, d_ff, act="relu2", out_dtypes=(BF16,))
    return _mm_ln(u, w_down, layer, h, ln_g, ln_b, alpha)


def kernel(x_prompt, x_sample, cache_nsa_kv, state_nsa_win, state_rwkv_wkv, state_rwkv_shift, page_table,
           nsa_w_in, nsa_cmp_pe, nsa_cmp_w1, nsa_cmp_w2, nsa_w_out,
           rwkv_mu, rwkv_w_rkv, rwkv_w0, rwkv_w1, rwkv_w2, rwkv_a0, rwkv_a1, rwkv_a2, rwkv_g1, rwkv_g2,
           rwkv_k_k, rwkv_k_a, rwkv_r_k, rwkv_lnx_g, rwkv_lnx_b, rwkv_w_out,
           ffn_w_up, ffn_w_down, ln_g, ln_b):
    bsz, t, d = x_prompt.shape
    db, s_len, _ = x_sample.shape
    depth = ffn_w_up.shape[0]
    alpha = (2 * depth) ** 0.25
    np_tok = bsz * t
    x = jnp.concatenate([x_prompt.reshape(np_tok, d), x_sample.reshape(db * s_len, d)], axis=0)
    n_l, n_pool = cache_nsa_kv.shape[:2]
    cache5 = cache_nsa_kv.reshape(n_l, n_pool, PAGE_SIZE, 4 * N_KV_HEADS, HEAD_DIM)
    page_table = page_table.astype(jnp.int32)

    kv_p, kv_s, win_p, win_s, wkv_p, wkv_s, sh_p, sh_s = ([] for _ in range(8))
    row = lambda v: v.reshape(1, -1)
    for i in range(depth):
        j = i // 2
        if i % 2 == 0:
            h, kvp_new, kvs_new, wp_new, ws_new = _nsa_layer(
                x, np_tok, bsz, t, db, s_len, cache5, j, state_nsa_win[j], page_table,
                nsa_w_in, nsa_cmp_pe[j], nsa_cmp_w1[j], nsa_cmp_w2[j], nsa_w_out,
                row(ln_g[i, 0]), row(ln_b[i, 0]), alpha)
            kv_p.append(kvp_new)
            kv_s.append(kvs_new)
            win_p.append(wp_new)
            win_s.append(ws_new)
        else:
            h, sp_new, ss_new, hp_new, hs_new = _rwkv_layer(
                x, np_tok, bsz, t, db, s_len, state_rwkv_shift[j], state_rwkv_wkv[j],
                rwkv_mu[j], rwkv_w_rkv[j], rwkv_w0[j], rwkv_w1[j], rwkv_w2[j], rwkv_a0[j], rwkv_a1[j],
                rwkv_a2[j], rwkv_g1[j], rwkv_g2[j], rwkv_k_k[j], rwkv_k_a[j], rwkv_r_k[j],
                rwkv_lnx_g[j], rwkv_lnx_b[j], rwkv_w_out, j, row(ln_g[i, 0]), row(ln_b[i, 0]), alpha)
            wkv_p.append(sp_new)
            wkv_s.append(ss_new)
            sh_p.append(hp_new)
            sh_s.append(hs_new)
        x = _mlp(h, ffn_w_up, ffn_w_down, i, row(ln_g[i, 1]), row(ln_b[i, 1]), alpha)
    return (x[:np_tok].reshape(bsz, t, d), x[np_tok:].reshape(db, s_len, d),
            jnp.stack(kv_p), jnp.stack(kv_s), jnp.stack(win_p), jnp.stack(win_s),
            jnp.stack(wkv_p), jnp.stack(wkv_s), jnp.stack(sh_p), jnp.stack(sh_s))
```

```python
import functools
import math

import jax
import jax.numpy as jnp
import numpy as np
from jax import lax
from jax.experimental import pallas as pl
from jax.experimental.pallas import tpu as pltpu

F32 = jnp.float32
BF16 = jnp.bfloat16

HEAD_DIM = 128
N_KV_HEADS = 4
GROUP = 4
N_HEADS = N_KV_HEADS * GROUP
QW = GROUP * HEAD_DIM
NQ_COLS = N_HEADS * HEAD_DIM
NKV_COLS = N_KV_HEADS * HEAD_DIM
CMP_BLOCK = 32
CMP_STRIDE = 16
SEL_BLOCK = 64
N_SEL = 16
WINDOW = 512
Q_BLOCK = 256
PAGE_SIZE = 128
RWKV_HEAD = 64
LN_EPS = 1e-5
LNX_EPS = 64e-5
FORCE = 1e4
NEG_INF = -1e30
TINY = 1e-30
CHUNKS_PER_PAGE = PAGE_SIZE // CMP_STRIDE

VMEM_LIMIT_BYTES = 56 * 1024 * 1024
VREG_LANES = 128
HI = lax.Precision.HIGHEST


def _params(*sem):
    return pltpu.CompilerParams(dimension_semantics=sem, vmem_limit_bytes=VMEM_LIMIT_BYTES)


def _tile(n, pref, mult=16):
    if n <= pref:
        return n
    for t in range(pref - pref % mult, 0, -mult):
        if n % t == 0:
            return t
    return n


def _dot(a, b, precision=None):
    return jnp.dot(a, b, preferred_element_type=F32, precision=precision)


def _dot_nt(a, b, precision=None):
    return lax.dot_general(a, b, (((1,), (1,)), ((), ())), preferred_element_type=F32, precision=precision)


def _dot_tn(a, b, precision=None):
    return lax.dot_general(a, b, (((0,), (0,)), ((), ())), preferred_element_type=F32, precision=precision)


def _masked_softmax(s, mask):
    s = jnp.where(mask, s, NEG_INF)
    m = jnp.max(s, axis=-1, keepdims=True)
    p = jnp.where(mask, jnp.exp(s - m), 0.0)
    return p * (1.0 / jnp.maximum(jnp.sum(p, axis=-1, keepdims=True), TINY))


def _layer_norm_rows(s, g, b, eps):
    mu = jnp.mean(s, axis=-1, keepdims=True)
    c = s - mu
    var = jnp.mean(c * c, axis=-1, keepdims=True)
    return c * lax.rsqrt(var + eps) * g + b


def _mm_kernel(x_ref, w_ref, *refs, act):
    out_refs, xb_ref = refs[:-1], refs[-1]

    @pl.when(pl.program_id(1) == 0)
    def _():
        xb_ref[...] = x_ref[...].astype(BF16)

    y = _dot(xb_ref[...], w_ref[...].astype(BF16))
    if act == "relu2":
        y = jnp.maximum(y, 0.0)
        y = y * y
    elif act == "sigmoid":
        y = jax.nn.sigmoid(y)
    for o_ref in out_refs:
        o_ref[...] = y.astype(o_ref.dtype)


def _mm(x, w, layer, n_out, *, act=None, out_dtypes=(F32,), tm_pref=1040, tn_pref=1024):
    m, k = x.shape
    tm = _tile(m, tm_pref)
    tn = _tile(n_out, tn_pref, 128)
    outs = pl.pallas_call(
        functools.partial(_mm_kernel, act=act),
        grid=(m // tm, n_out // tn),
        in_specs=[pl.BlockSpec((tm, k), lambda i, j: (i, 0)),
                  pl.BlockSpec((None, k, tn), lambda i, j: (layer, 0, j))],
        out_specs=[pl.BlockSpec((tm, tn), lambda i, j: (i, j)) for _ in out_dtypes],
        out_shape=[jax.ShapeDtypeStruct((m, n_out), dt) for dt in out_dtypes],
        scratch_shapes=[pltpu.VMEM((tm, k), BF16)],
        compiler_params=_params("parallel", "arbitrary"),
    )(x, w)
    return outs[0] if len(outs) == 1 else outs


def _mm_mix_kernel(x_ref, xp_ref, mu_ref, w_ref, o_ref, xb_ref):
    @pl.when(pl.program_id(2) == 0)
    def _():
        x = x_ref[...]
        xb_ref[...] = (x + (xp_ref[...] - x) * mu_ref[...]).astype(BF16)

    o_ref[...] = _dot(xb_ref[...], w_ref[...].astype(BF16))


def _mm_mix(x, xp, mu, w, *, tm_pref=1040, tn_pref=512):
    m, k = x.shape
    npar, _, n = w.shape
    tm = _tile(m, tm_pref)
    tn = _tile(n, tn_pref, 128)
    return pl.pallas_call(
        _mm_mix_kernel,
        grid=(m // tm, npar, n // tn),
        in_specs=[pl.BlockSpec((tm, k), lambda i, p, j: (i, 0)),
                  pl.BlockSpec((tm, k), lambda i, p, j: (i, 0)),
                  pl.BlockSpec((None, 1, k), lambda i, p, j: (p, 0, 0)),
                  pl.BlockSpec((None, k, tn), lambda i, p, j: (p, 0, j))],
        out_specs=pl.BlockSpec((None, tm, tn), lambda i, p, j: (p, i, j)),
        out_shape=jax.ShapeDtypeStruct((npar, m, n), F32),
        scratch_shapes=[pltpu.VMEM((tm, k), BF16)],
        compiler_params=_params("parallel", "arbitrary", "arbitrary"),
    )(x, xp, mu, w)


def _loras_kernel(x_ref, xp_ref, mu_ref, w1_ref, w2_ref, w0_ref, a1_ref, a2_ref, a0_ref, g1_ref, g2_ref,
                  lw_ref, a_ref, g_ref):
    x = x_ref[...]
    xx = xp_ref[...] - x
    mix = lambda p: (x + xx * mu_ref[p:p + 1, :]).astype(BF16)
    low = lambda xm, w: _dot(xm, w[...].astype(BF16))
    up = lambda h, w: _dot(h.astype(BF16), w[...].astype(BF16))
    u = -(up(jnp.tanh(low(mix(0), w1_ref)), w2_ref) + w0_ref[...])
    softplus = jnp.maximum(u, 0.0) + jnp.log(1.0 + jnp.exp(-jnp.abs(u)))
    lw_ref[...] = -jnp.exp(-softplus - 0.5)
    a_ref[...] = jax.nn.sigmoid(up(low(mix(1), a1_ref), a2_ref) + a0_ref[...])
    g_ref[...] = up(jax.nn.sigmoid(low(mix(2), g1_ref)), g2_ref)


def _loras(x, xp, mu, w1, w2, w0, a1, a2, a0, g1, g2, *, tm_pref=320):
    m, k = x.shape
    n = w2.shape[1]
    tm = _tile(m, tm_pref)
    row = lambda i: (i, 0)
    fixed = lambda i: (0, 0)
    whole = lambda v: pl.BlockSpec(v.shape, fixed)
    out = jax.ShapeDtypeStruct((m, n), F32)
    return pl.pallas_call(
        _loras_kernel,
        grid=(m // tm,),
        in_specs=[pl.BlockSpec((tm, k), row), pl.BlockSpec((tm, k), row), whole(mu),
                  whole(w1), whole(w2), whole(w0), whole(a1), whole(a2), whole(a0), whole(g1), whole(g2)],
        out_specs=[pl.BlockSpec((tm, n), row)] * 3,
        out_shape=[out, out, out],
        compiler_params=_params("parallel"),
    )(x, xp, mu, w1, w2, w0, a1, a2, a0, g1, g2)


def _mm_ln_kernel(z_ref, w_ref, res_ref, g_ref, b_ref, o_ref, *, alpha):
    kk = pl.program_id(1)

    @pl.when(kk == 0)
    def _():
        o_ref[...] = _dot(z_ref[...].astype(BF16), w_ref[...].astype(BF16))

    @pl.when(kk > 0)
    def _():
        o_ref[...] += _dot(z_ref[...].astype(BF16), w_ref[...].astype(BF16))

    @pl.when(kk == pl.num_programs(1) - 1)
    def _():
        s = alpha * res_ref[...] + o_ref[...]
        o_ref[...] = _layer_norm_rows(s, g_ref[...], b_ref[...], LN_EPS)


def _mm_ln_cols_kernel(z_ref, w_ref, res_ref, g_ref, b_ref, o_ref, *, alpha, tn):
    j = pl.program_id(1)
    col = pl.multiple_of(j * tn, tn)
    o_ref[:, pl.ds(col, tn)] = _dot(z_ref[...], w_ref[...].astype(BF16))

    @pl.when(j == pl.num_programs(1) - 1)
    def _():
        s = alpha * res_ref[...] + o_ref[...]
        o_ref[...] = _layer_norm_rows(s, g_ref[...], b_ref[...], LN_EPS)


def _mm_ln_cols(z, w, layer, res, g, b, alpha, *, tm_pref=832, tn_pref=512):
    m, k = z.shape
    n = w.shape[2]
    tm = _tile(m, tm_pref)
    tn = _tile(n, tn_pref, 128)
    return pl.pallas_call(
        functools.partial(_mm_ln_cols_kernel, alpha=alpha, tn=tn),
        grid=(m // tm, n // tn),
        in_specs=[pl.BlockSpec((tm, k), lambda i, j: (i, 0)),
                  pl.BlockSpec((None, k, tn), lambda i, j: (layer, 0, j)),
                  pl.BlockSpec((tm, n), lambda i, j: (i, 0)),
                  pl.BlockSpec((1, n), lambda i, j: (0, 0)),
                  pl.BlockSpec((1, n), lambda i, j: (0, 0))],
        out_specs=pl.BlockSpec((tm, n), lambda i, j: (i, 0)),
        out_shape=jax.ShapeDtypeStruct((m, n), F32),
        compiler_params=_params("parallel", "arbitrary"),
    )(z, w, res, g, b)


def _mm_ln(z, w, layer, res, g, b, alpha, *, tm_pref=640, tk_pref=1024):
    m, k = z.shape
    n = w.shape[2]
    tm = _tile(m, tm_pref)
    tk = _tile(k, tk_pref, 128)
    return pl.pallas_call(
        functools.partial(_mm_ln_kernel, alpha=alpha),
        grid=(m // tm, k // tk),
        in_specs=[pl.BlockSpec((tm, tk), lambda i, kk: (i, kk)),
                  pl.BlockSpec((None, tk, n), lambda i, kk: (layer, kk, 0)),
                  pl.BlockSpec((tm, n), lambda i, kk: (i, 0)),
                  pl.BlockSpec((1, n), lambda i, kk: (0, 0)),
                  pl.BlockSpec((1, n), lambda i, kk: (0, 0))],
        out_specs=pl.BlockSpec((tm, n), lambda i, kk: (i, 0)),
        out_shape=jax.ShapeDtypeStruct((m, n), F32),
        compiler_params=_params("parallel", "arbitrary"),
    )(z, w, res, g, b)


def _cmp_bias_kernel(pe_ref, w1_ref, o_ref):
    acc = jnp.zeros((1, HEAD_DIM), F32)
    for rj in range(CMP_BLOCK):
        acc = acc + _dot(pe_ref[rj:rj + 1, :], w1_ref[rj], HI)
    o_ref[...] = acc


def _cmp_bias(pe, w1):
    return pl.pallas_call(
        _cmp_bias_kernel,
        grid=(2,),
        in_specs=[pl.BlockSpec((None, CMP_BLOCK, HEAD_DIM), lambda p: (p, 0, 0)),
                  pl.BlockSpec((None, CMP_BLOCK, HEAD_DIM, HEAD_DIM), lambda p: (p, 0, 0, 0))],
        out_specs=pl.BlockSpec((None, 1, HEAD_DIM), lambda p: (p, 0, 0)),
        out_shape=jax.ShapeDtypeStruct((2, 1, HEAD_DIM), F32),
        compiler_params=_params("parallel"),
    )(pe, w1)


def _cmp_kernel(*refs, n_in):
    pages = refs[-5 - n_in:-5]
    wc_ref, bias_ref, w2_ref, out_ref, carry_ref = refs[-5:]
    mk = CHUNKS_PER_PAGE * n_in
    rows = N_KV_HEADS * mk

    @pl.when(pl.program_id(1) == 0)
    def _():
        carry_ref[...] = jnp.zeros_like(carry_ref)

    first = (lax.broadcasted_iota(jnp.int32, (rows, 1), 0) % mk) == 0
    if len(pages[0].shape) == 3:
        split = [pages[i][...].reshape(CHUNKS_PER_PAGE, CMP_STRIDE, 2 * N_KV_HEADS, HEAD_DIM)
                 for i in range(n_in)]
        by_slot = [[jnp.swapaxes(split[i][:, j], 0, 1) for j in range(CMP_STRIDE)] for i in range(n_in)]
        piece = lambda i, j, slot: by_slot[i][j][slot]
    else:
        r_out = lax.broadcasted_iota(jnp.int32, (PAGE_SIZE, PAGE_SIZE), 0)
        r_in = lax.broadcasted_iota(jnp.int32, (PAGE_SIZE, PAGE_SIZE), 1)
        perm = jnp.where(r_in == (r_out % CHUNKS_PER_PAGE) * CMP_STRIDE + r_out // CHUNKS_PER_PAGE, 1.0, 0.0)
        grouped = [_dot(perm.astype(BF16), pages[i][...].astype(BF16)) for i in range(n_in)]
        piece = lambda i, j, slot: grouped[i][j * CHUNKS_PER_PAGE:(j + 1) * CHUNKS_PER_PAGE,
                                              slot * HEAD_DIM:(slot + 1) * HEAD_DIM]
    for p in range(2):
        lhs = jnp.concatenate(
            [jnp.concatenate([piece(i, j, p * N_KV_HEADS + k).astype(BF16) for j in range(CMP_STRIDE)], axis=1)
             for k in range(N_KV_HEADS) for i in range(n_in)], axis=0)
        acc = _dot(lhs, wc_ref[p])
        part0 = acc[:, :HEAD_DIM]
        part1 = acc[:, HEAD_DIM:]
        prev = jnp.where(first, carry_ref[p], pltpu.roll(part0, 1, 0))
        carry_ref[p] = pltpu.roll(part0, rows - (mk - 1), 0)
        h = prev + part1 + bias_ref[p]
        c = _dot(jax.nn.gelu(h).astype(BF16), w2_ref[p])
        for k in range(N_KV_HEADS):
            col = (p * N_KV_HEADS + k) * HEAD_DIM
            out_ref[:, col:col + HEAD_DIM] = c[k * mk:(k + 1) * mk]


def _cmp_weights(w1, w2):
    n_r = CMP_BLOCK // CMP_STRIDE
    w1r = w1.reshape(2, n_r, CMP_STRIDE, HEAD_DIM, HEAD_DIM)
    wc = jnp.concatenate([w1r[:, r] for r in range(n_r)], axis=-1).astype(BF16)
    return wc.reshape(2, CMP_STRIDE * HEAD_DIM, 2 * HEAD_DIM), w2.astype(BF16)


def _cmp_call(n_seq, n_steps, n_in, page_specs, operands, wc, bias, w2b, num_prefetch, prefetch):
    mk = CHUNKS_PER_PAGE * n_in
    if num_prefetch:
        fixed3 = lambda s, t, pt: (0, 0, 0)
        out_map = lambda s, t, pt: (s, t, 0)
    else:
        fixed3 = lambda s, t: (0, 0, 0)
        out_map = lambda s, t: (s, t, 0)
    grid_spec = pltpu.PrefetchScalarGridSpec(
        num_scalar_prefetch=num_prefetch,
        grid=(n_seq, n_steps),
        in_specs=page_specs + [
            pl.BlockSpec((2, CMP_STRIDE * HEAD_DIM, 2 * HEAD_DIM), fixed3),
            pl.BlockSpec((2, 1, HEAD_DIM), fixed3),
            pl.BlockSpec((2, HEAD_DIM, HEAD_DIM), fixed3)],
        out_specs=pl.BlockSpec((None, mk, 2 * NKV_COLS), out_map),
        scratch_shapes=[pltpu.VMEM((2, N_KV_HEADS * mk, HEAD_DIM), F32)],
    )
    return pl.pallas_call(
        functools.partial(_cmp_kernel, n_in=n_in),
        grid_spec=grid_spec,
        out_shape=jax.ShapeDtypeStruct((n_seq, n_steps * mk, 2 * NKV_COLS), F32),
        compiler_params=_params("parallel", "arbitrary"),
    )(*prefetch, *operands, wc, bias, w2b)


CMP_PAGES_PER_STEP = 8


def _cmp_prompt(y_main, bsz, t, wc, bias, w2b):
    pages_per_seq = t // PAGE_SIZE
    n_in = math.gcd(pages_per_seq, CMP_PAGES_PER_STEP)
    n_steps = pages_per_seq // n_in
    col_blk = NQ_COLS // (2 * NKV_COLS)
    specs = [pl.BlockSpec((PAGE_SIZE, 2 * NKV_COLS),
                          functools.partial(lambda s, st, i: (s * pages_per_seq + st * n_in + i, col_blk), i=i))
             for i in range(n_in)]
    return _cmp_call(bsz, n_steps, n_in, specs, [y_main] * n_in, wc, bias, w2b, 0, ())


def _cmp_sample(cache5, layer, page_table, wc, bias, w2b):
    n_in = math.gcd(page_table.shape[1], CMP_PAGES_PER_STEP)
    db, n_pages = page_table.shape
    n_steps = n_pages // n_in
    specs = [pl.BlockSpec((None, None, PAGE_SIZE, 2 * N_KV_HEADS, HEAD_DIM),
                          functools.partial(lambda s, st, pt, i: (layer, pt[s, st * n_in + i], 0, 0, 0), i=i))
             for i in range(n_in)]
    return _cmp_call(db, n_steps, n_in, specs, [cache5] * n_in, wc, bias, w2b, 1, (page_table,))


def _overlap_shifted(n_rows, n_c, n_s, n_cols):
    m = np.arange(n_rows)[:, None]
    j = np.arange(n_cols)[None, :]
    c0 = (m - 1) * CMP_STRIDE
    s0 = j * SEL_BLOCK
    ov = (c0 < s0 + SEL_BLOCK) & (c0 + CMP_BLOCK > s0) & (m >= 1) & (m <= n_c) & (j < n_s)
    return jnp.asarray(ov.astype(np.float32))


SEL_KEY_TILE = 512


def _softmax_cols(s, mask):
    s = jnp.where(mask, s, NEG_INF)
    m = jnp.max(s, axis=0, keepdims=True)
    p = jnp.where(mask, jnp.exp2(s - m), 0.0)
    return p * (1.0 / jnp.maximum(jnp.sum(p, axis=0, keepdims=True), TINY))


def _select_blocks_cols(imp, qpos, n_s):
    n_rows = imp.shape[0]
    j = lax.broadcasted_iota(jnp.int32, (n_rows, 1), 0)
    cur = qpos // SEL_BLOCK
    forced = (j == 0) | (j == cur) | (j == cur - 1)
    valid = j * SEL_BLOCK <= qpos
    score = jnp.where(forced, FORCE, jnp.where(valid, imp, -FORCE))
    score = jnp.where(j < n_s, score, -jnp.inf)
    sub = 8
    blocks = [score[b * sub:(b + 1) * sub] for b in range(-(-n_s // sub))]
    ranks = [jnp.zeros(blk.shape, F32) for blk in blocks]
    row = lax.broadcasted_iota(jnp.int32, (sub, 1), 0)
    for i in range(n_s):
        bi, ri = divmod(i, sub)
        si = blocks[bi][ri:ri + 1, :]
        for b, blk in enumerate(blocks):
            if b < bi:
                beats = si > blk
            elif b > bi:
                beats = si >= blk
            else:
                beats = jnp.where(row > ri, jnp.where(si >= blk, 1.0, 0.0), jnp.where(si > blk, 1.0, 0.0)) > 0.5
            ranks[b] = ranks[b] + jnp.where(beats, 1.0, 0.0)
    rank = jnp.concatenate(ranks, axis=0)
    sel = jnp.where(rank < float(min(N_SEL, n_s)), 1.0, 0.0)
    if sel.shape[0] < n_rows:
        sel = jnp.concatenate([sel, jnp.zeros((n_rows - sel.shape[0], sel.shape[1]), F32)], axis=0)
    return sel


def _nsa_prompt_kernel(q_ref, kc_ref, vct_ref, ks_ref, vst_ref, kw_ref, vwt_ref, g_ref, ovt_ref, rest_ref, o_ref,
                       selbias_ref, *, t):
    del rest_ref
    start = pl.program_id(2) * Q_BLOCK
    n_cols = GROUP * Q_BLOCK
    q = q_ref[...] * (HEAD_DIM ** -0.5 * math.log2(math.e))
    q4 = jnp.concatenate([q[:, g * HEAD_DIM:(g + 1) * HEAD_DIM] for g in range(GROUP)], axis=0).astype(BF16)
    qpos1 = start + lax.broadcasted_iota(jnp.int32, (1, Q_BLOCK), 1)
    qpos4 = start + lax.broadcasted_iota(jnp.int32, (1, n_cols), 1) % Q_BLOCK

    n_cr = kc_ref.shape[0]
    n_s = ovt_ref.shape[0]
    s = _dot_nt(kc_ref[...].astype(BF16), q4)
    m_idx = lax.broadcasted_iota(jnp.int32, (n_cr, 1), 0)
    cmask = (m_idx >= 1) & (m_idx * CMP_STRIDE + (CMP_BLOCK - CMP_STRIDE - 1) <= qpos4)
    p_c = _softmax_cols(s, cmask)
    o_c = _dot(vct_ref[...].astype(BF16), p_c.astype(BF16))
    psum = p_c[:, 0:Q_BLOCK]
    for g in range(1, GROUP):
        psum = psum + p_c[:, g * Q_BLOCK:(g + 1) * Q_BLOCK]
    imp = _dot(ovt_ref[...], psum, HI)
    sel = _select_blocks_cols(imp, qpos1, n_s)

    wlen = WINDOW + Q_BLOCK
    w0 = pl.multiple_of(jnp.clip(start - WINDOW, 0, t - wlen), PAGE_SIZE)
    sw = _dot_nt(kw_ref[pl.ds(w0, wlen), :], q4)
    dpos = qpos4 - (w0 + lax.broadcasted_iota(jnp.int32, (wlen, 1), 0))
    p_w = _softmax_cols(sw, (dpos >= 0) & (dpos < WINDOW))
    wblk = w0 // PAGE_SIZE
    vwt = jnp.concatenate([vwt_ref[wblk + i] for i in range(wlen // PAGE_SIZE)], axis=1)
    o_w = _dot(vwt, p_w.astype(BF16))

    tk = SEL_KEY_TILE
    blocks_per_tile = tk // SEL_BLOCK
    selbias_ref[...] = jnp.where(sel > 0.5, 0.0, NEG_INF)

    heads_per_pass = 2
    assert GROUP % heads_per_pass == 0
    width = heads_per_pass * Q_BLOCK

    def tile(kt, carry, last):
        m, l, acc = carry
        off = kt * tk
        k_t = ks_ref[pl.ds(pl.multiple_of(kt * tk, tk), tk), :]
        rows = selbias_ref[pl.ds(pl.multiple_of(kt * blocks_per_tile, blocks_per_tile), blocks_per_tile), :]
        bias = jnp.concatenate([jnp.broadcast_to(rows[i:i + 1, :], (SEL_BLOCK, Q_BLOCK))
                                for i in range(blocks_per_tile)], axis=0)
        if last:
            kpos = off + lax.broadcasted_iota(jnp.int32, (tk, Q_BLOCK), 0)
            bias = jnp.where(kpos <= qpos1, bias, NEG_INF)
        bias = jnp.concatenate([bias] * heads_per_pass, axis=1)
        vblk = kt * (tk // PAGE_SIZE)
        vst = jnp.concatenate([vst_ref[vblk + i] for i in range(tk // PAGE_SIZE)], axis=1)
        ms, ls, accs = [], [], []
        for h in range(GROUP // heads_per_pass):
            cs = slice(h * width, (h + 1) * width)
            sc = _dot_nt(k_t, q4[h * width:(h + 1) * width]) + bias
            m_new = jnp.maximum(m[:, cs], jnp.max(sc, axis=0, keepdims=True))
            alpha = jnp.exp2(m[:, cs] - m_new)
            p = jnp.exp2(sc - m_new)
            ls.append(alpha * l[:, cs] + jnp.sum(p, axis=0, keepdims=True))
            accs.append(alpha * acc[:, cs] + _dot(vst, p.astype(BF16)))
            ms.append(m_new)
        return jnp.concatenate(ms, axis=1), jnp.concatenate(ls, axis=1), jnp.concatenate(accs, axis=1)

    n_tiles = (start + Q_BLOCK + tk - 1) // tk
    m0 = jnp.full((1, n_cols), NEG_INF, F32)
    l0 = jnp.zeros((1, n_cols), F32)
    a0 = jnp.zeros((HEAD_DIM, n_cols), F32)
    carry = lax.fori_loop(0, n_tiles - 1, functools.partial(tile, last=False), (m0, l0, a0))
    _, l, acc = tile(n_tiles - 1, carry, last=True)
    o_s = acc * (1.0 / jnp.maximum(l, TINY))

    gt = g_ref[...]
    o = gt[0:1, :] * o_c + gt[1:2, :] * o_s + gt[2:3, :] * o_w
    o_ref[...] = jnp.concatenate([o[:, g * Q_BLOCK:(g + 1) * Q_BLOCK].T for g in range(GROUP)],
                                 axis=1).astype(o_ref.dtype)


def _values_t(y_b, bsz, t, col0):
    v = lax.slice(y_b, (0, col0), (bsz * t, col0 + NKV_COLS))
    v = v.reshape(bsz, t // PAGE_SIZE, PAGE_SIZE, N_KV_HEADS, HEAD_DIM)
    return v.transpose(0, 3, 1, 4, 2)


def _nsa_prompt(y_main, y_b, cmp_p, gates, bsz, t, rest):
    assert t % SEL_KEY_TILE == 0 and t >= WINDOW + Q_BLOCK and SEL_KEY_TILE % PAGE_SIZE == 0
    np_tok = bsz * t
    nq = t // Q_BLOCK
    n_cr = t // CMP_STRIDE
    n_c = (t - CMP_BLOCK) // CMP_STRIDE + 1
    n_s = t // SEL_BLOCK
    assert n_s >= N_SEL and n_s % 8 == 0
    ovt = _overlap_shifted(n_cr, n_c, n_s, n_s).T
    n_pg = t // PAGE_SIZE
    v_sel_t = _values_t(y_b, bsz, t, NQ_COLS + 3 * NKV_COLS)
    v_win_t = _values_t(y_b, bsz, t, NQ_COLS + 5 * NKV_COLS)
    cmp_t = cmp_p.transpose(0, 2, 1)
    cb = NQ_COLS // HEAD_DIM
    g_t = gates[:np_tok].reshape(bsz, nq, Q_BLOCK, N_KV_HEADS, GROUP, 3).transpose(0, 3, 1, 5, 4, 2)
    g_t = g_t.reshape(bsz, N_KV_HEADS, nq, 3, GROUP * Q_BLOCK)
    seq_key = lambda c4: pl.BlockSpec((t, HEAD_DIM), lambda b, k, i: (b, cb + c4 * N_KV_HEADS + k))
    seq_val = pl.BlockSpec((None, None, n_pg, HEAD_DIM, PAGE_SIZE), lambda b, k, i: (b, k, 0, 0, 0))
    return pl.pallas_call(
        functools.partial(_nsa_prompt_kernel, t=t),
        grid=(bsz, N_KV_HEADS, nq),
        in_specs=[pl.BlockSpec((Q_BLOCK, QW), lambda b, k, i: (b * nq + i, k)),
                  pl.BlockSpec((None, n_cr, HEAD_DIM), lambda b, k, i: (b, 0, k)),
                  pl.BlockSpec((None, HEAD_DIM, n_cr), lambda b, k, i: (b, N_KV_HEADS + k, 0)),
                  seq_key(2), seq_val, seq_key(4), seq_val,
                  pl.BlockSpec((None, None, None, 3, GROUP * Q_BLOCK), lambda b, k, i: (b, k, i, 0, 0)),
                  pl.BlockSpec((n_s, n_cr), lambda b, k, i: (0, 0)),
                  pl.BlockSpec(memory_space=pl.ANY)],
        out_specs=pl.BlockSpec((Q_BLOCK, QW), lambda b, k, i: (b * nq + i, k)),
        out_shape=jax.ShapeDtypeStruct(rest.shape, rest.dtype),
        input_output_aliases={9: 0},
        scratch_shapes=[pltpu.VMEM((n_s, Q_BLOCK), F32)],
        compiler_params=_params("parallel", "parallel", "arbitrary"),
    )(y_main, cmp_p, cmp_t, y_b, v_sel_t, y_b, v_win_t, g_t, ovt, rest)


SAMPLE_ROWS = 8


def _nsa_sample_kernel(pt_ref, y_ref, cmp_ref, cache_ref, win_ref, g_ref, ov_ref, o_ref,
                       q_scr, sel_scr, m_scr, l_scr, acc_scr, oc_scr, kv_buf, kv_sem,
                       *, layer, past, s_len, n_s, n_in):
    pg = pl.program_id(1)
    n_pages = pl.num_programs(1)
    step = pl.program_id(0) * n_pages + pg
    cur = step % 2

    def page_copies(seq, grp, buf):
        out = []
        for i in range(n_in):
            page = pt_ref[seq, grp * n_in + i]
            for s in range(2 * N_KV_HEADS):
                out.append(pltpu.make_async_copy(
                    cache_ref.at[layer, page, :, 2 * N_KV_HEADS + s, :],
                    kv_buf.at[buf, s, pl.ds(i * PAGE_SIZE, PAGE_SIZE), :],
                    kv_sem.at[buf]))
        return out

    @pl.when(step == 0)
    def _():
        for cp in page_copies(0, 0, 0):
            cp.start()

    @pl.when(step + 1 < pl.num_programs(0) * n_pages)
    def _():
        nxt = step + 1
        for cp in page_copies(nxt // n_pages, nxt % n_pages, 1 - cur):
            cp.start()

    for cp in page_copies(pl.program_id(0), pg, cur):
        cp.wait()

    sr = SAMPLE_ROWS
    rk = GROUP * sr
    scale = HEAD_DIM ** -0.5
    tpos = lax.broadcasted_iota(jnp.int32, (rk, 1), 0) % sr
    qpos = past + tpos
    n_sc = sel_scr.shape[1]
    nbp = past // SEL_BLOCK
    kv0 = NQ_COLS

    @pl.when(pg == 0)
    def _():
        y = y_ref[...]
        n_cr = cmp_ref.shape[0]
        m_idx = lax.broadcasted_iota(jnp.int32, (1, n_cr), 1)
        cmask = (m_idx >= 1) & (m_idx * CMP_STRIDE + (CMP_BLOCK - CMP_STRIDE - 1) <= qpos)
        psums = []
        for k in range(N_KV_HEADS):
            qk = jnp.concatenate([y[:, (k * GROUP + g) * HEAD_DIM:(k * GROUP + g + 1) * HEAD_DIM]
                                  for g in range(GROUP)], axis=0) * scale
            qk = qk.astype(BF16)
            q_scr[k] = qk
            kc = cmp_ref[:, k * HEAD_DIM:(k + 1) * HEAD_DIM].astype(BF16)
            vc = cmp_ref[:, (N_KV_HEADS + k) * HEAD_DIM:(N_KV_HEADS + k + 1) * HEAD_DIM].astype(BF16)
            p_c = _masked_softmax(_dot_nt(qk, kc), cmask)
            oc_scr[k * rk:(k + 1) * rk, :] = _dot(p_c.astype(BF16), vc)
            psum = p_c[0:sr]
            for g in range(1, GROUP):
                psum = psum + p_c[g * sr:(g + 1) * sr]
            psums.append(psum)
        lanes = VREG_LANES
        psum_all = jnp.concatenate(psums + [jnp.zeros((lanes - N_KV_HEADS * sr, psums[0].shape[1]), F32)], axis=0)
        imp_t = _dot_nt(ov_ref[...], psum_all, HI)
        qpos_sel = past + lax.broadcasted_iota(jnp.int32, (1, lanes), 1) % sr
        sel = _select_blocks_cols(imp_t, qpos_sel, n_s).T
        for k in range(N_KV_HEADS):
            for g in range(GROUP):
                sel_scr[k * rk + g * sr:k * rk + (g + 1) * sr, :] = sel[k * sr:(k + 1) * sr]
        m_scr[...] = jnp.full(m_scr.shape, NEG_INF, F32)
        l_scr[...] = jnp.zeros(l_scr.shape, F32)
        acc_scr[...] = jnp.zeros(acc_scr.shape, F32)

    def online_update(k, sc, mask, v):
        rows = slice(k * rk, (k + 1) * rk)
        sc = jnp.where(mask, sc, NEG_INF)
        m_old = m_scr[rows, :]
        m_new = jnp.maximum(m_old, jnp.max(sc, axis=-1, keepdims=True))
        alpha = jnp.exp(m_old - m_new)
        p = jnp.where(mask, jnp.exp(sc - m_new), 0.0)
        l_scr[rows, :] = alpha * l_scr[rows, :] + jnp.sum(p, axis=-1, keepdims=True)
        acc_scr[rows, :] = alpha * acc_scr[rows, :] + _dot(p.astype(v.dtype), v)
        m_scr[rows, :] = m_new

    n_keys = n_in * PAGE_SIZE
    kpos = pg * n_keys + lax.broadcasted_iota(jnp.int32, (1, n_keys), 1)
    blk_row = lax.broadcasted_iota(jnp.int32, (n_sc, 1), 0)
    expand = jnp.where(blk_row == kpos // SEL_BLOCK, 1.0, 0.0).astype(BF16)
    selx = _dot(sel_scr[...].astype(BF16), expand)
    for k in range(N_KV_HEADS):
        kp = kv_buf[cur, k].astype(BF16)
        vp = kv_buf[cur, N_KV_HEADS + k].astype(BF16)
        mask = (selx[k * rk:(k + 1) * rk] > 0.5) & (kpos <= qpos)
        online_update(k, _dot_nt(q_scr[k], kp), mask, vp)

    @pl.when(pg == n_pages - 1)
    def _():
        y = y_ref[...]
        rpos = lax.broadcasted_iota(jnp.int32, (1, sr), 1)
        new_ok = (rpos < s_len) & (rpos <= tpos)
        gt = g_ref[...]
        win_by_slot = jnp.swapaxes(win_ref[...], 0, 1).astype(BF16)
        for k in range(N_KV_HEADS):
            rows = slice(k * rk, (k + 1) * rk)
            qk = q_scr[k]
            qf = qk.astype(F32)
            c_ks = kv0 + (2 * N_KV_HEADS + k) * HEAD_DIM
            c_vs = kv0 + (3 * N_KV_HEADS + k) * HEAD_DIM
            kn = y[:, c_ks:c_ks + HEAD_DIM]
            vn = y[:, c_vs:c_vs + HEAD_DIM]
            new_sel = sel_scr[rows, nbp:nbp + 1] > 0.5
            online_update(k, _dot_nt(qf, kn), new_sel & new_ok, vn)
            o_s = acc_scr[rows, :] / jnp.maximum(l_scr[rows, :], TINY)
            wb = win_ref.shape[0]
            c_kw = kv0 + (4 * N_KV_HEADS + k) * HEAD_DIM
            c_vw = kv0 + (5 * N_KV_HEADS + k) * HEAD_DIM
            kwb = win_by_slot[k]
            vwb = win_by_slot[N_KV_HEADS + k]
            kwn = y[:, c_kw:c_kw + HEAD_DIM]
            vwn = y[:, c_vw:c_vw + HEAD_DIM]
            dpos = qpos - (past - wb + lax.broadcasted_iota(jnp.int32, (1, wb), 1))
            mask_b = (dpos >= 0) & (dpos < WINDOW)
            s_b = jnp.where(mask_b, _dot_nt(qk, kwb), NEG_INF)
            s_n = jnp.where(new_ok, _dot_nt(qf, kwn), NEG_INF)
            mw = jnp.maximum(jnp.max(s_b, axis=-1, keepdims=True), jnp.max(s_n, axis=-1, keepdims=True))
            p_b = jnp.where(mask_b, jnp.exp(s_b - mw), 0.0)
            p_n = jnp.where(new_ok, jnp.exp(s_n - mw), 0.0)
            den = jnp.sum(p_b, axis=-1, keepdims=True) + jnp.sum(p_n, axis=-1, keepdims=True)
            o_w = (_dot(p_b.astype(BF16), vwb) + _dot(p_n, vwn)) / jnp.maximum(den, TINY)
            o_c = oc_scr[rows, :]
            for g in range(GROUP):
                r0, r1 = g * sr, (g + 1) * sr
                c = (k * GROUP + g) * 3
                o = gt[:, c:c + 1] * o_c[r0:r1] + gt[:, c + 1:c + 2] * o_s[r0:r1] + gt[:, c + 2:c + 3] * o_w[r0:r1]
                o_ref[:, (k * GROUP + g) * HEAD_DIM:(k * GROUP + g + 1) * HEAD_DIM] = o


SAMPLE_PAGES_PER_STEP = 8


def _nsa_sample(ys8, cmp_s, cache5, layer, win4, gates8, page_table, s_len):
    db, n_pages = page_table.shape
    past = n_pages * PAGE_SIZE
    n_in = math.gcd(n_pages, SAMPLE_PAGES_PER_STEP)
    assert s_len <= SAMPLE_ROWS and s_len <= SEL_BLOCK and past % SEL_BLOCK == 0
    n_cr = cmp_s.shape[1]
    n_c = (past + s_len - CMP_BLOCK) // CMP_STRIDE + 1
    assert n_c + 1 <= n_cr
    n_s = -(-(past + s_len) // SEL_BLOCK)
    n_sc = -(-n_s // VREG_LANES) * VREG_LANES
    ov = _overlap_shifted(n_cr, n_c, n_s, n_sc).T
    wb = win4.shape[1]
    ncol = ys8.shape[2]
    rows = N_KV_HEADS * GROUP * SAMPLE_ROWS
    grid_spec = pltpu.PrefetchScalarGridSpec(
        num_scalar_prefetch=1,
        grid=(db, n_pages // n_in),
        in_specs=[pl.BlockSpec((None, SAMPLE_ROWS, ncol), lambda b, p, pt: (b, 0, 0)),
                  pl.BlockSpec((None, n_cr, 2 * NKV_COLS), lambda b, p, pt: (b, 0, 0)),
                  pl.BlockSpec(memory_space=pl.ANY),
                  pl.BlockSpec((None, wb, 2 * N_KV_HEADS, HEAD_DIM), lambda b, p, pt: (b, 0, 0, 0)),
                  pl.BlockSpec((None, SAMPLE_ROWS, 3 * N_HEADS), lambda b, p, pt: (b, 0, 0)),
                  pl.BlockSpec((n_sc, n_cr), lambda b, p, pt: (0, 0))],
        out_specs=pl.BlockSpec((None, SAMPLE_ROWS, NQ_COLS), lambda b, p, pt: (b, 0, 0)),
        scratch_shapes=[pltpu.VMEM((N_KV_HEADS, GROUP * SAMPLE_ROWS, HEAD_DIM), BF16),
                        pltpu.VMEM((rows, n_sc), F32),
                        pltpu.VMEM((rows, 1), F32),
                        pltpu.VMEM((rows, 1), F32),
                        pltpu.VMEM((rows, HEAD_DIM), F32),
                        pltpu.VMEM((rows, HEAD_DIM), F32),
                        pltpu.VMEM((2, 2 * N_KV_HEADS, n_in * PAGE_SIZE, HEAD_DIM), F32),
                        pltpu.SemaphoreType.DMA((2,))],
    )
    return pl.pallas_call(
        functools.partial(_nsa_sample_kernel, layer=layer, past=past, s_len=s_len, n_s=n_s, n_in=n_in),
        grid_spec=grid_spec,
        out_shape=jax.ShapeDtypeStruct((db, SAMPLE_ROWS, NQ_COLS), F32),
        compiler_params=_params("arbitrary", "arbitrary"),
    )(page_table, ys8, cmp_s, cache5, win4, gates8, ov)


WKV_DIAG_BLOCK = 16
WKV_CHUNK = 64


def _bdot(a, b):
    return _dot(a.astype(BF16), b.astype(BF16))


def _bdot_nt(a, b):
    return _dot_nt(a.astype(BF16), b.astype(BF16))


def _bdot_tn(a, b):
    return _dot_tn(a.astype(BF16), b.astype(BF16))


def _unit_lower_inverses(mats, c):
    blk = min(WKV_DIAG_BLOCK, c)
    ri = lax.broadcasted_iota(jnp.int32, (c, c), 0)
    ci = lax.broadcasted_iota(jnp.int32, (c, c), 1)
    eye = jnp.where(ri == ci, 1.0, 0.0)
    same = (ri // blk) == (ci // blk)
    d = [jnp.where(same, a, 0.0) for a in mats]
    td = [eye + x for x in d]
    n_sq = int(math.log2(blk)) - 1
    pw = [_bdot(x, x) for x in d]
    stack = c >= WKV_DIAG_BLOCK
    for level in range(n_sq):
        if level + 1 < n_sq and stack:
            both = [_bdot(jnp.concatenate([p, t], axis=0), p) for p, t in zip(pw, td)]
            pw = [x[:c] for x in both]
            td = [t + x[c:] for t, x in zip(td, both)]
        else:
            td = [t + _bdot(t, p) for t, p in zip(td, pw)]
            if level + 1 < n_sq:
                pw = [_bdot(p, p) for p in pw]
    if blk == c:
        return td
    npow = [_bdot(t, jnp.where(same, 0.0, a)) for t, a in zip(td, mats)]
    tinv = [t + _bdot(x, t) for x, t in zip(npow, td)]
    for _ in range(int(math.log2(c // blk)) - 1):
        npow = [_bdot(x, x) for x in npow]
        tinv = [t + _bdot(x, t) for x, t in zip(npow, tinv)]
    return tinv


def _wkv_kernel(r_ref, k_ref, v_ref, lw_ref, a_ref, g_ref, kk_ref, ka_ref, rk_ref, lg_ref, lb_ref, s0_ref,
                rest_ref, z_ref, s_ref, *, c, nc, hb):
    del rest_ref
    n = RWKV_HEAD

    @pl.when(pl.program_id(2) == 0)
    def _():
        s_ref[...] = s0_ref[...]

    ri = lax.broadcasted_iota(jnp.int32, (c, c), 0)
    ci = lax.broadcasted_iota(jnp.int32, (c, c), 1)
    strict = ri > ci
    incl = ri >= ci
    tri = jnp.where(incl, 1.0, 0.0)
    lw_all = lw_ref[...]
    cum = jnp.concatenate([_dot(tri, lw_all[i * c:(i + 1) * c], HI) for i in range(nc)], axis=0)
    e_in_all = jnp.exp(cum)
    e_neg_all = jnp.exp(-cum)
    e_ex_all = jnp.exp(cum - lw_all)
    heads = range(hb)
    chunks = range(nc)
    sls = [slice(hh * n, (hh + 1) * n) for hh in heads]
    rws = [slice(i * c, (i + 1) * c) for i in chunks]
    r, v, k2, r_t, a_t, b_t, k_t, vb = ([] for _ in range(8))
    for sl in sls:
        k = k_ref[:, sl]
        a = a_ref[:, sl]
        kk = k * kk_ref[:, sl]
        kk = kk * (1.0 / jnp.maximum(jnp.sqrt(jnp.sum(kk * kk, axis=-1, keepdims=True)), 1e-12))
        r.append(r_ref[:, sl])
        v.append(v_ref[:, sl])
        k2.append(k * (1.0 + (a - 1.0) * ka_ref[:, sl]))
        r_t.append(r[-1] * e_in_all[:, sl])
        a_t.append(-kk * e_ex_all[:, sl])
        b_t.append(kk * a * e_neg_all[:, sl])
        k_t.append(k2[-1] * e_neg_all[:, sl])
        vb.append(v[-1].astype(BF16))

    items = [(hh, i) for hh in heads for i in chunks]
    aa = [_bdot_nt(jnp.concatenate([a_t[hh][rws[i]], r_t[hh][rws[i]]], axis=0),
                   jnp.concatenate([b_t[hh][rws[i]], k_t[hh][rws[i]]], axis=0)) for hh, i in items]
    r2 = lax.broadcasted_iota(jnp.int32, (2 * c, c), 0)
    c2 = lax.broadcasted_iota(jnp.int32, (2 * c, c), 1)
    tri2 = jnp.where(r2 < c, r2, r2 - c + 1) > c2
    a_ab = [jnp.where(strict, x[:c, :c], 0.0) for x in aa]
    a_rb = [jnp.where(incl, x[c:, :c], 0.0).astype(BF16) for x in aa]
    xkv = [_bdot(jnp.where(tri2, x[:, c:], 0.0), vb[hh][rws[i]]) for x, (hh, i) in zip(aa, items)]
    yv = [x[c:] for x in xkv]
    w_c = [e_in_all[i * c + c - 1:(i + 1) * c, sls[hh]] for hh, i in items]
    kv = [_bdot_tn(vb[hh][rws[i]], k_t[hh][rws[i]] * w) for w, (hh, i) in zip(w_c, items)]
    bh = [(b_t[hh][rws[i]] * w).astype(BF16) for w, (hh, i) in zip(w_c, items)]
    tinv = _unit_lower_inverses(a_ab, c)
    wt = [_bdot(t, a_t[hh][rws[i]]) for t, (hh, i) in zip(tinv, items)]
    u = [_bdot(t, x[:c]) for t, x in zip(tinv, xkv)]
    wr = [jnp.concatenate([w, r_t[hh][rws[i]]], axis=0).astype(BF16) for w, (hh, i) in zip(wt, items)]

    s = [s_ref[hh] for hh in heads]
    ys = [[] for _ in heads]
    for i in chunks:
        idx = [hh * nc + i for hh in heads]
        ls = [_dot_nt(wr[j], s[hh].astype(BF16)) for hh, j in zip(heads, idx)]
        p = [x[:c] + u[j] for x, j in zip(ls, idx)]
        for hh, j in zip(heads, idx):
            ys[hh].append(ls[hh][c:] + _bdot(a_rb[j], p[hh]) + yv[j])
        s = [s[hh] * w_c[j] + _bdot_tn(p[hh], bh[j]) + kv[j] for hh, j in zip(heads, idx)]
    zs = []
    for hh, sl in zip(heads, sls):
        s_ref[hh] = s[hh]
        y = ys[hh][0] if nc == 1 else jnp.concatenate(ys[hh], axis=0)
        mu = jnp.mean(y, axis=-1, keepdims=True)
        yc = y - mu
        yn = yc * lax.rsqrt(jnp.mean(yc * yc, axis=-1, keepdims=True) + LNX_EPS)
        yn = yn * lg_ref[:, sl] + lb_ref[:, sl]
        bonus = jnp.sum(r[hh] * k2[hh] * rk_ref[:, sl], axis=-1, keepdims=True) * v[hh]
        zs.append((yn + bonus) * g_ref[:, sl])
    z_ref[...] = jnp.concatenate(zs, axis=1).astype(z_ref.dtype)


def _wkv(rkv, lw, a, g, k_k, k_a, r_k, lnx_g, lnx_b, s0, rest, n_seq, t, c, nc, hb):
    rows, d = lw.shape
    h = d // RWKV_HEAD
    hw = hb * RWKV_HEAD
    tb = nc * c
    nblk = t // tb
    tok = lambda s, hg, ch: (s * nblk + ch, hg)
    par = lambda s, hg, ch: (0, hg)
    st = lambda s, hg, ch: (s, hg, 0, 0)
    rkv_spec = lambda p: pl.BlockSpec((None, tb, hw), lambda s, hg, ch: (p, s * nblk + ch, hg))
    return pl.pallas_call(
        functools.partial(_wkv_kernel, c=c, nc=nc, hb=hb),
        grid=(n_seq, h // hb, nblk),
        in_specs=[rkv_spec(0), rkv_spec(1), rkv_spec(2),
                  pl.BlockSpec((tb, hw), tok), pl.BlockSpec((tb, hw), tok), pl.BlockSpec((tb, hw), tok),
                  pl.BlockSpec((1, hw), par), pl.BlockSpec((1, hw), par), pl.BlockSpec((1, hw), par),
                  pl.BlockSpec((1, hw), par), pl.BlockSpec((1, hw), par),
                  pl.BlockSpec((None, hb, RWKV_HEAD, RWKV_HEAD), st),
                  pl.BlockSpec(memory_space=pl.ANY)],
        out_specs=[pl.BlockSpec((tb, hw), tok),
                   pl.BlockSpec((None, hb, RWKV_HEAD, RWKV_HEAD), st)],
        out_shape=[jax.ShapeDtypeStruct((rows, d), BF16),
                   jax.ShapeDtypeStruct((n_seq, h, RWKV_HEAD, RWKV_HEAD), F32)],
        input_output_aliases={12: 0},
        compiler_params=_params("parallel", "parallel", "arbitrary"),
    )(rkv, rkv, rkv, lw, a, g, k_k, k_a, r_k, lnx_g, lnx_b, s0, rest)


WKV_PROMPT_CHUNKS_PER_STEP = 2
WKV_PROMPT_HEADS_PER_STEP = 8


def _to_slots_kernel(x_ref, o_ref):
    x = x_ref[...]
    by_slot = jnp.stack([x[:, s * HEAD_DIM:(s + 1) * HEAD_DIM] for s in range(o_ref.shape[1])], axis=0)
    o_ref[...] = jnp.swapaxes(by_slot, 0, 1)


def _to_slots(y, rows, col0, n_slots, *, tm_pref=512):
    width = n_slots * HEAD_DIM
    assert col0 % width == 0
    tm = _tile(rows, tm_pref)
    return pl.pallas_call(
        _to_slots_kernel,
        grid=(rows // tm,),
        in_specs=[pl.BlockSpec((tm, width), lambda i: (i, col0 // width))],
        out_specs=pl.BlockSpec((tm, n_slots, HEAD_DIM), lambda i: (i, 0, 0)),
        out_shape=jax.ShapeDtypeStruct((rows, n_slots, HEAD_DIM), y.dtype),
        compiler_params=_params("parallel"),
    )(y)


def _nsa_layer(x, np_tok, bsz, t, db, s_len, cache5, layer, win, page_table, w_in, pe, w1, w2, w_out,
               ln_g, ln_b, alpha):
    n_main = NQ_COLS + 6 * NKV_COLS
    y_main, y_b = _mm(x, w_in, layer, n_main, out_dtypes=(F32, BF16), tm_pref=832, tn_pref=1024)
    w_gate = jnp.pad(w_in[layer:layer + 1, :, n_main:], ((0, 0), (0, 0), (0, VREG_LANES - 3 * N_HEADS)))
    gates = _mm(x, w_gate, 0, VREG_LANES, act="sigmoid")[:, :3 * N_HEADS]

    wc, w2b = _cmp_weights(w1, w2)
    bias = _cmp_bias(pe, w1)
    cmp_p = _cmp_prompt(y_main, bsz, t, wc, bias, w2b)
    cmp_s = _cmp_sample(cache5, layer, page_table, wc, bias, w2b)

    pad_rows = ((0, 0), (0, SAMPLE_ROWS - s_len), (0, 0))
    ys = y_main[np_tok:].reshape(db, s_len, n_main)
    ys8 = jnp.pad(ys, pad_rows)
    gates8 = jnp.pad(gates[np_tok:].reshape(db, s_len, 3 * N_HEADS), pad_rows)
    wb = win.shape[1]
    win4 = win.reshape(db, wb, 2 * N_KV_HEADS, HEAD_DIM)
    o_s = _nsa_sample(ys8, cmp_s, cache5, layer, win4, gates8, page_table, s_len)
    o_s = o_s[:, :s_len].reshape(db * s_len, NQ_COLS).astype(BF16)

    o = _nsa_prompt(y_main, y_b, cmp_p, gates, bsz, t, jnp.pad(o_s, ((np_tok, 0), (0, 0))))
    h = _mm_ln_cols(o, w_out, layer, x, ln_g, ln_b, alpha)

    kv0 = NQ_COLS
    kw0 = NQ_COLS + 4 * NKV_COLS
    kv_p = _to_slots(y_main, np_tok, kv0, 4 * N_KV_HEADS).reshape(bsz, t, 4, N_KV_HEADS, HEAD_DIM)
    wlen = min(WINDOW, t)
    win_p = jnp.stack([lax.slice(y_main, (b * t + t - wlen, kw0), ((b + 1) * t, n_main)) for b in range(bsz)])
    win_p = win_p.reshape(bsz, wlen, 2, N_KV_HEADS, HEAD_DIM)
    kv_s = ys[:, :, kv0:kw0].reshape(db, s_len, 4, N_KV_HEADS, HEAD_DIM)
    kvw_s = ys[:, :, kw0:].reshape(db, s_len, 2, N_KV_HEADS, HEAD_DIM)
    win_s = jnp.concatenate([win, kvw_s], axis=1)[:, s_len:]
    return h, kv_p, kv_s, win_p, win_s


def _rwkv_layer(x, np_tok, bsz, t, db, s_len, shift_s, wkv_s, mu, w_rkv, w0, w1, w2, a0, a1, a2, g1, g2,
                k_k, k_a, r_k, lnx_g, lnx_b, w_out, layer, ln_g, ln_b, alpha):
    d = x.shape[1]
    heads = d // RWKV_HEAD
    xprev = jnp.concatenate([jnp.zeros((1, d), x.dtype), x[:-1]], axis=0)
    xprev = xprev.at[np.arange(1, bsz) * t].set(0.0)
    xprev = xprev.at[np_tok + np.arange(db) * s_len].set(shift_s.astype(x.dtype))

    row = lambda v: v.reshape(1, -1)
    rkv = _mm_mix(x, xprev, mu[:3, None, :], w_rkv)
    lw, a, g = _loras(x, xprev, mu[3:6], w1, w2, row(w0), a1, a2, row(a0), g1, g2)

    pvec = (row(k_k), row(k_a), row(r_k), row(lnx_g), row(lnx_b))
    c_p = _tile(t, WKV_CHUNK, 8)
    nc_p = math.gcd(t // c_p, WKV_PROMPT_CHUNKS_PER_STEP)
    hb_p = math.gcd(heads, WKV_PROMPT_HEADS_PER_STEP)
    zero_state = jnp.zeros((bsz, heads, RWKV_HEAD, RWKV_HEAD), F32)
    c_s = -(-s_len // 8) * 8
    pad = lambda v: jnp.pad(v.reshape(v.shape[:-2] + (db, s_len, d)),
                            [(0, 0)] * (v.ndim - 1) + [(0, c_s - s_len), (0, 0)]
                            ).reshape(v.shape[:-2] + (db * c_s, d))
    z_s, st_s = _wkv(pad(rkv[:, np_tok:]), pad(lw[np_tok:]), pad(a[np_tok:]), pad(g[np_tok:]), *pvec,
                     wkv_s.astype(F32), jnp.zeros((db * c_s, d), BF16), db, c_s, c_s, 1, heads)
    z_s = z_s.reshape(db, c_s, d)[:, :s_len].reshape(db * s_len, d)

    z, st_p = _wkv(rkv, lw, a, g, *pvec, zero_state, jnp.pad(z_s, ((np_tok, 0), (0, 0))), bsz, t, c_p, nc_p, hb_p)
    h = _mm_ln_cols(z, w_out, layer, x, ln_g, ln_b, alpha)
    last_p = x[np.arange(1, bsz + 1) * t - 1]
    last_s = x[np_tok + np.arange(1, db + 1) * s_len - 1]
    return h, st_p, st_s, last_p, last_s


def _mlp(h, w_up, w_down, layer, ln_g, ln_b, alpha):
    d_ff = w_up.shape[2]
    u = _mm(h, w_up, layer, d_ff, act="relu2", out_dtypes=(BF16,))
    return _mm_ln(u, w_down, layer, h, ln_g, ln_b, alpha)


def kernel(x_prompt, x_sample, cache_nsa_kv, state_nsa_win, state_rwkv_wkv, state_rwkv_shift, page_table,
           nsa_w_in, nsa_cmp_pe, nsa_cmp_w1, nsa_cmp_w2, nsa_w_out,
           rwkv_mu, rwkv_w_rkv, rwkv_w0, rwkv_w1, rwkv_w2, rwkv_a0, rwkv_a1, rwkv_a2, rwkv_g1, rwkv_g2,
           rwkv_k_k, rwkv_k_a, rwkv_r_k, rwkv_lnx_g, rwkv_lnx_b, rwkv_w_out,
           ffn_w_up, ffn_w_down, ln_g, ln_b):
    bsz, t, d = x_prompt.shape
    db, s_len, _ = x_sample.shape
    depth = ffn_w_up.shape[0]
    alpha = (2 * depth) ** 0.25
    np_tok = bsz * t
    x = jnp.concatenate([x_prompt.reshape(np_tok, d), x_sample.reshape(db * s_len, d)], axis=0)
    n_l, n_pool = cache_nsa_kv.shape[:2]
    cache5 = cache_nsa_kv.reshape(n_l, n_pool, PAGE_SIZE, 4 * N_KV_HEADS, HEAD_DIM)
    page_table = page_table.astype(jnp.int32)

    kv_p, kv_s, win_p, win_s, wkv_p, wkv_s, sh_p, sh_s = ([] for _ in range(8))
    row = lambda v: v.reshape(1, -1)
    for i in range(depth):
        j = i // 2
        if i % 2 == 0:
            h, kvp_new, kvs_new, wp_new, ws_new = _nsa_layer(
                x, np_tok, bsz, t, db, s_len, cache5, j, state_nsa_win[j], page_table,
                nsa_w_in, nsa_cmp_pe[j], nsa_cmp_w1[j], nsa_cmp_w2[j], nsa_w_out,
                row(ln_g[i, 0]), row(ln_b[i, 0]), alpha)
            kv_p.append(kvp_new)
            kv_s.append(kvs_new)
            win_p.append(wp_new)
            win_s.append(ws_new)
        else:
            h, sp_new, ss_new, hp_new, hs_new = _rwkv_layer(
                x, np_tok, bsz, t, db, s_len, state_rwkv_shift[j], state_rwkv_wkv[j],
                rwkv_mu[j], rwkv_w_rkv[j], rwkv_w0[j], rwkv_w1[j], rwkv_w2[j], rwkv_a0[j], rwkv_a1[j],
                rwkv_a2[j], rwkv_g1[j], rwkv_g2[j], rwkv_k_k[j], rwkv_k_a[j], rwkv_r_k[j],
                rwkv_lnx_g[j], rwkv_lnx_b[j], rwkv_w_out, j, row(ln_g[i, 0]), row(ln_b[i, 0]), alpha)
            wkv_p.append(sp_new)
            wkv_s.append(ss_new)
            sh_p.append(hp_new)
            sh_s.append(hs_new)
        x = _mlp(h, ffn_w_up, ffn_w_down, i, row(ln_g[i, 1]), row(ln_b[i, 1]), alpha)
    return (x[:np_tok].reshape(bsz, t, d), x[np_tok:].reshape(db, s_len, d),
            jnp.stack(kv_p), jnp.stack(kv_s), jnp.stack(win_p), jnp.stack(win_s),
            jnp.stack(wkv_p), jnp.stack(wkv_s), jnp.stack(sh_p), jnp.stack(sh_s))
```
